```python
import math
import jax, jax.numpy as jnp
from jax import lax
import numpy as np

D_MODEL = 4096
BATCH = 1
SEQ = 8192
DEPTH = 1

DIFF_HEADS = 16
DIFF_HEAD_DIM = 64
DIFF_V_DIM = 2 * DIFF_HEAD_DIM
DIFF_SUBLN_EPS = 1e-5
MLA_HEADS = 16
MLA_Q_RANK = 1024
MLA_KV_RANK = 512
MLA_NOPE_DIM = 128
MLA_ROPE_DIM = 64
MLA_V_DIM = 128
MLA_QK_DIM = MLA_NOPE_DIM + MLA_ROPE_DIM
D_FF = 11008
CONV_WIDTH = 3
ROPE_THETA = 10000.0
NORM_EPS = 1e-6
Q_BLOCK = 128

DIFF_Q_W = DIFF_HEADS * 2 * DIFF_HEAD_DIM
DIFF_K_W = DIFF_HEADS * 2 * DIFF_HEAD_DIM
DIFF_V_W = DIFF_HEADS * DIFF_V_DIM
GATE_W = D_MODEL
IN_WIDTHS = (DIFF_Q_W, DIFF_K_W, DIFF_V_W, MLA_Q_RANK, MLA_KV_RANK, MLA_ROPE_DIM, GATE_W, GATE_W)
IN_W = sum(IN_WIDTHS)
IN_SPLITS = tuple(int(v) for v in np.cumsum(IN_WIDTHS)[:-1])
DIFF_OUT_W = DIFF_HEADS * DIFF_V_DIM
MLA_OUT_W = MLA_HEADS * MLA_V_DIM

kernel_name = "hybrid_diffattn_mla_convffn"


def rmsnorm(x, w, eps=NORM_EPS):
    xf = x.astype(jnp.float32)
    y = xf * lax.rsqrt(jnp.mean(xf * xf, axis=-1, keepdims=True) + eps)
    return (y * w.astype(jnp.float32)).astype(x.dtype)


def rope_cos_sin(positions, dim):
    inv_freq = ROPE_THETA ** (-jnp.arange(0, dim, 2, dtype=jnp.float32) / dim)
    ang = positions.astype(jnp.float32)[..., None] * inv_freq
    return jnp.cos(ang), jnp.sin(ang)


def apply_rope(x, cos, sin):
    xf = x.astype(jnp.float32)
    half = xf.shape[-1] // 2
    x1, x2 = xf[..., :half], xf[..., half:]
    c, s = cos[:, :, None, :], sin[:, :, None, :]
    return jnp.concatenate([x1 * c - x2 * s, x2 * c + x1 * s], axis=-1).astype(x.dtype)


def diff_attention(q, k, v, lam):
    B, S = q.shape[0], q.shape[1]
    nb = S // Q_BLOCK
    scale = DIFF_HEAD_DIM ** -0.5
    qb = q.reshape(B, nb, Q_BLOCK, DIFF_HEADS, 2, DIFF_HEAD_DIM).transpose(1, 0, 3, 4, 2, 5)
    kt = k.transpose(0, 2, 3, 1, 4)
    vt = v.transpose(0, 2, 1, 3)
    key_idx = jnp.arange(S)

    def block(args):
        q_blk, start = args
        s = jnp.einsum('bhcqd,bhckd->bhcqk', q_blk, kt).astype(jnp.float32) * scale
        mask = (start + jnp.arange(Q_BLOCK))[:, None] >= key_idx[None, :]
        s = jnp.where(mask, s, -jnp.inf)
        p = jax.nn.softmax(s, axis=-1)
        p = p[:, :, 0] - lam * p[:, :, 1]
        return jnp.einsum('bhqk,bhkd->bhqd', p.astype(vt.dtype), vt)

    o = lax.map(block, (qb, jnp.arange(nb) * Q_BLOCK))
    return o.transpose(1, 0, 3, 2, 4).reshape(B, S, DIFF_HEADS, DIFF_V_DIM)


def mla_attention(q_nope, q_rope, k_nope, k_rope, v):
    B, S = q_nope.shape[0], q_nope.shape[1]
    nb = S // Q_BLOCK
    scale = MLA_QK_DIM ** -0.5
    qn = q_nope.reshape(B, nb, Q_BLOCK, MLA_HEADS, MLA_NOPE_DIM).transpose(1, 0, 3, 2, 4)
    qr = q_rope.reshape(B, nb, Q_BLOCK, MLA_HEADS, MLA_ROPE_DIM).transpose(1, 0, 3, 2, 4)
    kn = k_nope.transpose(0, 2, 1, 3)
    vt = v.transpose(0, 2, 1, 3)
    key_idx = jnp.arange(S)

    def block(args):
        qn_blk, qr_blk, start = args
        s = (jnp.einsum('bhqd,bhkd->bhqk', qn_blk, kn).astype(jnp.float32)
             + jnp.einsum('bhqr,bkr->bhqk', qr_blk, k_rope).astype(jnp.float32)) * scale
        mask = (start + jnp.arange(Q_BLOCK))[:, None] >= key_idx[None, :]
        p = jax.nn.softmax(jnp.where(mask, s, -jnp.inf), axis=-1)
        return jnp.einsum('bhqk,bhkd->bhqd', p.astype(vt.dtype), vt)

    o = lax.map(block, (qn, qr, jnp.arange(nb) * Q_BLOCK))
    return o.transpose(1, 0, 3, 2, 4).reshape(B, S, MLA_HEADS, MLA_V_DIM)


def causal_depthwise_conv(u, w, b):
    S = u.shape[1]
    up = jnp.pad(u, ((0, 0), (CONV_WIDTH - 1, 0), (0, 0)))
    out = b
    for j in range(CONV_WIDTH):
        out = out + w[j] * up[:, j:j + S]
    return out


def setup_inputs(seed: int = 0) -> dict:
    key = jax.random.key(seed)
    ks = jax.random.split(key, 24)
    f32 = jnp.float32

    def dense(k, shape, fan_in):
        return jax.random.normal(k, shape, f32) * (fan_in ** -0.5)

    def gain(k, shape):
        return 1.0 + 0.01 * jax.random.normal(k, shape, f32)

    L = DEPTH
    x = jax.random.normal(ks[0], (BATCH, SEQ, D_MODEL), f32)
    positions = jnp.broadcast_to(jnp.arange(SEQ, dtype=jnp.int32), (BATCH, SEQ))
    return {
        "x": x,
        "positions": positions,
        "norm_mix_w": gain(ks[1], (L, D_MODEL)),
        "w_in": dense(ks[2], (L, D_MODEL, IN_W), D_MODEL),
        "diff_lambda_q1": 0.1 * jax.random.normal(ks[3], (L, DIFF_HEAD_DIM), f32),
        "diff_lambda_k1": 0.1 * jax.random.normal(ks[4], (L, DIFF_HEAD_DIM), f32),
        "diff_lambda_q2": 0.1 * jax.random.normal(ks[5], (L, DIFF_HEAD_DIM), f32),
        "diff_lambda_k2": 0.1 * jax.random.normal(ks[6], (L, DIFF_HEAD_DIM), f32),
        "diff_subln_w": gain(ks[7], (L, DIFF_V_DIM)),
        "mla_q_norm_w": gain(ks[8], (L, MLA_Q_RANK)),
        "mla_w_uq": dense(ks[9], (L, MLA_Q_RANK, MLA_HEADS * MLA_QK_DIM), MLA_Q_RANK),
        "mla_kv_norm_w": gain(ks[10], (L, MLA_KV_RANK)),
        "mla_w_ukv": dense(ks[11], (L, MLA_KV_RANK, MLA_HEADS * (MLA_NOPE_DIM + MLA_V_DIM)), MLA_KV_RANK),
        "w_o_diff": dense(ks[12], (L, DIFF_OUT_W, D_MODEL), DIFF_OUT_W),
        "w_o_mla": dense(ks[13], (L, MLA_OUT_W, D_MODEL), MLA_OUT_W),
        "w_out": dense(ks[14], (L, D_MODEL, D_MODEL), D_MODEL),
        "norm_ffn_w": gain(ks[15], (L, D_MODEL)),
        "ffn_w_up": dense(ks[16], (L, D_MODEL, 2 * D_FF), D_MODEL),
        "ffn_conv_w": dense(ks[17], (L, CONV_WIDTH, 2 * D_FF), CONV_WIDTH),
        "ffn_conv_b": 0.01 * jax.random.normal(ks[18], (L, 2 * D_FF), f32),
        "ffn_w_down": dense(ks[19], (L, D_FF, D_MODEL), D_FF),
        "final_norm_w": gain(ks[20], (D_MODEL,)),
    }


def reference(x, positions, norm_mix_w, w_in, diff_lambda_q1, diff_lambda_k1, diff_lambda_q2,
              diff_lambda_k2, diff_subln_w, mla_q_norm_w, mla_w_uq, mla_kv_norm_w, mla_w_ukv,
              w_o_diff, w_o_mla, w_out, norm_ffn_w, ffn_w_up, ffn_conv_w, ffn_conv_b,
              ffn_w_down, final_norm_w):
    B, S = x.shape[0], x.shape[1]
    cos_d, sin_d = rope_cos_sin(positions, DIFF_HEAD_DIM)
    cos_r, sin_r = rope_cos_sin(positions, MLA_ROPE_DIM)

    for l in range(DEPTH):
        h = rmsnorm(x, norm_mix_w[l])
        z = h @ w_in[l]
        q_d, k_d, v_d, c_q, c_kv, k_r, g_d, g_m = jnp.split(z, IN_SPLITS, axis=-1)

        q_d = apply_rope(q_d.reshape(B, S, 2 * DIFF_HEADS, DIFF_HEAD_DIM), cos_d, sin_d)
        k_d = apply_rope(k_d.reshape(B, S, 2 * DIFF_HEADS, DIFF_HEAD_DIM), cos_d, sin_d)
        q_d = q_d.reshape(B, S, DIFF_HEADS, 2, DIFF_HEAD_DIM)
        k_d = k_d.reshape(B, S, DIFF_HEADS, 2, DIFF_HEAD_DIM)
        v_d = v_d.reshape(B, S, DIFF_HEADS, DIFF_V_DIM)
        lambda_init = 0.8 - 0.6 * math.exp(-0.3 * l)
        lam = (jnp.exp(jnp.sum(diff_lambda_q1[l].astype(jnp.float32) * diff_lambda_k1[l].astype(jnp.float32)))
               - jnp.exp(jnp.sum(diff_lambda_q2[l].astype(jnp.float32) * diff_lambda_k2[l].astype(jnp.float32)))
               + lambda_init)
        o_d = diff_attention(q_d, k_d, v_d, lam)
        o_d = rmsnorm(o_d, diff_subln_w[l], DIFF_SUBLN_EPS) * (1.0 - lambda_init)
        o_d = o_d.reshape(B, S, DIFF_OUT_W)

        cq = rmsnorm(c_q, mla_q_norm_w[l])
        q_m = (cq @ mla_w_uq[l]).reshape(B, S, MLA_HEADS, MLA_QK_DIM)
        q_nope, q_rope = q_m[..., :MLA_NOPE_DIM], apply_rope(q_m[..., MLA_NOPE_DIM:], cos_r, sin_r)
        ckv = rmsnorm(c_kv, mla_kv_norm_w[l])
        kv = (ckv @ mla_w_ukv[l]).reshape(B, S, MLA_HEADS, MLA_NOPE_DIM + MLA_V_DIM)
        k_nope, v_m = kv[..., :MLA_NOPE_DIM], kv[..., MLA_NOPE_DIM:]
        k_rope = apply_rope(k_r[:, :, None, :], cos_r, sin_r)[:, :, 0, :]
        o_m = mla_attention(q_nope, q_rope, k_nope, k_rope, v_m).reshape(B, S, MLA_OUT_W)

        y = (jax.nn.sigmoid(g_d) * (o_d @ w_o_diff[l])
             + jax.nn.sigmoid(g_m) * (o_m @ w_o_mla[l]))
        x = x + y @ w_out[l]

        h = rmsnorm(x, norm_ffn_w[l])
        u = causal_depthwise_conv(h @ ffn_w_up[l], ffn_conv_w[l], ffn_conv_b[l])
        gate, val = u[..., :D_FF], u[..., D_FF:]
        x = x + (jax.nn.silu(gate) * val) @ ffn_w_down[l]

    return rmsnorm(x, final_norm_w)
```

```python
import functools
import math

import jax
import jax.numpy as jnp
from jax import lax
from jax.experimental import pallas as pl
from jax.experimental.pallas import tpu as pltpu

BF16 = jnp.bfloat16
F32 = jnp.float32

LANES = 128
V7X_VMEM_LIMIT_BYTES = 56 << 20

DIFF_HEADS = 16
DIFF_HEAD_DIM = 64
MLA_HEADS = 16
MLA_NOPE_DIM = 128
MLA_ROPE_DIM = 64
MLA_V_DIM = 128
ROPE_THETA = 10000.0
NORM_EPS = 1e-6
DIFF_SUBLN_EPS = 1e-5
CONV_WIDTH = 3
MASK_VALUE = -1e30


def _cparams(n_axes):
    return pltpu.CompilerParams(
        dimension_semantics=("arbitrary",) * n_axes,
        vmem_limit_bytes=V7X_VMEM_LIMIT_BYTES,
    )


def _rope_table_kernel(pos_ref, freq_ref, sign_ref, cos_ref, sin_ref):
    ang = pos_ref[...].astype(F32) * freq_ref[...]
    cos_ref[...] = jnp.cos(ang)
    sin_ref[...] = jnp.sin(ang) * sign_ref[...]


def _rope_tables(pos_col, tm=512):
    S = pos_col.shape[0]
    half = MLA_ROPE_DIM // 2
    inv_freq = ROPE_THETA ** (-jnp.arange(0, MLA_ROPE_DIM, 2, dtype=F32) / MLA_ROPE_DIM)
    freq = jnp.tile(inv_freq, LANES // half).reshape(1, LANES)
    sign = jnp.tile(jnp.concatenate([-jnp.ones((half,), F32), jnp.ones((half,), F32)]),
                    LANES // (2 * half)).reshape(1, LANES)
    return pl.pallas_call(
        _rope_table_kernel,
        grid=(S // tm,),
        in_specs=[pl.BlockSpec((tm, 1), lambda i: (i, 0)),
                  pl.BlockSpec((1, LANES), lambda i: (0, 0)),
                  pl.BlockSpec((1, LANES), lambda i: (0, 0))],
        out_specs=[pl.BlockSpec((tm, LANES), lambda i: (i, 0)),
                   pl.BlockSpec((tm, LANES), lambda i: (i, 0))],
        out_shape=[jax.ShapeDtypeStruct((S, LANES), F32)] * 2,
        compiler_params=_cparams(1),
        name="rope_tables",
    )(pos_col, freq, sign)


def _rope_partner(zc):
    lane = lax.broadcasted_iota(jnp.int32, zc.shape, 1)
    first_half = (lane & 32) == 0
    return jnp.where(first_half, pltpu.roll(zc, 96, 1), pltpu.roll(zc, 32, 1))


def _rope_lanes(z, cos, sin):
    outs = []
    for c in range(z.shape[1] // LANES):
        zc = z[:, c * LANES:(c + 1) * LANES]
        outs.append(zc * cos + _rope_partner(zc) * sin)
    return outs[0] if len(outs) == 1 else jnp.concatenate(outs, axis=1)


def _rmsnorm_kernel(x_ref, w_ref, o_ref, *, eps):
    xf = x_ref[...].astype(F32)
    ms = jnp.mean(xf * xf, axis=1, keepdims=True)
    o_ref[...] = (xf * lax.rsqrt(ms + eps) * w_ref[...]).astype(o_ref.dtype)


def _rmsnorm(x, w, out_dtype, eps=NORM_EPS, tm=256):
    M, D = x.shape
    return pl.pallas_call(
        functools.partial(_rmsnorm_kernel, eps=eps),
        grid=(M // tm,),
        in_specs=[pl.BlockSpec((tm, D), lambda i: (i, 0)),
                  pl.BlockSpec((1, D), lambda i: (0, 0))],
        out_specs=pl.BlockSpec((tm, D), lambda i: (i, 0)),
        out_shape=jax.ShapeDtypeStruct((M, D), out_dtype),
        compiler_params=_cparams(1),
        name="rmsnorm",
    )(x, w.reshape(1, D).astype(F32))


def _mm_kernel(*refs, n_pairs, has_gate, has_norm, has_addend, has_rope, epilogue, epi_arg, eps):
    refs = list(refs)
    pair_refs = []
    for _ in range(n_pairs):
        a_ref = refs.pop(0)
        w_ref = refs.pop(0)
        g_ref = refs.pop(0) if has_gate else None
        pair_refs.append((a_ref, w_ref, g_ref))
    nw_ref = refs.pop(0) if has_norm else None
    add_ref = refs.pop(0) if has_addend else None
    cos_ref = refs.pop(0) if has_rope else None
    sin_ref = refs.pop(0) if has_rope else None
    o_ref = refs.pop(0)
    wbf_refs = refs

    j = pl.program_id(0)
    i = pl.program_id(1)

    @pl.when(i == 0)
    def _():
        for (_, w_ref, _), wbf in zip(pair_refs, wbf_refs):
            wbf[...] = w_ref[...].astype(BF16)

    acc = None
    for (a_ref, _, g_ref), wbf in zip(pair_refs, wbf_refs):
        a = a_ref[...]
        if has_norm:
            af = a.astype(F32)
            ms = jnp.mean(af * af, axis=1, keepdims=True)
            a = (af * lax.rsqrt(ms + eps) * nw_ref[...]).astype(BF16)
        d = jnp.dot(a, wbf[...], preferred_element_type=F32)
        if has_gate:
            d = d * g_ref[...].astype(F32)
        acc = d if acc is None else acc + d
    if has_addend:
        acc = acc + add_ref[...]

    if epilogue == "none":
        o_ref[...] = acc.astype(o_ref.dtype)
    elif epilogue == "sigmoid":
        o_ref[...] = (1.0 / (1.0 + jnp.exp(-acc))).astype(o_ref.dtype)
    elif epilogue == "rope_lt":
        @pl.when(j < epi_arg)
        def _():
            o_ref[...] = _rope_lanes(acc, cos_ref[...], sin_ref[...]).astype(o_ref.dtype)

        @pl.when(j >= epi_arg)
        def _():
            o_ref[...] = acc.astype(o_ref.dtype)
    elif epilogue == "scale_rope_ge":
        scale, first_rope_tile = epi_arg
        acc = acc * scale

        @pl.when(j >= first_rope_tile)
        def _():
            o_ref[...] = _rope_lanes(acc, cos_ref[...], sin_ref[...]).astype(o_ref.dtype)

        @pl.when(j < first_rope_tile)
        def _():
            o_ref[...] = acc.astype(o_ref.dtype)
    elif epilogue == "krope_dup":
        lane = lax.broadcasted_iota(jnp.int32, acc.shape, 1)
        kr = jnp.where(lane < MLA_ROPE_DIM, acc, 0.0)
        roped = kr * cos_ref[...] + _rope_partner(kr) * sin_ref[...]
        o_ref[...] = (roped + pltpu.roll(roped, MLA_ROPE_DIM, 1)).astype(o_ref.dtype)
    else:
        raise ValueError(epilogue)


def _mm(pairs, *, N, tm, tn, out_dtype, name, epilogue="none", epi_arg=None,
        norm_w=None, addend=None, rope=None, eps=NORM_EPS):
    M = pairs[0]["a"].shape[0]
    has_gate = pairs[0].get("gate") is not None
    args, in_specs, scratch = [], [], []
    for p in pairs:
        K = p["K"]
        args.append(p["a"])
        in_specs.append(pl.BlockSpec((tm, K), functools.partial(lambda j, i, b: (i, b), b=p["a_blk"])))
        args.append(p["w"])
        in_specs.append(pl.BlockSpec(
            (K, tn), functools.partial(lambda j, i, r, c: (r, c + j), r=p["w_row_blk"], c=p["w_col_blk"])))
        if has_gate:
            g, g_off = p["gate"]
            args.append(g)
            in_specs.append(pl.BlockSpec((tm, tn), functools.partial(lambda j, i, c: (i, c + j), c=g_off)))
        scratch.append(pltpu.VMEM((K, tn), BF16))
    if norm_w is not None:
        args.append(norm_w.reshape(1, -1).astype(F32))
        in_specs.append(pl.BlockSpec((1, norm_w.shape[-1]), lambda j, i: (0, 0)))
    if addend is not None:
        args.append(addend)
        in_specs.append(pl.BlockSpec((tm, tn), lambda j, i: (i, j)))
    if rope is not None:
        for t in rope:
            args.append(t)
            in_specs.append(pl.BlockSpec((tm, LANES), lambda j, i: (i, 0)))
    kern = functools.partial(
        _mm_kernel, n_pairs=len(pairs), has_gate=has_gate, has_norm=norm_w is not None,
        has_addend=addend is not None, has_rope=rope is not None, epilogue=epilogue, epi_arg=epi_arg, eps=eps)
    return pl.pallas_call(
        kern,
        grid=(N // tn, M // tm),
        in_specs=in_specs,
        out_specs=pl.BlockSpec((tm, tn), lambda j, i: (i, j)),
        out_shape=jax.ShapeDtypeStruct((M, N), out_dtype),
        scratch_shapes=scratch,
        compiler_params=_cparams(2),
        name=name,
    )(*args)


CONV_HALO = 8


def _ffn_up_kernel(a_ref, wg_ref, wv_ref, cwg_ref, cwv_ref, cbg_ref, cbv_ref, o_ref,
                   wgbf, wvbf, ug_buf, uv_buf, *, tm):
    i = pl.program_id(1)

    @pl.when(i == 0)
    def _():
        wgbf[...] = wg_ref[...].astype(BF16)
        wvbf[...] = wv_ref[...].astype(BF16)
        ug_buf[0:CONV_HALO, :] = jnp.zeros((CONV_HALO, ug_buf.shape[1]), F32)
        uv_buf[0:CONV_HALO, :] = jnp.zeros((CONV_HALO, uv_buf.shape[1]), F32)

    a = a_ref[...]

    def conv(w_bf, buf, cw_ref, cb_ref):
        u = jnp.dot(a, w_bf[...], preferred_element_type=F32)
        buf[CONV_HALO:CONV_HALO + tm, :] = u
        out = cb_ref[...] + cw_ref[0:1, :] * buf[CONV_HALO - 2:CONV_HALO - 2 + tm, :]
        out = out + cw_ref[1:2, :] * buf[CONV_HALO - 1:CONV_HALO - 1 + tm, :]
        out = out + cw_ref[2:3, :] * u
        buf[0:CONV_HALO, :] = buf[tm:tm + CONV_HALO, :]
        return out

    g = conv(wgbf, ug_buf, cwg_ref, cbg_ref)
    v = conv(wvbf, uv_buf, cwv_ref, cbv_ref)
    o_ref[...] = (g / (1.0 + jnp.exp(-g)) * v).astype(o_ref.dtype)


def _ffn_up(h, w_up, conv_w, conv_b, *, tm, tn):
    M, K = h.shape
    d_ff = w_up.shape[1] // 2
    nj = d_ff // tn
    cb = conv_b.reshape(1, -1)
    return pl.pallas_call(
        functools.partial(_ffn_up_kernel, tm=tm),
        grid=(nj, M // tm),
        in_specs=[
            pl.BlockSpec((tm, K), lambda j, i: (i, 0)),
            pl.BlockSpec((K, tn), lambda j, i: (0, j)),
            pl.BlockSpec((K, tn), lambda j, i: (0, nj + j)),
            pl.BlockSpec((CONV_WIDTH, tn), lambda j, i: (0, j)),
            pl.BlockSpec((CONV_WIDTH, tn), lambda j, i: (0, nj + j)),
            pl.BlockSpec((1, tn), lambda j, i: (0, j)),
            pl.BlockSpec((1, tn), lambda j, i: (0, nj + j)),
        ],
        out_specs=pl.BlockSpec((tm, tn), lambda j, i: (i, j)),
        out_shape=jax.ShapeDtypeStruct((M, d_ff), BF16),
        scratch_shapes=[pltpu.VMEM((K, tn), BF16), pltpu.VMEM((K, tn), BF16),
                        pltpu.VMEM((CONV_HALO + tm, tn), F32), pltpu.VMEM((CONV_HALO + tm, tn), F32)],
        compiler_params=_cparams(2),
        name="ffn_up_conv_gate",
    )(h, w_up, w_up, conv_w, conv_w, cb, cb)


def _online_softmax_step(s, v, m_ref, l_ref, acc_ref):
    m_prev = m_ref[...][:, :1]
    m_new = jnp.maximum(m_prev, jnp.max(s, axis=1, keepdims=True))
    p = jnp.exp(s - m_new)
    alpha = jnp.exp(m_prev - m_new)
    l_new = alpha * l_ref[...][:, :1] + jnp.sum(p, axis=1, keepdims=True)
    acc_ref[...] = acc_ref[...] * alpha + jnp.dot(p.astype(BF16), v, preferred_element_type=F32)
    m_ref[...] = jnp.broadcast_to(m_new, m_ref.shape)
    l_ref[...] = jnp.broadcast_to(l_new, l_ref.shape)


def _diff_attn_kernel(q_ref, k_ref, v_ref, lq1_ref, lk1_ref, lq2_ref, lk2_ref, sw_ref, o_ref,
                      m_ref, l_ref, acc_ref, *, T, lam_init, scale):
    qi = pl.program_id(1)
    q = q_ref[...] * jnp.asarray(scale, BF16)
    lane = lax.broadcasted_iota(jnp.int32, q.shape, 1)
    zero = jnp.zeros_like(q)
    qs = jnp.concatenate([jnp.where(lane < DIFF_HEAD_DIM, q, zero),
                          jnp.where(lane >= DIFF_HEAD_DIM, q, zero)], axis=0)

    m_ref[...] = jnp.full(m_ref.shape, MASK_VALUE, F32)
    l_ref[...] = jnp.zeros(l_ref.shape, F32)
    acc_ref[...] = jnp.zeros(acc_ref.shape, F32)

    def block(start, masked):
        k = k_ref[pl.ds(start, T), :]
        v = v_ref[pl.ds(start, T), :]
        s = lax.dot_general(qs, k, (((1,), (1,)), ((), ())), preferred_element_type=F32)
        if masked:
            row = lax.broadcasted_iota(jnp.int32, s.shape, 0)
            col = lax.broadcasted_iota(jnp.int32, s.shape, 1)
            qpos = jnp.where(row >= T, row - T, row)
            s = jnp.where(qpos >= col, s, MASK_VALUE)
        _online_softmax_step(s, v, m_ref, l_ref, acc_ref)

    def body(jb, carry):
        block(pl.multiple_of(jb * T, T), False)
        return carry

    lax.fori_loop(0, qi, body, 0)
    block(pl.multiple_of(qi * T, T), True)

    o = acc_ref[...] / l_ref[...]
    lam = (jnp.exp(jnp.sum(lq1_ref[...] * lk1_ref[...], axis=1, keepdims=True))
           - jnp.exp(jnp.sum(lq2_ref[...] * lk2_ref[...], axis=1, keepdims=True)) + lam_init)
    od = o[:T] - lam * o[T:]
    ms = jnp.mean(od * od, axis=1, keepdims=True)
    o_ref[...] = (od * lax.rsqrt(ms + DIFF_SUBLN_EPS) * sw_ref[...] * (1.0 - lam_init)).astype(o_ref.dtype)


def _diff_attn(zqkv, lq1, lk1, lq2, lk2, subln_w, *, lam_init, T):
    S = zqkv.shape[0]
    H = DIFF_HEADS
    hd = 2 * DIFF_HEAD_DIM
    vec = lambda a: a.reshape(1, -1).astype(F32)
    small = lambda n: pl.BlockSpec((1, n), lambda h, qi: (0, 0))
    return pl.pallas_call(
        functools.partial(_diff_attn_kernel, T=T, lam_init=lam_init, scale=DIFF_HEAD_DIM ** -0.5),
        grid=(H, S // T),
        in_specs=[
            pl.BlockSpec((T, hd), lambda h, qi: (qi, h)),
            pl.BlockSpec((S, hd), lambda h, qi: (0, H + h)),
            pl.BlockSpec((S, hd), lambda h, qi: (0, 2 * H + h)),
            small(DIFF_HEAD_DIM), small(DIFF_HEAD_DIM), small(DIFF_HEAD_DIM), small(DIFF_HEAD_DIM),
            small(hd),
        ],
        out_specs=pl.BlockSpec((T, hd), lambda h, qi: (qi, h)),
        out_shape=jax.ShapeDtypeStruct((S, H * hd), BF16),
        scratch_shapes=[pltpu.VMEM((2 * T, LANES), F32), pltpu.VMEM((2 * T, LANES), F32),
                        pltpu.VMEM((2 * T, hd), F32)],
        compiler_params=_cparams(2),
        name="diff_attention",
    )(zqkv, zqkv, zqkv, vec(lq1), vec(lk1), vec(lq2), vec(lk2), vec(subln_w))


def _mla_attn_kernel(qn_ref, qr_ref, kn_ref, kr_ref, v_ref, o_ref, kcat, m_ref, l_ref, acc_ref, *, T):
    h = pl.program_id(0)
    qi = pl.program_id(1)

    @pl.when(qi == 0)
    def _():
        kcat[:, 0:MLA_NOPE_DIM] = kn_ref[...]
        kcat[:, MLA_NOPE_DIM:] = kr_ref[...]

    qr = qr_ref[...]
    lane = lax.broadcasted_iota(jnp.int32, qr.shape, 1)
    lo = (h % 2) * MLA_ROPE_DIM
    mine = jnp.logical_and(lane >= lo, lane < lo + MLA_ROPE_DIM)
    qcat = jnp.concatenate([qn_ref[...], jnp.where(mine, qr, jnp.zeros_like(qr))], axis=1)

    m_ref[...] = jnp.full(m_ref.shape, MASK_VALUE, F32)
    l_ref[...] = jnp.zeros(l_ref.shape, F32)
    acc_ref[...] = jnp.zeros(acc_ref.shape, F32)

    def block(start, masked):
        k = kcat[pl.ds(start, T), :]
        v = v_ref[pl.ds(start, T), :]
        s = lax.dot_general(qcat, k, (((1,), (1,)), ((), ())), preferred_element_type=F32)
        if masked:
            row = lax.broadcasted_iota(jnp.int32, s.shape, 0)
            col = lax.broadcasted_iota(jnp.int32, s.shape, 1)
            s = jnp.where(row >= col, s, MASK_VALUE)
        _online_softmax_step(s, v, m_ref, l_ref, acc_ref)

    def body(jb, carry):
        block(pl.multiple_of(jb * T, T), False)
        return carry

    lax.fori_loop(0, qi, body, 0)
    block(pl.multiple_of(qi * T, T), True)
    o_ref[...] = (acc_ref[...] / l_ref[...]).astype(o_ref.dtype)


def _mla_attn(qm, kv, kr_dup, *, T):
    S = qm.shape[0]
    H = MLA_HEADS
    return pl.pallas_call(
        functools.partial(_mla_attn_kernel, T=T),
        grid=(H, S // T),
        in_specs=[
            pl.BlockSpec((T, MLA_NOPE_DIM), lambda h, qi: (qi, h)),
            pl.BlockSpec((T, LANES), lambda h, qi: (qi, H + h // 2)),
            pl.BlockSpec((S, MLA_NOPE_DIM), lambda h, qi: (0, 2 * h)),
            pl.BlockSpec((S, LANES), lambda h, qi: (0, 0)),
            pl.BlockSpec((S, MLA_V_DIM), lambda h, qi: (0, 2 * h + 1)),
        ],
        out_specs=pl.BlockSpec((T, MLA_V_DIM), lambda h, qi: (qi, h)),
        out_shape=jax.ShapeDtypeStruct((S, H * MLA_V_DIM), BF16),
        scratch_shapes=[pltpu.VMEM((S, MLA_NOPE_DIM + LANES), BF16),
                        pltpu.VMEM((T, LANES), F32), pltpu.VMEM((T, LANES), F32),
                        pltpu.VMEM((T, MLA_V_DIM), F32)],
        compiler_params=_cparams(2),
        name="mla_attention",
    )(qm, qm, kv, kr_dup, kv)


def _block_forward(x2d, pos_col, l, norm_mix_w, w_in, lq1, lk1, lq2, lk2, subln_w, q_norm_w, w_uq,
                   kv_norm_w, w_ukv, w_o_diff, w_o_mla, w_out, norm_ffn_w, w_up, conv_w, conv_b, w_down,
                   *, tm=1024, tn=512, t_attn=512, tn_ffn=256, tm_down=512):
    S, D = x2d.shape
    H = DIFF_HEADS
    qkv_w = 3 * H * 2 * DIFF_HEAD_DIM
    q_rank = w_uq.shape[0]
    kv_rank = w_ukv.shape[0]
    lat_w = q_rank + kv_rank
    main_w = qkv_w + lat_w
    gate_start = main_w + MLA_ROPE_DIM
    lam_init = 0.8 - 0.6 * math.exp(-0.3 * l)

    cos, sin = _rope_tables(pos_col)
    h = _rmsnorm(x2d, norm_mix_w, BF16)

    z = _mm([dict(a=h, a_blk=0, K=D, w=w_in, w_row_blk=0, w_col_blk=0)], N=main_w, tm=tm, tn=tn,
            out_dtype=BF16, name="in_proj_main", epilogue="rope_lt",
            epi_arg=(2 * H * 2 * DIFF_HEAD_DIM) // tn, rope=(cos, sin))
    kr_dup = _mm([dict(a=h, a_blk=0, K=D, w=w_in, w_row_blk=0, w_col_blk=main_w // LANES)], N=LANES,
                 tm=tm, tn=LANES, out_dtype=BF16, name="in_proj_krope", epilogue="krope_dup",
                 rope=(cos, sin))
    gates = _mm([dict(a=h, a_blk=0, K=D, w=w_in[:, gate_start:], w_row_blk=0, w_col_blk=0)], N=2 * D,
                tm=tm, tn=tn, out_dtype=BF16, name="in_proj_gates", epilogue="sigmoid")

    o_d = _diff_attn(z, lq1, lk1, lq2, lk2, subln_w, lam_init=lam_init, T=t_attn)

    qk_dim = MLA_NOPE_DIM + MLA_ROPE_DIM
    w_uq3 = w_uq.reshape(q_rank, MLA_HEADS, qk_dim)
    w_uq_perm = jnp.concatenate([w_uq3[:, :, :MLA_NOPE_DIM].reshape(q_rank, -1),
                                 w_uq3[:, :, MLA_NOPE_DIM:].reshape(q_rank, -1)], axis=1)
    qm = _mm([dict(a=z, a_blk=qkv_w // q_rank, K=q_rank, w=w_uq_perm, w_row_blk=0, w_col_blk=0)],
             N=MLA_HEADS * qk_dim, tm=tm, tn=tn, out_dtype=BF16, name="mla_q_up", norm_w=q_norm_w,
             epilogue="scale_rope_ge", epi_arg=(qk_dim ** -0.5, (MLA_HEADS * MLA_NOPE_DIM) // tn),
             rope=(cos, sin))
    kv = _mm([dict(a=z, a_blk=(qkv_w + q_rank) // kv_rank, K=kv_rank, w=w_ukv, w_row_blk=0, w_col_blk=0)],
             N=w_ukv.shape[1], tm=tm, tn=tn, out_dtype=BF16, name="mla_kv_up", norm_w=kv_norm_w)
    o_m = _mla_attn(qm, kv, kr_dup, T=t_attn)

    y = _mm([dict(a=o_d, a_blk=0, K=o_d.shape[1], w=w_o_diff, w_row_blk=0, w_col_blk=0, gate=(gates, 0)),
             dict(a=o_m, a_blk=0, K=o_m.shape[1], w=w_o_mla, w_row_blk=0, w_col_blk=0, gate=(gates, D // tn))],
            N=D, tm=tm, tn=tn, out_dtype=BF16, name="branch_merge")
    x1 = _mm([dict(a=y, a_blk=0, K=D, w=w_out, w_row_blk=0, w_col_blk=0)], N=D, tm=tm, tn=tn,
             out_dtype=F32, name="out_proj", addend=x2d)

    h2 = _rmsnorm(x1, norm_ffn_w, BF16)
    act = _ffn_up(h2, w_up, conv_w, conv_b, tm=tm, tn=tn_ffn)
    d_ff = act.shape[1]
    k_half = d_ff // 2
    p0 = _mm([dict(a=act, a_blk=0, K=k_half, w=w_down, w_row_blk=0, w_col_blk=0)], N=D, tm=tm_down, tn=tn,
             out_dtype=F32, name="ffn_down_lo", addend=x1)
    x2 = _mm([dict(a=act, a_blk=1, K=k_half, w=w_down, w_row_blk=1, w_col_blk=0)], N=D, tm=tm_down, tn=tn,
             out_dtype=F32, name="ffn_down_hi", addend=p0)
    return x2


def kernel(x, positions, norm_mix_w, w_in, diff_lambda_q1, diff_lambda_k1, diff_lambda_q2, diff_lambda_k2, diff_subln_w, mla_q_norm_w, mla_w_uq, mla_kv_norm_w, mla_w_ukv, w_o_diff, w_o_mla, w_out, norm_ffn_w, ffn_w_up, ffn_conv_w, ffn_conv_b, ffn_w_down, final_norm_w):
    B, S, D = x.shape
    assert B == 1
    x2d = x.reshape(S, D)
    pos_col = positions.reshape(S, 1)
    for l in range(w_in.shape[0]):
        x2d = _block_forward(
            x2d, pos_col, l, norm_mix_w[l], w_in[l], diff_lambda_q1[l], diff_lambda_k1[l], diff_lambda_q2[l],
            diff_lambda_k2[l], diff_subln_w[l], mla_q_norm_w[l], mla_w_uq[l], mla_kv_norm_w[l], mla_w_ukv[l],
            w_o_diff[l], w_o_mla[l], w_out[l], norm_ffn_w[l], ffn_w_up[l], ffn_conv_w[l], ffn_conv_b[l],
            ffn_w_down[l])
    out = _rmsnorm(x2d, final_norm_w, F32)
    return out.reshape(B, S, D)
```

```python
import functools
import math

import jax
import jax.numpy as jnp
from jax import lax
from jax.experimental import pallas as pl
from jax.experimental.pallas import tpu as pltpu

BF16 = jnp.bfloat16
F32 = jnp.float32

LANES = 128
V7X_VMEM_LIMIT_BYTES = 56 << 20

DIFF_HEADS = 16
DIFF_HEAD_DIM = 64
MLA_HEADS = 16
MLA_NOPE_DIM = 128
MLA_ROPE_DIM = 64
MLA_V_DIM = 128
ROPE_THETA = 10000.0
NORM_EPS = 1e-6
DIFF_SUBLN_EPS = 1e-5
CONV_WIDTH = 3
MASK_VALUE = -1e30


def _cparams(n_axes, flags=None):
    return pltpu.CompilerParams(
        dimension_semantics=("arbitrary",) * n_axes,
        vmem_limit_bytes=V7X_VMEM_LIMIT_BYTES,
        flags=flags,
    )


def _rope_table_kernel(pos_ref, freq_ref, sign_ref, cos_ref, sin_ref):
    ang = pos_ref[...].astype(F32) * freq_ref[...]
    cos_ref[...] = jnp.cos(ang)
    sin_ref[...] = jnp.sin(ang) * sign_ref[...]


def _rope_tables(pos_col, tm=512):
    S = pos_col.shape[0]
    half = MLA_ROPE_DIM // 2
    inv_freq = ROPE_THETA ** (-jnp.arange(0, MLA_ROPE_DIM, 2, dtype=F32) / MLA_ROPE_DIM)
    freq = jnp.tile(inv_freq, LANES // half).reshape(1, LANES)
    sign = jnp.tile(jnp.concatenate([-jnp.ones((half,), F32), jnp.ones((half,), F32)]),
                    LANES // (2 * half)).reshape(1, LANES)
    return pl.pallas_call(
        _rope_table_kernel,
        grid=(S // tm,),
        in_specs=[pl.BlockSpec((tm, 1), lambda i: (i, 0)),
                  pl.BlockSpec((1, LANES), lambda i: (0, 0)),
                  pl.BlockSpec((1, LANES), lambda i: (0, 0))],
        out_specs=[pl.BlockSpec((tm, LANES), lambda i: (i, 0)),
                   pl.BlockSpec((tm, LANES), lambda i: (i, 0))],
        out_shape=[jax.ShapeDtypeStruct((S, LANES), F32)] * 2,
        compiler_params=_cparams(1),
        name="rope_tables",
    )(pos_col, freq, sign)


def _rope_partner(zc):
    lane = lax.broadcasted_iota(jnp.int32, zc.shape, 1)
    first_half = (lane & 32) == 0
    return jnp.where(first_half, pltpu.roll(zc, 96, 1), pltpu.roll(zc, 32, 1))


def _rope_lanes(z, cos, sin):
    outs = []
    for c in range(z.shape[1] // LANES):
        zc = z[:, c * LANES:(c + 1) * LANES]
        outs.append(zc * cos + _rope_partner(zc) * sin)
    return outs[0] if len(outs) == 1 else jnp.concatenate(outs, axis=1)


def _rmsnorm_kernel(x_ref, w_ref, o_ref, *, eps):
    xf = x_ref[...].astype(F32)
    ms = jnp.mean(xf * xf, axis=1, keepdims=True)
    o_ref[...] = (xf * lax.rsqrt(ms + eps) * w_ref[...]).astype(o_ref.dtype)


def _rmsnorm(x, w, out_dtype, eps=NORM_EPS, tm=256):
    M, D = x.shape
    return pl.pallas_call(
        functools.partial(_rmsnorm_kernel, eps=eps),
        grid=(M // tm,),
        in_specs=[pl.BlockSpec((tm, D), lambda i: (i, 0)),
                  pl.BlockSpec((1, D), lambda i: (0, 0))],
        out_specs=pl.BlockSpec((tm, D), lambda i: (i, 0)),
        out_shape=jax.ShapeDtypeStruct((M, D), out_dtype),
        compiler_params=_cparams(1),
        name="rmsnorm",
    )(x, w.reshape(1, D).astype(F32))


def _mm_kernel(*refs, n_pairs, has_gate, has_norm, has_addend, has_rope, epilogue, epi_arg, eps):
    refs = list(refs)
    pair_refs = []
    for _ in range(n_pairs):
        a_ref = refs.pop(0)
        w_ref = refs.pop(0)
        g_ref = refs.pop(0) if has_gate else None
        pair_refs.append((a_ref, w_ref, g_ref))
    nw_ref = refs.pop(0) if has_norm else None
    add_ref = refs.pop(0) if has_addend else None
    cos_ref = refs.pop(0) if has_rope else None
    sin_ref = refs.pop(0) if has_rope else None
    o_ref = refs.pop(0)
    wbf_refs = refs

    j = pl.program_id(0)
    i = pl.program_id(1)

    @pl.when(i == 0)
    def _():
        for (_, w_ref, _), wbf in zip(pair_refs, wbf_refs):
            wbf[...] = w_ref[...].astype(BF16)

    acc = None
    for (a_ref, _, g_ref), wbf in zip(pair_refs, wbf_refs):
        a = a_ref[...]
        if has_norm:
            af = a.astype(F32)
            ms = jnp.mean(af * af, axis=1, keepdims=True)
            a = (af * lax.rsqrt(ms + eps) * nw_ref[...]).astype(BF16)
        d = jnp.dot(a, wbf[...], preferred_element_type=F32)
        if has_gate:
            d = d * g_ref[...].astype(F32)
        acc = d if acc is None else acc + d
    if has_addend:
        acc = acc + add_ref[...]

    if epilogue == "none":
        o_ref[...] = acc.astype(o_ref.dtype)
    elif epilogue == "sigmoid":
        o_ref[...] = (1.0 / (1.0 + jnp.exp(-acc))).astype(o_ref.dtype)
    elif epilogue == "rope_lt":
        @pl.when(j < epi_arg)
        def _():
            o_ref[...] = _rope_lanes(acc, cos_ref[...], sin_ref[...]).astype(o_ref.dtype)

        @pl.when(j >= epi_arg)
        def _():
            o_ref[...] = acc.astype(o_ref.dtype)
    elif epilogue == "scale_rope_ge":
        scale, first_rope_tile = epi_arg
        acc = acc * scale

        @pl.when(j >= first_rope_tile)
        def _():
            o_ref[...] = _rope_lanes(acc, cos_ref[...], sin_ref[...]).astype(o_ref.dtype)

        @pl.when(j < first_rope_tile)
        def _():
            o_ref[...] = acc.astype(o_ref.dtype)
    elif epilogue == "krope_dup":
        lane = lax.broadcasted_iota(jnp.int32, acc.shape, 1)
        kr = jnp.where(lane < MLA_ROPE_DIM, acc, 0.0)
        roped = kr * cos_ref[...] + _rope_partner(kr) * sin_ref[...]
        o_ref[...] = (roped + pltpu.roll(roped, MLA_ROPE_DIM, 1)).astype(o_ref.dtype)
    else:
        raise ValueError(epilogue)


def _mm(pairs, *, N, tm, tn, out_dtype, name, epilogue="none", epi_arg=None,
        norm_w=None, addend=None, rope=None, eps=NORM_EPS):
    M = pairs[0]["a"].shape[0]
    has_gate = pairs[0].get("gate") is not None
    args, in_specs, scratch = [], [], []
    for p in pairs:
        K = p["K"]
        args.append(p["a"])
        in_specs.append(pl.BlockSpec((tm, K), functools.partial(lambda j, i, b: (i, b), b=p["a_blk"])))
        args.append(p["w"])
        in_specs.append(pl.BlockSpec(
            (K, tn), functools.partial(lambda j, i, r, c: (r, c + j), r=p["w_row_blk"], c=p["w_col_blk"])))
        if has_gate:
            g, g_off = p["gate"]
            args.append(g)
            in_specs.append(pl.BlockSpec((tm, tn), functools.partial(lambda j, i, c: (i, c + j), c=g_off)))
        scratch.append(pltpu.VMEM((K, tn), BF16))
    if norm_w is not None:
        args.append(norm_w.reshape(1, -1).astype(F32))
        in_specs.append(pl.BlockSpec((1, norm_w.shape[-1]), lambda j, i: (0, 0)))
    if addend is not None:
        args.append(addend)
        in_specs.append(pl.BlockSpec((tm, tn), lambda j, i: (i, j)))
    if rope is not None:
        for t in rope:
            args.append(t)
            in_specs.append(pl.BlockSpec((tm, LANES), lambda j, i: (i, 0)))
    kern = functools.partial(
        _mm_kernel, n_pairs=len(pairs), has_gate=has_gate, has_norm=norm_w is not None,
        has_addend=addend is not None, has_rope=rope is not None, epilogue=epilogue, epi_arg=epi_arg, eps=eps)
    return pl.pallas_call(
        kern,
        grid=(N // tn, M // tm),
        in_specs=in_specs,
        out_specs=pl.BlockSpec((tm, tn), lambda j, i: (i, j)),
        out_shape=jax.ShapeDtypeStruct((M, N), out_dtype),
        scratch_shapes=scratch,
        compiler_params=_cparams(2),
        name=name,
    )(*args)


CONV_HALO = 8


def _ffn_up_kernel(a_ref, wg_ref, wv_ref, cwg_ref, cwv_ref, cbg_ref, cbv_ref, o_ref,
                   wgbf, wvbf, ug_buf, uv_buf, *, tm):
    i = pl.program_id(1)

    @pl.when(i == 0)
    def _():
        wgbf[...] = wg_ref[...].astype(BF16)
        wvbf[...] = wv_ref[...].astype(BF16)
        ug_buf[0:CONV_HALO, :] = jnp.zeros((CONV_HALO, ug_buf.shape[1]), F32)
        uv_buf[0:CONV_HALO, :] = jnp.zeros((CONV_HALO, uv_buf.shape[1]), F32)

    a = a_ref[...]

    def conv(w_bf, buf, cw_ref, cb_ref):
        u = jnp.dot(a, w_bf[...], preferred_element_type=F32)
        buf[CONV_HALO:CONV_HALO + tm, :] = u
        out = cb_ref[...] + cw_ref[0:1, :] * buf[CONV_HALO - 2:CONV_HALO - 2 + tm, :]
        out = out + cw_ref[1:2, :] * buf[CONV_HALO - 1:CONV_HALO - 1 + tm, :]
        out = out + cw_ref[2:3, :] * u
        buf[0:CONV_HALO, :] = buf[tm:tm + CONV_HALO, :]
        return out

    g = conv(wgbf, ug_buf, cwg_ref, cbg_ref)
    v = conv(wvbf, uv_buf, cwv_ref, cbv_ref)
    o_ref[...] = (g / (1.0 + jnp.exp(-g)) * v).astype(o_ref.dtype)


def _ffn_up(h, w_up, conv_w, conv_b, *, tm, tn):
    M, K = h.shape
    d_ff = w_up.shape[1] // 2
    nj = d_ff // tn
    cb = conv_b.reshape(1, -1)
    return pl.pallas_call(
        functools.partial(_ffn_up_kernel, tm=tm),
        grid=(nj, M // tm),
        in_specs=[
            pl.BlockSpec((tm, K), lambda j, i: (i, 0)),
            pl.BlockSpec((K, tn), lambda j, i: (0, j)),
            pl.BlockSpec((K, tn), lambda j, i: (0, nj + j)),
            pl.BlockSpec((CONV_WIDTH, tn), lambda j, i: (0, j)),
            pl.BlockSpec((CONV_WIDTH, tn), lambda j, i: (0, nj + j)),
            pl.BlockSpec((1, tn), lambda j, i: (0, j)),
            pl.BlockSpec((1, tn), lambda j, i: (0, nj + j)),
        ],
        out_specs=pl.BlockSpec((tm, tn), lambda j, i: (i, j)),
        out_shape=jax.ShapeDtypeStruct((M, d_ff), BF16),
        scratch_shapes=[pltpu.VMEM((K, tn), BF16), pltpu.VMEM((K, tn), BF16),
                        pltpu.VMEM((CONV_HALO + tm, tn), F32), pltpu.VMEM((CONV_HALO + tm, tn), F32)],
        compiler_params=_cparams(2),
        name="ffn_up_conv_gate",
    )(h, w_up, w_up, conv_w, conv_w, cb, cb)


ATTN_CHUNK = 256


def _nt_dot(a, b):
    return lax.dot_general(a, b, (((1,), (1,)), ((), ())), preferred_element_type=F32)


def _eye_bf16(n):
    r = lax.broadcasted_iota(jnp.int32, (n, n), 0)
    c = lax.broadcasted_iota(jnp.int32, (n, n), 1)
    return jnp.where(r == c, 1.0, 0.0).astype(BF16)


def _transpose_bf16(x, eye):
    return _nt_dot(eye, x).astype(BF16)


SUBLANES = 8
REDUCE_WAYS = 8


def _reduce_rows(x, op, final):
    n = x.shape[0]
    groups = [x[r * SUBLANES:(r + 1) * SUBLANES] for r in range(n // SUBLANES)]
    ways = min(REDUCE_WAYS, len(groups))
    parts = groups[:ways]
    for g, blk in enumerate(groups[ways:]):
        parts[g % ways] = op(parts[g % ways], blk)
    while len(parts) > 1:
        parts = [op(parts[i], parts[i + 1]) if i + 1 < len(parts) else parts[i] for i in range(0, len(parts), 2)]
    return final(parts[0], axis=0, keepdims=True)


def _softmax_chunk(load_s, m_ref, l_ref, cols, mask_q0):
    def scores():
        sT = load_s()
        if mask_q0 is not None:
            key = lax.broadcasted_iota(jnp.int32, sT.shape, 0)
            qq = lax.broadcasted_iota(jnp.int32, sT.shape, 1) + mask_q0
            sT = jnp.where(qq >= key, sT, MASK_VALUE)
        return sT

    m_prev = m_ref[:, cols]
    m_new = jnp.maximum(m_prev, _reduce_rows(scores(), jnp.maximum, jnp.max))
    p = jnp.exp(scores() - m_new)
    alpha = jnp.exp(m_prev - m_new)
    l_ref[:, cols] = alpha * l_ref[:, cols] + _reduce_rows(p, jnp.add, jnp.sum)
    m_ref[:, cols] = m_new
    return p.astype(BF16), alpha


def _causal_attn_loop(qi, k_ref, vT_ref, qT_ref, s_ref, p_ref, a_ref, m_ref, l_ref, acc_ref, *, T, n_maps):
    CW = min(ATTN_CHUNK, T)
    per_map = T // CW
    chunks = [(slice(c * CW, (c + 1) * CW), (c % per_map) * CW) for c in range(n_maps * per_map)]
    m_ref[...] = jnp.full(m_ref.shape, MASK_VALUE, F32)
    l_ref[...] = jnp.zeros(l_ref.shape, F32)
    acc_ref[...] = jnp.zeros(acc_ref.shape, F32)

    def qk(blk, slot, cols):
        k = k_ref[pl.ds(pl.multiple_of(blk * T, T), T), :]
        s_ref[slot, :, cols] = jnp.dot(k, qT_ref[:, cols], preferred_element_type=F32)

    def softmax(slot, cols, q0, diagonal):
        n = q0 + CW if diagonal else T
        p, alpha = _softmax_chunk(lambda: s_ref[slot, 0:n, cols], m_ref, l_ref, cols, q0 if diagonal else None)
        p_ref[slot, 0:n, cols] = p
        a_ref[slot, :, cols] = alpha

    def pv(blk, slot, cols, q0, diagonal):
        n = q0 + CW if diagonal else T
        acc_ref[:, cols] = acc_ref[:, cols] * a_ref[slot, :, cols] + jnp.dot(
            vT_ref[blk, :, 0:n], p_ref[slot, 0:n, cols], preferred_element_type=F32)

    def step(t, slot):
        for cols, q0 in chunks:
            pv(t - 2, slot, cols, q0, False)
            qk(t, slot, cols)
            softmax(1 - slot, cols, q0, False)

    def drain(slot):
        for cols, q0 in chunks:
            softmax(slot, cols, q0, True)
        for cols, q0 in chunks:
            pv(qi, slot, cols, q0, True)

    for cols, _ in chunks:
        qk(0, 0, cols)

    @pl.when(qi == 0)
    def _():
        drain(0)

    @pl.when(qi >= 1)
    def _():
        for cols, q0 in chunks:
            qk(1, 1, cols)
            softmax(0, cols, q0, False)

    def body(u, carry):
        t = 2 + 2 * u
        step(t, 0)
        step(t + 1, 1)
        return carry

    lax.fori_loop(0, lax.shift_right_arithmetic(qi - 1, 1), body, 0)

    @pl.when(jnp.logical_and(qi >= 2, qi % 2 == 0))
    def _():
        step(qi, 0)
        for cols, q0 in chunks:
            pv(qi - 1, 1, cols, q0, False)
        drain(0)

    @pl.when(qi % 2 == 1)
    def _():
        for cols, q0 in chunks:
            pv(qi - 1, 0, cols, q0, False)
        drain(1)


def _diff_attn_kernel(q_ref, k_ref, v_ref, lq1_ref, lk1_ref, lq2_ref, lk2_ref, sw_ref, o_ref,
                      vT_ref, qT_ref, s_ref, p_ref, a_ref, m_ref, l_ref, acc_ref, *, T, lam_init, scale):
    qi = pl.program_id(1)
    eye = _eye_bf16(LANES)

    @pl.when(qi == 0)
    def _():
        for jb in range(vT_ref.shape[0]):
            vT_ref[jb] = _transpose_bf16(v_ref[jb * T:(jb + 1) * T, :], eye)

    q = q_ref[...] * jnp.asarray(scale, BF16)
    lane = lax.broadcasted_iota(jnp.int32, q.shape, 1)
    zero = jnp.zeros_like(q)
    qT_ref[:, 0:T] = _transpose_bf16(jnp.where(lane < DIFF_HEAD_DIM, q, zero), eye)
    qT_ref[:, T:2 * T] = _transpose_bf16(jnp.where(lane >= DIFF_HEAD_DIM, q, zero), eye)

    _causal_attn_loop(qi, k_ref, vT_ref, qT_ref, s_ref, p_ref, a_ref, m_ref, l_ref, acc_ref, T=T, n_maps=2)

    oT = acc_ref[...] / l_ref[...]
    lam = (jnp.exp(jnp.sum(lq1_ref[...] * lk1_ref[...], axis=1, keepdims=True))
           - jnp.exp(jnp.sum(lq2_ref[...] * lk2_ref[...], axis=1, keepdims=True)) + lam_init)
    odT = oT[:, 0:T] - lam * oT[:, T:2 * T]
    ms = jnp.mean(odT * odT, axis=0, keepdims=True)
    outT = (odT * lax.rsqrt(ms + DIFF_SUBLN_EPS) * sw_ref[...] * (1.0 - lam_init)).astype(BF16)
    for r in range(T // LANES):
        o_ref[r * LANES:(r + 1) * LANES, :] = _transpose_bf16(
            outT[:, r * LANES:(r + 1) * LANES], eye).astype(o_ref.dtype)


def _diff_attn(zqkv, lq1, lk1, lq2, lk2, subln_w, *, lam_init, T):
    S = zqkv.shape[0]
    H = DIFF_HEADS
    hd = 2 * DIFF_HEAD_DIM
    vec = lambda a: a.reshape(1, -1).astype(F32)
    small = lambda n: pl.BlockSpec((1, n), lambda h, qi: (0, 0))
    return pl.pallas_call(
        functools.partial(_diff_attn_kernel, T=T, lam_init=lam_init, scale=DIFF_HEAD_DIM ** -0.5),
        grid=(H, S // T),
        in_specs=[
            pl.BlockSpec((T, hd), lambda h, qi: (qi, h)),
            pl.BlockSpec((S, hd), lambda h, qi: (0, H + h)),
            pl.BlockSpec((S, hd), lambda h, qi: (0, 2 * H + h)),
            small(DIFF_HEAD_DIM), small(DIFF_HEAD_DIM), small(DIFF_HEAD_DIM), small(DIFF_HEAD_DIM),
            pl.BlockSpec((hd, 1), lambda h, qi: (0, 0)),
        ],
        out_specs=pl.BlockSpec((T, hd), lambda h, qi: (qi, h)),
        out_shape=jax.ShapeDtypeStruct((S, H * hd), BF16),
        scratch_shapes=[pltpu.VMEM((S // T, hd, T), BF16), pltpu.VMEM((hd, 2 * T), BF16),
                        pltpu.VMEM((2, T, 2 * T), F32), pltpu.VMEM((2, T, 2 * T), BF16),
                        pltpu.VMEM((2, 1, 2 * T), F32),
                        pltpu.VMEM((1, 2 * T), F32), pltpu.VMEM((1, 2 * T), F32),
                        pltpu.VMEM((hd, 2 * T), F32)],
        compiler_params=_cparams(2),
        name="diff_attention",
    )(zqkv, zqkv, zqkv, vec(lq1), vec(lk1), vec(lq2), vec(lk2), subln_w.reshape(-1, 1).astype(F32))


def _mla_attn_kernel(qn_ref, qr_ref, kn_ref, kr_ref, v_ref, o_ref, kcat, vT_ref, qT_ref, s_ref, p_ref, a_ref,
                     m_ref, l_ref, acc_ref, *, T):
    h = pl.program_id(0)
    qi = pl.program_id(1)
    eye = _eye_bf16(LANES)

    @pl.when(qi == 0)
    def _():
        kcat[:, 0:MLA_NOPE_DIM] = kn_ref[...]
        kcat[:, MLA_NOPE_DIM:] = kr_ref[...]
        for jb in range(vT_ref.shape[0]):
            vT_ref[jb] = _transpose_bf16(v_ref[jb * T:(jb + 1) * T, :], eye)

    qr = qr_ref[...]
    lane = lax.broadcasted_iota(jnp.int32, qr.shape, 1)
    lo = (h % 2) * MLA_ROPE_DIM
    mine = jnp.logical_and(lane >= lo, lane < lo + MLA_ROPE_DIM)
    qT_ref[0:MLA_NOPE_DIM, :] = _transpose_bf16(qn_ref[...], eye)
    qT_ref[MLA_NOPE_DIM:, :] = _transpose_bf16(jnp.where(mine, qr, jnp.zeros_like(qr)), eye)

    _causal_attn_loop(qi, kcat, vT_ref, qT_ref, s_ref, p_ref, a_ref, m_ref, l_ref, acc_ref, T=T, n_maps=1)

    oT = (acc_ref[...] / l_ref[...]).astype(BF16)
    for r in range(T // LANES):
        o_ref[r * LANES:(r + 1) * LANES, :] = _transpose_bf16(
            oT[:, r * LANES:(r + 1) * LANES], eye).astype(o_ref.dtype)


def _mla_attn(qm, kv, kr_dup, *, T):
    S = qm.shape[0]
    H = MLA_HEADS
    return pl.pallas_call(
        functools.partial(_mla_attn_kernel, T=T),
        grid=(H, S // T),
        in_specs=[
            pl.BlockSpec((T, MLA_NOPE_DIM), lambda h, qi: (qi, h)),
            pl.BlockSpec((T, LANES), lambda h, qi: (qi, H + h // 2)),
            pl.BlockSpec((S, MLA_NOPE_DIM), lambda h, qi: (0, 2 * h)),
            pl.BlockSpec((S, LANES), lambda h, qi: (0, 0)),
            pl.BlockSpec((S, MLA_V_DIM), lambda h, qi: (0, 2 * h + 1)),
        ],
        out_specs=pl.BlockSpec((T, MLA_V_DIM), lambda h, qi: (qi, h)),
        out_shape=jax.ShapeDtypeStruct((S, H * MLA_V_DIM), BF16),
        scratch_shapes=[pltpu.VMEM((S, MLA_NOPE_DIM + LANES), BF16),
                        pltpu.VMEM((S // T, MLA_V_DIM, T), BF16), pltpu.VMEM((MLA_NOPE_DIM + LANES, T), BF16),
                        pltpu.VMEM((2, T, T), F32), pltpu.VMEM((2, T, T), BF16), pltpu.VMEM((2, 1, T), F32),
                        pltpu.VMEM((1, T), F32), pltpu.VMEM((1, T), F32),
                        pltpu.VMEM((MLA_V_DIM, T), F32)],
        compiler_params=_cparams(2),
        name="mla_attention",
    )(qm, qm, kv, kr_dup, kv)


def _block_forward(x2d, pos_col, l, norm_mix_w, w_in, lq1, lk1, lq2, lk2, subln_w, q_norm_w, w_uq,
                   kv_norm_w, w_ukv, w_o_diff, w_o_mla, w_out, norm_ffn_w, w_up, conv_w, conv_b, w_down,
                   *, tm=1024, tn=512, t_attn=512, tn_ffn=256, tm_down=512):
    S, D = x2d.shape
    H = DIFF_HEADS
    qkv_w = 3 * H * 2 * DIFF_HEAD_DIM
    q_rank = w_uq.shape[0]
    kv_rank = w_ukv.shape[0]
    lat_w = q_rank + kv_rank
    main_w = qkv_w + lat_w
    gate_start = main_w + MLA_ROPE_DIM
    lam_init = 0.8 - 0.6 * math.exp(-0.3 * l)

    cos, sin = _rope_tables(pos_col)
    h = _rmsnorm(x2d, norm_mix_w, BF16)

    z = _mm([dict(a=h, a_blk=0, K=D, w=w_in, w_row_blk=0, w_col_blk=0)], N=main_w, tm=tm, tn=tn,
            out_dtype=BF16, name="in_proj_main", epilogue="rope_lt",
            epi_arg=(2 * H * 2 * DIFF_HEAD_DIM) // tn, rope=(cos, sin))
    kr_dup = _mm([dict(a=h, a_blk=0, K=D, w=w_in, w_row_blk=0, w_col_blk=main_w // LANES)], N=LANES,
                 tm=tm, tn=LANES, out_dtype=BF16, name="in_proj_krope", epilogue="krope_dup",
                 rope=(cos, sin))
    gates = _mm([dict(a=h, a_blk=0, K=D, w=w_in[:, gate_start:], w_row_blk=0, w_col_blk=0)], N=2 * D,
                tm=tm, tn=tn, out_dtype=BF16, name="in_proj_gates", epilogue="sigmoid")

    o_d = _diff_attn(z, lq1, lk1, lq2, lk2, subln_w, lam_init=lam_init, T=t_attn)

    qk_dim = MLA_NOPE_DIM + MLA_ROPE_DIM
    w_uq3 = w_uq.reshape(q_rank, MLA_HEADS, qk_dim)
    w_uq_perm = jnp.concatenate([w_uq3[:, :, :MLA_NOPE_DIM].reshape(q_rank, -1),
                                 w_uq3[:, :, MLA_NOPE_DIM:].reshape(q_rank, -1)], axis=1)
    qm = _mm([dict(a=z, a_blk=qkv_w // q_rank, K=q_rank, w=w_uq_perm, w_row_blk=0, w_col_blk=0)],
             N=MLA_HEADS * qk_dim, tm=tm, tn=tn, out_dtype=BF16, name="mla_q_up", norm_w=q_norm_w,
             epilogue="scale_rope_ge", epi_arg=(qk_dim ** -0.5, (MLA_HEADS * MLA_NOPE_DIM) // tn),
             rope=(cos, sin))
    kv = _mm([dict(a=z, a_blk=(qkv_w + q_rank) // kv_rank, K=kv_rank, w=w_ukv, w_row_blk=0, w_col_blk=0)],
             N=w_ukv.shape[1], tm=tm, tn=tn, out_dtype=BF16, name="mla_kv_up", norm_w=kv_norm_w)
    o_m = _mla_attn(qm, kv, kr_dup, T=t_attn)

    y = _mm([dict(a=o_d, a_blk=0, K=o_d.shape[1], w=w_o_diff, w_row_blk=0, w_col_blk=0, gate=(gates, 0)),
             dict(a=o_m, a_blk=0, K=o_m.shape[1], w=w_o_mla, w_row_blk=0, w_col_blk=0, gate=(gates, D // tn))],
            N=D, tm=tm, tn=tn, out_dtype=BF16, name="branch_merge")
    x1 = _mm([dict(a=y, a_blk=0, K=D, w=w_out, w_row_blk=0, w_col_blk=0)], N=D, tm=tm, tn=tn,
             out_dtype=F32, name="out_proj", addend=x2d)

    h2 = _rmsnorm(x1, norm_ffn_w, BF16)
    act = _ffn_up(h2, w_up, conv_w, conv_b, tm=tm, tn=tn_ffn)
    d_ff = act.shape[1]
    k_half = d_ff // 2
    p0 = _mm([dict(a=act, a_blk=0, K=k_half, w=w_down, w_row_blk=0, w_col_blk=0)], N=D, tm=tm_down, tn=tn,
             out_dtype=F32, name="ffn_down_lo", addend=x1)
    x2 = _mm([dict(a=act, a_blk=1, K=k_half, w=w_down, w_row_blk=1, w_col_blk=0)], N=D, tm=tm_down, tn=tn,
             out_dtype=F32, name="ffn_down_hi", addend=p0)
    return x2


def kernel(x, positions, norm_mix_w, w_in, diff_lambda_q1, diff_lambda_k1, diff_lambda_q2, diff_lambda_k2, diff_subln_w, mla_q_norm_w, mla_w_uq, mla_kv_norm_w, mla_w_ukv, w_o_diff, w_o_mla, w_out, norm_ffn_w, ffn_w_up, ffn_conv_w, ffn_conv_b, ffn_w_down, final_norm_w):
    B, S, D = x.shape
    assert B == 1
    x2d = x.reshape(S, D)
    pos_col = positions.reshape(S, 1)
    for l in range(w_in.shape[0]):
        x2d = _block_forward(
            x2d, pos_col, l, norm_mix_w[l], w_in[l], diff_lambda_q1[l], diff_lambda_k1[l], diff_lambda_q2[l],
            diff_lambda_k2[l], diff_subln_w[l], mla_q_norm_w[l], mla_w_uq[l], mla_kv_norm_w[l], mla_w_ukv[l],
            w_o_diff[l], w_o_mla[l], w_out[l], norm_ffn_w[l], ffn_w_up[l], ffn_conv_w[l], ffn_conv_b[l],
            ffn_w_down[l])
    out = _rmsnorm(x2d, final_norm_w, F32)
    return out.reshape(B, S, D)
```

```python
import functools
import math

import jax
import jax.numpy as jnp
from jax import lax
from jax.experimental import pallas as pl
from jax.experimental.pallas import tpu as pltpu

BF16 = jnp.bfloat16
F32 = jnp.float32

LANES = 128
V7X_VMEM_LIMIT_BYTES = 56 << 20

DIFF_HEADS = 16
DIFF_HEAD_DIM = 64
MLA_HEADS = 16
MLA_NOPE_DIM = 128
MLA_ROPE_DIM = 64
MLA_V_DIM = 128
ROPE_THETA = 10000.0
NORM_EPS = 1e-6
DIFF_SUBLN_EPS = 1e-5
CONV_WIDTH = 3
MASK_VALUE = -1e30
LOG2_E = math.log2(math.e)


def _cparams(n_axes, flags=None):
    return pltpu.CompilerParams(
        dimension_semantics=("arbitrary",) * n_axes,
        vmem_limit_bytes=V7X_VMEM_LIMIT_BYTES,
        flags=flags,
    )


def _rope_table_kernel(pos_ref, freq_ref, sign_ref, cos_ref, sin_ref):
    ang = pos_ref[...].astype(F32) * freq_ref[...]
    cos_ref[...] = jnp.cos(ang)
    sin_ref[...] = jnp.sin(ang) * sign_ref[...]


def _rope_tables(pos_col, tm=512):
    S = pos_col.shape[0]
    half = MLA_ROPE_DIM // 2
    inv_freq = ROPE_THETA ** (-jnp.arange(0, MLA_ROPE_DIM, 2, dtype=F32) / MLA_ROPE_DIM)
    freq = jnp.tile(inv_freq, LANES // half).reshape(1, LANES)
    sign = jnp.tile(jnp.concatenate([-jnp.ones((half,), F32), jnp.ones((half,), F32)]),
                    LANES // (2 * half)).reshape(1, LANES)
    return pl.pallas_call(
        _rope_table_kernel,
        grid=(S // tm,),
        in_specs=[pl.BlockSpec((tm, 1), lambda i: (i, 0)),
                  pl.BlockSpec((1, LANES), lambda i: (0, 0)),
                  pl.BlockSpec((1, LANES), lambda i: (0, 0))],
        out_specs=[pl.BlockSpec((tm, LANES), lambda i: (i, 0)),
                   pl.BlockSpec((tm, LANES), lambda i: (i, 0))],
        out_shape=[jax.ShapeDtypeStruct((S, LANES), F32)] * 2,
        compiler_params=_cparams(1),
        name="rope_tables",
    )(pos_col, freq, sign)


def _rope_partner(zc):
    lane = lax.broadcasted_iota(jnp.int32, zc.shape, 1)
    first_half = (lane & 32) == 0
    return jnp.where(first_half, pltpu.roll(zc, 96, 1), pltpu.roll(zc, 32, 1))


def _rope_lanes(z, cos, sin):
    outs = []
    for c in range(z.shape[1] // LANES):
        zc = z[:, c * LANES:(c + 1) * LANES]
        outs.append(zc * cos + _rope_partner(zc) * sin)
    return outs[0] if len(outs) == 1 else jnp.concatenate(outs, axis=1)


def _rmsnorm_kernel(x_ref, w_ref, o_ref, *, eps):
    xf = x_ref[...].astype(F32)
    ms = jnp.mean(xf * xf, axis=1, keepdims=True)
    o_ref[...] = (xf * lax.rsqrt(ms + eps) * w_ref[...]).astype(o_ref.dtype)


def _rmsnorm(x, w, out_dtype, eps=NORM_EPS, tm=256):
    M, D = x.shape
    return pl.pallas_call(
        functools.partial(_rmsnorm_kernel, eps=eps),
        grid=(M // tm,),
        in_specs=[pl.BlockSpec((tm, D), lambda i: (i, 0)),
                  pl.BlockSpec((1, D), lambda i: (0, 0))],
        out_specs=pl.BlockSpec((tm, D), lambda i: (i, 0)),
        out_shape=jax.ShapeDtypeStruct((M, D), out_dtype),
        compiler_params=_cparams(1),
        name="rmsnorm",
    )(x, w.reshape(1, D).astype(F32))


def _mm_kernel(*refs, n_pairs, has_gate, has_norm, has_addend, has_rope, epilogue, epi_arg, eps):
    refs = list(refs)
    pair_refs = []
    for _ in range(n_pairs):
        a_ref = refs.pop(0)
        w_ref = refs.pop(0)
        g_ref = refs.pop(0) if has_gate else None
        pair_refs.append((a_ref, w_ref, g_ref))
    nw_ref = refs.pop(0) if has_norm else None
    add_ref = refs.pop(0) if has_addend else None
    cos_ref = refs.pop(0) if has_rope else None
    sin_ref = refs.pop(0) if has_rope else None
    o_ref = refs.pop(0)
    wbf_refs = refs

    j = pl.program_id(0)
    i = pl.program_id(1)

    @pl.when(i == 0)
    def _():
        for (_, w_ref, _), wbf in zip(pair_refs, wbf_refs):
            wbf[...] = w_ref[...].astype(BF16)

    acc = None
    for (a_ref, _, g_ref), wbf in zip(pair_refs, wbf_refs):
        a = a_ref[...]
        if has_norm:
            af = a.astype(F32)
            ms = jnp.mean(af * af, axis=1, keepdims=True)
            a = (af * lax.rsqrt(ms + eps) * nw_ref[...]).astype(BF16)
        d = jnp.dot(a, wbf[...], preferred_element_type=F32)
        if has_gate:
            d = d * g_ref[...].astype(F32)
        acc = d if acc is None else acc + d
    if has_addend:
        acc = acc + add_ref[...]

    if epilogue == "none":
        o_ref[...] = acc.astype(o_ref.dtype)
    elif epilogue == "sigmoid":
        o_ref[...] = (1.0 / (1.0 + jnp.exp(-acc))).astype(o_ref.dtype)
    elif epilogue == "rope_lt":
        n_rope, n_scaled, scale = epi_arg

        @pl.when(j < n_rope)
        def _():
            roped = _rope_lanes(acc, cos_ref[...], sin_ref[...])
            o_ref[...] = (roped * jnp.where(j < n_scaled, scale, 1.0)).astype(o_ref.dtype)

        @pl.when(j >= n_rope)
        def _():
            o_ref[...] = acc.astype(o_ref.dtype)
    elif epilogue == "scale_rope_ge":
        scale, first_rope_tile = epi_arg
        acc = acc * scale

        @pl.when(j >= first_rope_tile)
        def _():
            o_ref[...] = _rope_lanes(acc, cos_ref[...], sin_ref[...]).astype(o_ref.dtype)

        @pl.when(j < first_rope_tile)
        def _():
            o_ref[...] = acc.astype(o_ref.dtype)
    elif epilogue == "krope_dup":
        lane = lax.broadcasted_iota(jnp.int32, acc.shape, 1)
        kr = jnp.where(lane < MLA_ROPE_DIM, acc, 0.0)
        roped = kr * cos_ref[...] + _rope_partner(kr) * sin_ref[...]
        o_ref[...] = (roped + pltpu.roll(roped, MLA_ROPE_DIM, 1)).astype(o_ref.dtype)
    else:
        raise ValueError(epilogue)


def _mm(pairs, *, N, tm, tn, out_dtype, name, epilogue="none", epi_arg=None,
        norm_w=None, addend=None, rope=None, eps=NORM_EPS):
    M = pairs[0]["a"].shape[0]
    has_gate = pairs[0].get("gate") is not None
    args, in_specs, scratch = [], [], []
    for p in pairs:
        K = p["K"]
        args.append(p["a"])
        in_specs.append(pl.BlockSpec((tm, K), functools.partial(lambda j, i, b: (i, b), b=p["a_blk"])))
        args.append(p["w"])
        in_specs.append(pl.BlockSpec(
            (K, tn), functools.partial(lambda j, i, r, c: (r, c + j), r=p["w_row_blk"], c=p["w_col_blk"])))
        if has_gate:
            g, g_off = p["gate"]
            args.append(g)
            in_specs.append(pl.BlockSpec((tm, tn), functools.partial(lambda j, i, c: (i, c + j), c=g_off)))
        scratch.append(pltpu.VMEM((K, tn), BF16))
    if norm_w is not None:
        args.append(norm_w.reshape(1, -1).astype(F32))
        in_specs.append(pl.BlockSpec((1, norm_w.shape[-1]), lambda j, i: (0, 0)))
    if addend is not None:
        args.append(addend)
        in_specs.append(pl.BlockSpec((tm, tn), lambda j, i: (i, j)))
    if rope is not None:
        for t in rope:
            args.append(t)
            in_specs.append(pl.BlockSpec((tm, LANES), lambda j, i: (i, 0)))
    kern = functools.partial(
        _mm_kernel, n_pairs=len(pairs), has_gate=has_gate, has_norm=norm_w is not None,
        has_addend=addend is not None, has_rope=rope is not None, epilogue=epilogue, epi_arg=epi_arg, eps=eps)
    return pl.pallas_call(
        kern,
        grid=(N // tn, M // tm),
        in_specs=in_specs,
        out_specs=pl.BlockSpec((tm, tn), lambda j, i: (i, j)),
        out_shape=jax.ShapeDtypeStruct((M, N), out_dtype),
        scratch_shapes=scratch,
        compiler_params=_cparams(2),
        name=name,
    )(*args)


CONV_HALO = 8


def _ffn_up_kernel(a_ref, wg_ref, wv_ref, cwg_ref, cwv_ref, cbg_ref, cbv_ref, o_ref,
                   wgbf, wvbf, ug_buf, uv_buf, *, tm):
    i = pl.program_id(1)

    @pl.when(i == 0)
    def _():
        wgbf[...] = wg_ref[...].astype(BF16)
        wvbf[...] = wv_ref[...].astype(BF16)
        ug_buf[0:CONV_HALO, :] = jnp.zeros((CONV_HALO, ug_buf.shape[1]), F32)
        uv_buf[0:CONV_HALO, :] = jnp.zeros((CONV_HALO, uv_buf.shape[1]), F32)

    a = a_ref[...]

    def conv(w_bf, buf, cw_ref, cb_ref):
        u = jnp.dot(a, w_bf[...], preferred_element_type=F32)
        buf[CONV_HALO:CONV_HALO + tm, :] = u
        out = cb_ref[...] + cw_ref[0:1, :] * buf[CONV_HALO - 2:CONV_HALO - 2 + tm, :]
        out = out + cw_ref[1:2, :] * buf[CONV_HALO - 1:CONV_HALO - 1 + tm, :]
        out = out + cw_ref[2:3, :] * u
        buf[0:CONV_HALO, :] = buf[tm:tm + CONV_HALO, :]
        return out

    g = conv(wgbf, ug_buf, cwg_ref, cbg_ref)
    v = conv(wvbf, uv_buf, cwv_ref, cbv_ref)
    o_ref[...] = (g / (1.0 + jnp.exp(-g)) * v).astype(o_ref.dtype)


def _ffn_up(h, w_up, conv_w, conv_b, *, tm, tn):
    M, K = h.shape
    d_ff = w_up.shape[1] // 2
    nj = d_ff // tn
    cb = conv_b.reshape(1, -1)
    return pl.pallas_call(
        functools.partial(_ffn_up_kernel, tm=tm),
        grid=(nj, M // tm),
        in_specs=[
            pl.BlockSpec((tm, K), lambda j, i: (i, 0)),
            pl.BlockSpec((K, tn), lambda j, i: (0, j)),
            pl.BlockSpec((K, tn), lambda j, i: (0, nj + j)),
            pl.BlockSpec((CONV_WIDTH, tn), lambda j, i: (0, j)),
            pl.BlockSpec((CONV_WIDTH, tn), lambda j, i: (0, nj + j)),
            pl.BlockSpec((1, tn), lambda j, i: (0, j)),
            pl.BlockSpec((1, tn), lambda j, i: (0, nj + j)),
        ],
        out_specs=pl.BlockSpec((tm, tn), lambda j, i: (i, j)),
        out_shape=jax.ShapeDtypeStruct((M, d_ff), BF16),
        scratch_shapes=[pltpu.VMEM((K, tn), BF16), pltpu.VMEM((K, tn), BF16),
                        pltpu.VMEM((CONV_HALO + tm, tn), F32), pltpu.VMEM((CONV_HALO + tm, tn), F32)],
        compiler_params=_cparams(2),
        name="ffn_up_conv_gate",
    )(h, w_up, w_up, conv_w, conv_w, cb, cb)


ATTN_CHUNK = 256


def _nt_dot(a, b):
    return lax.dot_general(a, b, (((1,), (1,)), ((), ())), preferred_element_type=F32)


def _eye_bf16(n):
    r = lax.broadcasted_iota(jnp.int32, (n, n), 0)
    c = lax.broadcasted_iota(jnp.int32, (n, n), 1)
    return jnp.where(r == c, 1.0, 0.0).astype(BF16)


def _transpose_bf16(x, eye):
    return _nt_dot(eye, x).astype(BF16)


SUBLANES = 8
REDUCE_WAYS = 8


def _reduce_rows(x, op, final):
    n = x.shape[0]
    groups = [x[r * SUBLANES:(r + 1) * SUBLANES] for r in range(n // SUBLANES)]
    ways = min(REDUCE_WAYS, len(groups))
    parts = groups[:ways]
    for g, blk in enumerate(groups[ways:]):
        parts[g % ways] = op(parts[g % ways], blk)
    while len(parts) > 1:
        parts = [op(parts[i], parts[i + 1]) if i + 1 < len(parts) else parts[i] for i in range(0, len(parts), 2)]
    return final(parts[0], axis=0, keepdims=True)


def _softmax_chunk(load_s, m_ref, cols, mask_q0):
    def scores():
        sT = load_s()
        if mask_q0 is not None:
            key = lax.broadcasted_iota(jnp.int32, sT.shape, 0)
            qq = lax.broadcasted_iota(jnp.int32, sT.shape, 1) + mask_q0
            sT = jnp.where(qq >= key, sT, MASK_VALUE)
        return sT

    m_prev = m_ref[:, cols]
    m_new = jnp.maximum(m_prev, _reduce_rows(scores(), jnp.maximum, jnp.max))
    p = jnp.exp2(scores() - m_new)
    alpha = jnp.exp2(m_prev - m_new)
    m_ref[:, cols] = m_new
    return p.astype(BF16), alpha


ONES_ROWS = 16


def _store_vT(vT_ref, v_ref, eye, T):
    dv = v_ref.shape[1]
    for jb in range(vT_ref.shape[0]):
        vT_ref[jb, 0:dv, :] = _transpose_bf16(v_ref[jb * T:(jb + 1) * T, :], eye)
        vT_ref[jb, dv:dv + ONES_ROWS, :] = jnp.ones((ONES_ROWS, T), BF16)


def _causal_attn_loop(qi, k_ref, vT_ref, qT_ref, s_ref, p_ref, a_ref, m_ref, acc_ref, *, T, n_maps):
    CW = min(ATTN_CHUNK, T)
    per_map = T // CW
    chunks = [(slice(c * CW, (c + 1) * CW), (c % per_map) * CW) for c in range(n_maps * per_map)]
    m_ref[...] = jnp.full(m_ref.shape, MASK_VALUE, F32)
    acc_ref[...] = jnp.zeros(acc_ref.shape, F32)

    def qk(blk, slot, cols):
        k = k_ref[pl.ds(pl.multiple_of(blk * T, T), T), :]
        s_ref[slot, :, cols] = jnp.dot(k, qT_ref[:, cols], preferred_element_type=F32)

    def softmax(slot, cols, q0, diagonal):
        n = q0 + CW if diagonal else T
        p, alpha = _softmax_chunk(lambda: s_ref[slot, 0:n, cols], m_ref, cols, q0 if diagonal else None)
        p_ref[slot, 0:n, cols] = p
        a_ref[slot, :, cols] = alpha

    def pv(blk, slot, cols, q0, diagonal):
        n = q0 + CW if diagonal else T
        acc_ref[:, cols] = acc_ref[:, cols] * a_ref[slot, :, cols] + jnp.dot(
            vT_ref[blk, :, 0:n], p_ref[slot, 0:n, cols], preferred_element_type=F32)

    def step(t, slot):
        for cols, q0 in chunks:
            pv(t - 2, slot, cols, q0, False)
            qk(t, slot, cols)
            softmax(1 - slot, cols, q0, False)

    def drain(slot):
        for cols, q0 in chunks:
            softmax(slot, cols, q0, True)
        for cols, q0 in chunks:
            pv(qi, slot, cols, q0, True)

    for cols, _ in chunks:
        qk(0, 0, cols)

    @pl.when(qi == 0)
    def _():
        drain(0)

    @pl.when(qi >= 1)
    def _():
        for cols, q0 in chunks:
            qk(1, 1, cols)
            softmax(0, cols, q0, False)

    def body(u, carry):
        t = 2 + 2 * u
        step(t, 0)
        step(t + 1, 1)
        return carry

    lax.fori_loop(0, lax.shift_right_arithmetic(qi - 1, 1), body, 0)

    @pl.when(jnp.logical_and(qi >= 2, qi % 2 == 0))
    def _():
        step(qi, 0)
        for cols, q0 in chunks:
            pv(qi - 1, 1, cols, q0, False)
        drain(0)

    @pl.when(qi % 2 == 1)
    def _():
        for cols, q0 in chunks:
            pv(qi - 1, 0, cols, q0, False)
        drain(1)


def _diff_attn_kernel(q_ref, k_ref, v_ref, lq1_ref, lk1_ref, lq2_ref, lk2_ref, sw_ref, o_ref,
                      vT_ref, qT_ref, s_ref, p_ref, a_ref, m_ref, acc_ref, *, T, lam_init):
    qi = pl.program_id(1)
    eye = _eye_bf16(LANES)
    dv = v_ref.shape[1]

    @pl.when(qi == 0)
    def _():
        _store_vT(vT_ref, v_ref, eye, T)

    q = q_ref[...]
    lane = lax.broadcasted_iota(jnp.int32, q.shape, 1)
    zero = jnp.zeros_like(q)
    qT_ref[:, 0:T] = _transpose_bf16(jnp.where(lane < DIFF_HEAD_DIM, q, zero), eye)
    qT_ref[:, T:2 * T] = _transpose_bf16(jnp.where(lane >= DIFF_HEAD_DIM, q, zero), eye)

    _causal_attn_loop(qi, k_ref, vT_ref, qT_ref, s_ref, p_ref, a_ref, m_ref, acc_ref, T=T, n_maps=2)

    oT = acc_ref[0:dv, :] / acc_ref[dv:dv + 1, :]
    lam = (jnp.exp(jnp.sum(lq1_ref[...] * lk1_ref[...], axis=1, keepdims=True))
           - jnp.exp(jnp.sum(lq2_ref[...] * lk2_ref[...], axis=1, keepdims=True)) + lam_init)
    odT = oT[:, 0:T] - lam * oT[:, T:2 * T]
    ms = jnp.mean(odT * odT, axis=0, keepdims=True)
    outT = (odT * lax.rsqrt(ms + DIFF_SUBLN_EPS) * sw_ref[...] * (1.0 - lam_init)).astype(BF16)
    for r in range(T // LANES):
        o_ref[r * LANES:(r + 1) * LANES, :] = _transpose_bf16(
            outT[:, r * LANES:(r + 1) * LANES], eye).astype(o_ref.dtype)


def _diff_attn(zqkv, lq1, lk1, lq2, lk2, subln_w, *, lam_init, T):
    S = zqkv.shape[0]
    H = DIFF_HEADS
    hd = 2 * DIFF_HEAD_DIM
    vec = lambda a: a.reshape(1, -1).astype(F32)
    small = lambda n: pl.BlockSpec((1, n), lambda h, qi: (0, 0))
    return pl.pallas_call(
        functools.partial(_diff_attn_kernel, T=T, lam_init=lam_init),
        grid=(H, S // T),
        in_specs=[
            pl.BlockSpec((T, hd), lambda h, qi: (qi, h)),
            pl.BlockSpec((S, hd), lambda h, qi: (0, H + h)),
            pl.BlockSpec((S, hd), lambda h, qi: (0, 2 * H + h)),
            small(DIFF_HEAD_DIM), small(DIFF_HEAD_DIM), small(DIFF_HEAD_DIM), small(DIFF_HEAD_DIM),
            pl.BlockSpec((hd, 1), lambda h, qi: (0, 0)),
        ],
        out_specs=pl.BlockSpec((T, hd), lambda h, qi: (qi, h)),
        out_shape=jax.ShapeDtypeStruct((S, H * hd), BF16),
        scratch_shapes=[pltpu.VMEM((S // T, hd + ONES_ROWS, T), BF16), pltpu.VMEM((hd, 2 * T), BF16),
                        pltpu.VMEM((2, T, 2 * T), F32), pltpu.VMEM((2, T, 2 * T), BF16),
                        pltpu.VMEM((2, 1, 2 * T), F32), pltpu.VMEM((1, 2 * T), F32),
                        pltpu.VMEM((hd + ONES_ROWS, 2 * T), F32)],
        compiler_params=_cparams(2),
        name="diff_attention",
    )(zqkv, zqkv, zqkv, vec(lq1), vec(lk1), vec(lq2), vec(lk2), subln_w.reshape(-1, 1).astype(F32))


def _mla_attn_kernel(qn_ref, qr_ref, kn_ref, kr_ref, v_ref, o_ref, kcat, vT_ref, qT_ref, s_ref, p_ref, a_ref,
                     m_ref, acc_ref, *, T):
    h = pl.program_id(0)
    qi = pl.program_id(1)
    eye = _eye_bf16(LANES)
    dv = v_ref.shape[1]

    @pl.when(qi == 0)
    def _():
        kcat[:, 0:MLA_NOPE_DIM] = kn_ref[...]
        kcat[:, MLA_NOPE_DIM:] = kr_ref[...]
        _store_vT(vT_ref, v_ref, eye, T)

    qr = qr_ref[...]
    lane = lax.broadcasted_iota(jnp.int32, qr.shape, 1)
    lo = (h % 2) * MLA_ROPE_DIM
    mine = jnp.logical_and(lane >= lo, lane < lo + MLA_ROPE_DIM)
    qT_ref[0:MLA_NOPE_DIM, :] = _transpose_bf16(qn_ref[...], eye)
    qT_ref[MLA_NOPE_DIM:, :] = _transpose_bf16(jnp.where(mine, qr, jnp.zeros_like(qr)), eye)

    _causal_attn_loop(qi, kcat, vT_ref, qT_ref, s_ref, p_ref, a_ref, m_ref, acc_ref, T=T, n_maps=1)

    oT = (acc_ref[0:dv, :] / acc_ref[dv:dv + 1, :]).astype(BF16)
    for r in range(T // LANES):
        o_ref[r * LANES:(r + 1) * LANES, :] = _transpose_bf16(
            oT[:, r * LANES:(r + 1) * LANES], eye).astype(o_ref.dtype)


def _mla_attn(qm, kv, kr_dup, *, T):
    S = qm.shape[0]
    H = MLA_HEADS
    return pl.pallas_call(
        functools.partial(_mla_attn_kernel, T=T),
        grid=(H, S // T),
        in_specs=[
            pl.BlockSpec((T, MLA_NOPE_DIM), lambda h, qi: (qi, h)),
            pl.BlockSpec((T, LANES), lambda h, qi: (qi, H + h // 2)),
            pl.BlockSpec((S, MLA_NOPE_DIM), lambda h, qi: (0, 2 * h)),
            pl.BlockSpec((S, LANES), lambda h, qi: (0, 0)),
            pl.BlockSpec((S, MLA_V_DIM), lambda h, qi: (0, 2 * h + 1)),
        ],
        out_specs=pl.BlockSpec((T, MLA_V_DIM), lambda h, qi: (qi, h)),
        out_shape=jax.ShapeDtypeStruct((S, H * MLA_V_DIM), BF16),
        scratch_shapes=[pltpu.VMEM((S, MLA_NOPE_DIM + LANES), BF16),
                        pltpu.VMEM((S // T, MLA_V_DIM + ONES_ROWS, T), BF16),
                        pltpu.VMEM((MLA_NOPE_DIM + LANES, T), BF16),
                        pltpu.VMEM((2, T, T), F32), pltpu.VMEM((2, T, T), BF16), pltpu.VMEM((2, 1, T), F32),
                        pltpu.VMEM((1, T), F32), pltpu.VMEM((MLA_V_DIM + ONES_ROWS, T), F32)],
        compiler_params=_cparams(2),
        name="mla_attention",
    )(qm, qm, kv, kr_dup, kv)


def _block_forward(x2d, pos_col, l, norm_mix_w, w_in, lq1, lk1, lq2, lk2, subln_w, q_norm_w, w_uq,
                   kv_norm_w, w_ukv, w_o_diff, w_o_mla, w_out, norm_ffn_w, w_up, conv_w, conv_b, w_down,
                   *, tm=1024, tn=512, t_attn=512, tn_ffn=256, tm_down=512):
    S, D = x2d.shape
    H = DIFF_HEADS
    qkv_w = 3 * H * 2 * DIFF_HEAD_DIM
    q_rank = w_uq.shape[0]
    kv_rank = w_ukv.shape[0]
    lat_w = q_rank + kv_rank
    main_w = qkv_w + lat_w
    gate_start = main_w + MLA_ROPE_DIM
    lam_init = 0.8 - 0.6 * math.exp(-0.3 * l)

    cos, sin = _rope_tables(pos_col)
    h = _rmsnorm(x2d, norm_mix_w, BF16)

    z = _mm([dict(a=h, a_blk=0, K=D, w=w_in, w_row_blk=0, w_col_blk=0)], N=main_w, tm=tm, tn=tn,
            out_dtype=BF16, name="in_proj_main", epilogue="rope_lt",
            epi_arg=((2 * H * 2 * DIFF_HEAD_DIM) // tn, (H * 2 * DIFF_HEAD_DIM) // tn,
                     DIFF_HEAD_DIM ** -0.5 * LOG2_E), rope=(cos, sin))
    kr_dup = _mm([dict(a=h, a_blk=0, K=D, w=w_in, w_row_blk=0, w_col_blk=main_w // LANES)], N=LANES,
                 tm=tm, tn=LANES, out_dtype=BF16, name="in_proj_krope", epilogue="krope_dup",
                 rope=(cos, sin))
    gates = _mm([dict(a=h, a_blk=0, K=D, w=w_in[:, gate_start:], w_row_blk=0, w_col_blk=0)], N=2 * D,
                tm=tm, tn=tn, out_dtype=BF16, name="in_proj_gates", epilogue="sigmoid")

    o_d = _diff_attn(z, lq1, lk1, lq2, lk2, subln_w, lam_init=lam_init, T=t_attn)

    qk_dim = MLA_NOPE_DIM + MLA_ROPE_DIM
    w_uq3 = w_uq.reshape(q_rank, MLA_HEADS, qk_dim)
    w_uq_perm = jnp.concatenate([w_uq3[:, :, :MLA_NOPE_DIM].reshape(q_rank, -1),
                                 w_uq3[:, :, MLA_NOPE_DIM:].reshape(q_rank, -1)], axis=1)
    qm = _mm([dict(a=z, a_blk=qkv_w // q_rank, K=q_rank, w=w_uq_perm, w_row_blk=0, w_col_blk=0)],
             N=MLA_HEADS * qk_dim, tm=tm, tn=tn, out_dtype=BF16, name="mla_q_up", norm_w=q_norm_w,
             epilogue="scale_rope_ge", epi_arg=(qk_dim ** -0.5 * LOG2_E, (MLA_HEADS * MLA_NOPE_DIM) // tn),
             rope=(cos, sin))
    kv = _mm([dict(a=z, a_blk=(qkv_w + q_rank) // kv_rank, K=kv_rank, w=w_ukv, w_row_blk=0, w_col_blk=0)],
             N=w_ukv.shape[1], tm=tm, tn=tn, out_dtype=BF16, name="mla_kv_up", norm_w=kv_norm_w)
    o_m = _mla_attn(qm, kv, kr_dup, T=t_attn)

    y = _mm([dict(a=o_d, a_blk=0, K=o_d.shape[1], w=w_o_diff, w_row_blk=0, w_col_blk=0, gate=(gates, 0)),
             dict(a=o_m, a_blk=0, K=o_m.shape[1], w=w_o_mla, w_row_blk=0, w_col_blk=0, gate=(gates, D // tn))],
            N=D, tm=tm, tn=tn, out_dtype=BF16, name="branch_merge")
    x1 = _mm([dict(a=y, a_blk=0, K=D, w=w_out, w_row_blk=0, w_col_blk=0)], N=D, tm=tm, tn=tn,
             out_dtype=F32, name="out_proj", addend=x2d)

    h2 = _rmsnorm(x1, norm_ffn_w, BF16)
    act = _ffn_up(h2, w_up, conv_w, conv_b, tm=tm, tn=tn_ffn)
    d_ff = act.shape[1]
    k_half = d_ff // 2
    p0 = _mm([dict(a=act, a_blk=0, K=k_half, w=w_down, w_row_blk=0, w_col_blk=0)], N=D, tm=tm_down, tn=tn,
             out_dtype=F32, name="ffn_down_lo", addend=x1)
    x2 = _mm([dict(a=act, a_blk=1, K=k_half, w=w_down, w_row_blk=1, w_col_blk=0)], N=D, tm=tm_down, tn=tn,
             out_dtype=F32, name="ffn_down_hi", addend=p0)
    return x2


def kernel(x, positions, norm_mix_w, w_in, diff_lambda_q1, diff_lambda_k1, diff_lambda_q2, diff_lambda_k2, diff_subln_w, mla_q_norm_w, mla_w_uq, mla_kv_norm_w, mla_w_ukv, w_o_diff, w_o_mla, w_out, norm_ffn_w, ffn_w_up, ffn_conv_w, ffn_conv_b, ffn_w_down, final_norm_w):
    B, S, D = x.shape
    assert B == 1
    x2d = x.reshape(S, D)
    pos_col = positions.reshape(S, 1)
    for l in range(w_in.shape[0]):
        x2d = _block_forward(
            x2d, pos_col, l, norm_mix_w[l], w_in[l], diff_lambda_q1[l], diff_lambda_k1[l], diff_lambda_q2[l],
            diff_lambda_k2[l], diff_subln_w[l], mla_q_norm_w[l], mla_w_uq[l], mla_kv_norm_w[l], mla_w_ukv[l],
            w_o_diff[l], w_o_mla[l], w_out[l], norm_ffn_w[l], ffn_w_up[l], ffn_conv_w[l], ffn_conv_b[l],
            ffn_w_down[l])
    out = _rmsnorm(x2d, final_norm_w, F32)
    return out.reshape(B, S, D)
```

```python
import functools
import math

import jax
import jax.numpy as jnp
from jax import lax
from jax.experimental import pallas as pl
from jax.experimental.pallas import tpu as pltpu

BF16 = jnp.bfloat16
F32 = jnp.float32

LANES = 128
V7X_VMEM_LIMIT_BYTES = 56 << 20

DIFF_HEADS = 16
DIFF_HEAD_DIM = 64
MLA_HEADS = 16
MLA_NOPE_DIM = 128
MLA_ROPE_DIM = 64
MLA_V_DIM = 128
ROPE_THETA = 10000.0
NORM_EPS = 1e-6
DIFF_SUBLN_EPS = 1e-5
CONV_WIDTH = 3
MASK_VALUE = -1e30
LOG2_E = math.log2(math.e)


def _cparams(n_axes, flags=None):
    return pltpu.CompilerParams(
        dimension_semantics=("arbitrary",) * n_axes,
        vmem_limit_bytes=V7X_VMEM_LIMIT_BYTES,
        flags=flags,
    )


def _rope_table_kernel(pos_ref, freq_ref, sign_ref, cos_ref, sin_ref):
    ang = pos_ref[...].astype(F32) * freq_ref[...]
    cos_ref[...] = jnp.cos(ang)
    sin_ref[...] = jnp.sin(ang) * sign_ref[...]


def _rope_tables(pos_col, tm=512):
    S = pos_col.shape[0]
    half = MLA_ROPE_DIM // 2
    inv_freq = ROPE_THETA ** (-jnp.arange(0, MLA_ROPE_DIM, 2, dtype=F32) / MLA_ROPE_DIM)
    freq = jnp.tile(inv_freq, LANES // half).reshape(1, LANES)
    sign = jnp.tile(jnp.concatenate([-jnp.ones((half,), F32), jnp.ones((half,), F32)]),
                    LANES // (2 * half)).reshape(1, LANES)
    return pl.pallas_call(
        _rope_table_kernel,
        grid=(S // tm,),
        in_specs=[pl.BlockSpec((tm, 1), lambda i: (i, 0)),
                  pl.BlockSpec((1, LANES), lambda i: (0, 0)),
                  pl.BlockSpec((1, LANES), lambda i: (0, 0))],
        out_specs=[pl.BlockSpec((tm, LANES), lambda i: (i, 0)),
                   pl.BlockSpec((tm, LANES), lambda i: (i, 0))],
        out_shape=[jax.ShapeDtypeStruct((S, LANES), F32)] * 2,
        compiler_params=_cparams(1),
        name="rope_tables",
    )(pos_col, freq, sign)


def _rope_partner(zc):
    lane = lax.broadcasted_iota(jnp.int32, zc.shape, 1)
    first_half = (lane & 32) == 0
    return jnp.where(first_half, pltpu.roll(zc, 96, 1), pltpu.roll(zc, 32, 1))


def _rope_lanes(z, cos, sin):
    outs = []
    for c in range(z.shape[1] // LANES):
        zc = z[:, c * LANES:(c + 1) * LANES]
        outs.append(zc * cos + _rope_partner(zc) * sin)
    return outs[0] if len(outs) == 1 else jnp.concatenate(outs, axis=1)


def _rmsnorm_kernel(x_ref, w_ref, o_ref, *, eps):
    xf = x_ref[...].astype(F32)
    ms = jnp.mean(xf * xf, axis=1, keepdims=True)
    o_ref[...] = (xf * lax.rsqrt(ms + eps) * w_ref[...]).astype(o_ref.dtype)


def _rmsnorm(x, w, out_dtype, eps=NORM_EPS, tm=256):
    M, D = x.shape
    return pl.pallas_call(
        functools.partial(_rmsnorm_kernel, eps=eps),
        grid=(M // tm,),
        in_specs=[pl.BlockSpec((tm, D), lambda i: (i, 0)),
                  pl.BlockSpec((1, D), lambda i: (0, 0))],
        out_specs=pl.BlockSpec((tm, D), lambda i: (i, 0)),
        out_shape=jax.ShapeDtypeStruct((M, D), out_dtype),
        compiler_params=_cparams(1),
        name="rmsnorm",
    )(x, w.reshape(1, D).astype(F32))


def _tile_walk(ni, n_steps):
    def cur(s):
        c = jnp.minimum(s, n_steps - 1)
        return c % ni, c // ni

    def prev(s):
        p = jnp.maximum(s - 1, 0)
        return p % ni, p // ni

    return cur, prev


MM_SUB_ROWS = 256


def _mm_kernel(*refs, n_pairs, has_gate, has_norm, has_addend, has_rope, epilogue, epi_arg, eps, ni, n_steps,
               tm, sub):
    refs = list(refs)
    pair_refs = []
    for _ in range(n_pairs):
        a_ref = refs.pop(0)
        w_ref = refs.pop(0)
        g_ref = refs.pop(0) if has_gate else None
        pair_refs.append((a_ref, w_ref, g_ref))
    nw_ref = refs.pop(0) if has_norm else None
    add_ref = refs.pop(0) if has_addend else None
    cos_ref = refs.pop(0) if has_rope else None
    sin_ref = refs.pop(0) if has_rope else None
    o_ref = refs.pop(0)
    wbf_refs = refs[:n_pairs]
    raw_refs = refs[n_pairs:]

    s = pl.program_id(0)
    cur, prev = _tile_walk(ni, n_steps)
    i_cur, _ = cur(s)
    _, j = prev(s)

    @pl.when(s == 0)
    def _():
        for raw in raw_refs:
            raw[...] = jnp.zeros(raw.shape, F32)

    @pl.when(i_cur == 0)
    def _():
        for (_, w_ref, _), wbf in zip(pair_refs, wbf_refs):
            wbf[...] = w_ref[...].astype(BF16)

    def finish_previous(rows, roped):
        if has_gate:
            acc = None
            for (_, _, g_ref), raw in zip(pair_refs, raw_refs):
                d = raw[rows, :] * g_ref[rows, :].astype(F32)
                acc = d if acc is None else acc + d
        else:
            acc = raw_refs[0][rows, :]
        if has_addend:
            acc = acc + add_ref[rows, :]
        if epilogue == "sigmoid":
            out = 1.0 / (1.0 + jnp.exp(-acc))
        elif epilogue == "rope_lt":
            _, n_scaled, scale = epi_arg
            out = acc
            if roped:
                out = _rope_lanes(acc, cos_ref[rows, :], sin_ref[rows, :]) * jnp.where(j < n_scaled, scale, 1.0)
        elif epilogue == "scale_rope_ge":
            out = acc * epi_arg[0]
            if roped:
                out = _rope_lanes(out, cos_ref[rows, :], sin_ref[rows, :])
        elif epilogue == "krope_dup":
            lane = lax.broadcasted_iota(jnp.int32, acc.shape, 1)
            kr = jnp.where(lane < MLA_ROPE_DIM, acc, 0.0)
            r = kr * cos_ref[rows, :] + _rope_partner(kr) * sin_ref[rows, :]
            out = r + pltpu.roll(r, MLA_ROPE_DIM, 1)
        else:
            out = acc
        o_ref[rows, :] = out.astype(o_ref.dtype)

    def multiply_current(rows):
        acc = None
        for p, ((a_ref, _, _), wbf) in enumerate(zip(pair_refs, wbf_refs)):
            a = a_ref[rows, :]
            if has_norm:
                af = a.astype(F32)
                ms = jnp.mean(af * af, axis=1, keepdims=True)
                a = (af * lax.rsqrt(ms + eps) * nw_ref[...]).astype(BF16)
            d = jnp.dot(a, wbf[...], preferred_element_type=F32)
            if has_gate:
                raw_refs[p][rows, :] = d
            else:
                acc = d if acc is None else acc + d
        if not has_gate:
            raw_refs[0][rows, :] = acc

    def step(roped):
        for r in range(tm // sub):
            rows = slice(r * sub, (r + 1) * sub)
            finish_previous(rows, roped)
            multiply_current(rows)

    if epilogue == "rope_lt":
        pl.when(j < epi_arg[0])(lambda: step(True))
        pl.when(j >= epi_arg[0])(lambda: step(False))
    elif epilogue == "scale_rope_ge":
        pl.when(j >= epi_arg[1])(lambda: step(True))
        pl.when(j < epi_arg[1])(lambda: step(False))
    else:
        step(False)


def _mm(pairs, *, N, tm, tn, out_dtype, name, epilogue="none", epi_arg=None,
        norm_w=None, addend=None, rope=None, eps=NORM_EPS):
    M = pairs[0]["a"].shape[0]
    ni, nj = M // tm, N // tn
    n_steps = ni * nj
    cur, prev = _tile_walk(ni, n_steps)
    has_gate = pairs[0].get("gate") is not None
    args, in_specs, wbf_scratch = [], [], []

    def at_cur(fn):
        return lambda s: fn(*cur(s))

    def at_prev(fn):
        return lambda s: fn(*prev(s))

    for p in pairs:
        K = p["K"]
        args.append(p["a"])
        in_specs.append(pl.BlockSpec((tm, K), at_cur(functools.partial(lambda i, j, b: (i, b), b=p["a_blk"]))))
        args.append(p["w"])
        in_specs.append(pl.BlockSpec((K, tn), at_cur(functools.partial(
            lambda i, j, r, c: (r, c + j), r=p["w_row_blk"], c=p["w_col_blk"]))))
        if has_gate:
            g, g_off = p["gate"]
            args.append(g)
            in_specs.append(pl.BlockSpec((tm, tn), at_prev(functools.partial(lambda i, j, c: (i, c + j), c=g_off))))
        wbf_scratch.append(pltpu.VMEM((K, tn), BF16))
    if norm_w is not None:
        args.append(norm_w.reshape(1, -1).astype(F32))
        in_specs.append(pl.BlockSpec((1, norm_w.shape[-1]), lambda s: (0, 0)))
    if addend is not None:
        args.append(addend)
        in_specs.append(pl.BlockSpec((tm, tn), at_prev(lambda i, j: (i, j))))
    if rope is not None:
        for t in rope:
            args.append(t)
            in_specs.append(pl.BlockSpec((tm, LANES), at_prev(lambda i, j: (i, 0))))
    raw_scratch = [pltpu.VMEM((tm, tn), F32)] * (len(pairs) if has_gate else 1)
    kern = functools.partial(
        _mm_kernel, n_pairs=len(pairs), has_gate=has_gate, has_norm=norm_w is not None,
        has_addend=addend is not None, has_rope=rope is not None, epilogue=epilogue, epi_arg=epi_arg, eps=eps,
        ni=ni, n_steps=n_steps, tm=tm, sub=min(MM_SUB_ROWS, tm))
    return pl.pallas_call(
        kern,
        grid=(n_steps + 1,),
        in_specs=in_specs,
        out_specs=pl.BlockSpec((tm, tn), at_prev(lambda i, j: (i, j))),
        out_shape=jax.ShapeDtypeStruct((M, N), out_dtype),
        scratch_shapes=wbf_scratch + raw_scratch,
        compiler_params=_cparams(1),
        name=name,
    )(*args)


CONV_HALO = 8


def _ffn_up_kernel(a_ref, wg_ref, wv_ref, cwg_ref, cwv_ref, cbg_ref, cbv_ref, o_ref,
                   wgbf, wvbf, ug_buf, uv_buf, *, tm, ni, n_steps):
    s = pl.program_id(0)
    cur, _ = _tile_walk(ni, n_steps)
    i_cur, _ = cur(s)

    @pl.when(s == 0)
    def _():
        ug_buf[...] = jnp.zeros(ug_buf.shape, F32)
        uv_buf[...] = jnp.zeros(uv_buf.shape, F32)

    @pl.when(i_cur == 0)
    def _():
        wgbf[...] = wg_ref[...].astype(BF16)
        wvbf[...] = wv_ref[...].astype(BF16)

    sub = min(MM_SUB_ROWS, tm)

    def conv(buf, r0, cw_ref, cb_ref):
        lo = CONV_HALO + r0
        out = cb_ref[...] + cw_ref[0:1, :] * buf[lo - 2:lo - 2 + sub, :]
        out = out + cw_ref[1:2, :] * buf[lo - 1:lo - 1 + sub, :]
        return out + cw_ref[2:3, :] * buf[lo:lo + sub, :]

    keep = jnp.where(i_cur == 0, 0.0, 1.0)
    halo_g = ug_buf[tm:tm + CONV_HALO, :] * keep
    halo_v = uv_buf[tm:tm + CONV_HALO, :] * keep

    for r in reversed(range(tm // sub)):
        r0 = r * sub
        g = conv(ug_buf, r0, cwg_ref, cbg_ref)
        v = conv(uv_buf, r0, cwv_ref, cbv_ref)
        o_ref[r0:r0 + sub, :] = (g / (1.0 + jnp.exp(-g)) * v).astype(o_ref.dtype)
        a = a_ref[r0:r0 + sub, :]
        ug_buf[CONV_HALO + r0:CONV_HALO + r0 + sub, :] = jnp.dot(a, wgbf[...], preferred_element_type=F32)
        uv_buf[CONV_HALO + r0:CONV_HALO + r0 + sub, :] = jnp.dot(a, wvbf[...], preferred_element_type=F32)

    ug_buf[0:CONV_HALO, :] = halo_g
    uv_buf[0:CONV_HALO, :] = halo_v


def _ffn_up(h, w_up, conv_w, conv_b, *, tm, tn):
    M, K = h.shape
    d_ff = w_up.shape[1] // 2
    ni, nj = M // tm, d_ff // tn
    n_steps = ni * nj
    cur, prev = _tile_walk(ni, n_steps)
    cb = conv_b.reshape(1, -1)

    def at_cur(fn):
        return lambda s: fn(*cur(s))

    def at_prev(fn):
        return lambda s: fn(*prev(s))

    return pl.pallas_call(
        functools.partial(_ffn_up_kernel, tm=tm, ni=ni, n_steps=n_steps),
        grid=(n_steps + 1,),
        in_specs=[
            pl.BlockSpec((tm, K), at_cur(lambda i, j: (i, 0))),
            pl.BlockSpec((K, tn), at_cur(lambda i, j: (0, j))),
            pl.BlockSpec((K, tn), at_cur(lambda i, j: (0, nj + j))),
            pl.BlockSpec((CONV_WIDTH, tn), at_prev(lambda i, j: (0, j))),
            pl.BlockSpec((CONV_WIDTH, tn), at_prev(lambda i, j: (0, nj + j))),
            pl.BlockSpec((1, tn), at_prev(lambda i, j: (0, j))),
            pl.BlockSpec((1, tn), at_prev(lambda i, j: (0, nj + j))),
        ],
        out_specs=pl.BlockSpec((tm, tn), at_prev(lambda i, j: (i, j))),
        out_shape=jax.ShapeDtypeStruct((M, d_ff), BF16),
        scratch_shapes=[pltpu.VMEM((K, tn), BF16), pltpu.VMEM((K, tn), BF16),
                        pltpu.VMEM((CONV_HALO + tm, tn), F32), pltpu.VMEM((CONV_HALO + tm, tn), F32)],
        compiler_params=_cparams(1),
        name="ffn_up_conv_gate",
    )(h, w_up, w_up, conv_w, conv_w, cb, cb)


ATTN_CHUNK = 256


def _nt_dot(a, b):
    return lax.dot_general(a, b, (((1,), (1,)), ((), ())), preferred_element_type=F32)


def _eye_bf16(n):
    r = lax.broadcasted_iota(jnp.int32, (n, n), 0)
    c = lax.broadcasted_iota(jnp.int32, (n, n), 1)
    return jnp.where(r == c, 1.0, 0.0).astype(BF16)


def _transpose_bf16(x, eye):
    return _nt_dot(eye, x).astype(BF16)


SUBLANES = 8
REDUCE_WAYS = 8


def _reduce_rows(x, op, final):
    n = x.shape[0]
    groups = [x[r * SUBLANES:(r + 1) * SUBLANES] for r in range(n // SUBLANES)]
    ways = min(REDUCE_WAYS, len(groups))
    parts = groups[:ways]
    for g, blk in enumerate(groups[ways:]):
        parts[g % ways] = op(parts[g % ways], blk)
    while len(parts) > 1:
        parts = [op(parts[i], parts[i + 1]) if i + 1 < len(parts) else parts[i] for i in range(0, len(parts), 2)]
    return final(parts[0], axis=0, keepdims=True)


def _softmax_chunk(load_s, m_ref, cols, mask_q0):
    def scores():
        sT = load_s()
        if mask_q0 is not None:
            key = lax.broadcasted_iota(jnp.int32, sT.shape, 0)
            qq = lax.broadcasted_iota(jnp.int32, sT.shape, 1) + mask_q0
            sT = jnp.where(qq >= key, sT, MASK_VALUE)
        return sT

    m_prev = m_ref[:, cols]
    m_new = jnp.maximum(m_prev, _reduce_rows(scores(), jnp.maximum, jnp.max))
    p = jnp.exp2(scores() - m_new)
    alpha = jnp.exp2(m_prev - m_new)
    m_ref[:, cols] = m_new
    return p.astype(BF16), alpha


ONES_ROWS = 16


def _store_vT(vT_ref, v_ref, eye, T):
    dv = v_ref.shape[1]
    for jb in range(vT_ref.shape[0]):
        vT_ref[jb, 0:dv, :] = _transpose_bf16(v_ref[jb * T:(jb + 1) * T, :], eye)
        vT_ref[jb, dv:dv + ONES_ROWS, :] = jnp.ones((ONES_ROWS, T), BF16)


def _causal_attn_loop(qi, k_ref, vT_ref, qT_ref, s_ref, p_ref, a_ref, m_ref, acc_ref, *, T, n_maps):
    CW = min(ATTN_CHUNK, T)
    per_map = T // CW
    chunks = [(slice(c * CW, (c + 1) * CW), (c % per_map) * CW) for c in range(n_maps * per_map)]
    m_ref[...] = jnp.full(m_ref.shape, MASK_VALUE, F32)
    acc_ref[...] = jnp.zeros(acc_ref.shape, F32)

    def qk(blk, slot, cols):
        k = k_ref[pl.ds(pl.multiple_of(blk * T, T), T), :]
        s_ref[slot, :, cols] = jnp.dot(k, qT_ref[:, cols], preferred_element_type=F32)

    def softmax(slot, cols, q0, diagonal):
        n = q0 + CW if diagonal else T
        p, alpha = _softmax_chunk(lambda: s_ref[slot, 0:n, cols], m_ref, cols, q0 if diagonal else None)
        p_ref[slot, 0:n, cols] = p
        a_ref[slot, :, cols] = alpha

    def pv(blk, slot, cols, q0, diagonal):
        n = q0 + CW if diagonal else T
        acc_ref[:, cols] = acc_ref[:, cols] * a_ref[slot, :, cols] + jnp.dot(
            vT_ref[blk, :, 0:n], p_ref[slot, 0:n, cols], preferred_element_type=F32)

    def step(t, slot):
        for cols, q0 in chunks:
            pv(t - 2, slot, cols, q0, False)
            qk(t, slot, cols)
            softmax(1 - slot, cols, q0, False)

    def drain(slot):
        for cols, q0 in chunks:
            softmax(slot, cols, q0, True)
        for cols, q0 in chunks:
            pv(qi, slot, cols, q0, True)

    for cols, _ in chunks:
        qk(0, 0, cols)

    @pl.when(qi == 0)
    def _():
        drain(0)

    @pl.when(qi >= 1)
    def _():
        for cols, q0 in chunks:
            qk(1, 1, cols)
            softmax(0, cols, q0, False)

    def body(u, carry):
        t = 2 + 2 * u
        step(t, 0)
        step(t + 1, 1)
        return carry

    lax.fori_loop(0, lax.shift_right_arithmetic(qi - 1, 1), body, 0)

    @pl.when(jnp.logical_and(qi >= 2, qi % 2 == 0))
    def _():
        step(qi, 0)
        for cols, q0 in chunks:
            pv(qi - 1, 1, cols, q0, False)
        drain(0)

    @pl.when(qi % 2 == 1)
    def _():
        for cols, q0 in chunks:
            pv(qi - 1, 0, cols, q0, False)
        drain(1)


def _diff_attn_kernel(q_ref, k_ref, v_ref, lq1_ref, lk1_ref, lq2_ref, lk2_ref, sw_ref, o_ref,
                      vT_ref, qT_ref, s_ref, p_ref, a_ref, m_ref, acc_ref, *, T, lam_init):
    qi = pl.program_id(1)
    eye = _eye_bf16(LANES)
    dv = v_ref.shape[1]

    @pl.when(qi == 0)
    def _():
        _store_vT(vT_ref, v_ref, eye, T)

    q = q_ref[...]
    lane = lax.broadcasted_iota(jnp.int32, q.shape, 1)
    zero = jnp.zeros_like(q)
    qT_ref[:, 0:T] = _transpose_bf16(jnp.where(lane < DIFF_HEAD_DIM, q, zero), eye)
    qT_ref[:, T:2 * T] = _transpose_bf16(jnp.where(lane >= DIFF_HEAD_DIM, q, zero), eye)

    _causal_attn_loop(qi, k_ref, vT_ref, qT_ref, s_ref, p_ref, a_ref, m_ref, acc_ref, T=T, n_maps=2)

    oT = acc_ref[0:dv, :] / acc_ref[dv:dv + 1, :]
    lam = (jnp.exp(jnp.sum(lq1_ref[...] * lk1_ref[...], axis=1, keepdims=True))
           - jnp.exp(jnp.sum(lq2_ref[...] * lk2_ref[...], axis=1, keepdims=True)) + lam_init)
    odT = oT[:, 0:T] - lam * oT[:, T:2 * T]
    ms = jnp.mean(odT * odT, axis=0, keepdims=True)
    outT = (odT * lax.rsqrt(ms + DIFF_SUBLN_EPS) * sw_ref[...] * (1.0 - lam_init)).astype(BF16)
    for r in range(T // LANES):
        o_ref[r * LANES:(r + 1) * LANES, :] = _transpose_bf16(
            outT[:, r * LANES:(r + 1) * LANES], eye).astype(o_ref.dtype)


def _diff_attn(zqkv, lq1, lk1, lq2, lk2, subln_w, *, lam_init, T):
    S = zqkv.shape[0]
    H = DIFF_HEADS
    hd = 2 * DIFF_HEAD_DIM
    vec = lambda a: a.reshape(1, -1).astype(F32)
    small = lambda n: pl.BlockSpec((1, n), lambda h, qi: (0, 0))
    return pl.pallas_call(
        functools.partial(_diff_attn_kernel, T=T, lam_init=lam_init),
        grid=(H, S // T),
        in_specs=[
            pl.BlockSpec((T, hd), lambda h, qi: (qi, h)),
            pl.BlockSpec((S, hd), lambda h, qi: (0, H + h)),
            pl.BlockSpec((S, hd), lambda h, qi: (0, 2 * H + h)),
            small(DIFF_HEAD_DIM), small(DIFF_HEAD_DIM), small(DIFF_HEAD_DIM), small(DIFF_HEAD_DIM),
            pl.BlockSpec((hd, 1), lambda h, qi: (0, 0)),
        ],
        out_specs=pl.BlockSpec((T, hd), lambda h, qi: (qi, h)),
        out_shape=jax.ShapeDtypeStruct((S, H * hd), BF16),
        scratch_shapes=[pltpu.VMEM((S // T, hd + ONES_ROWS, T), BF16), pltpu.VMEM((hd, 2 * T), BF16),
                        pltpu.VMEM((2, T, 2 * T), F32), pltpu.VMEM((2, T, 2 * T), BF16),
                        pltpu.VMEM((2, 1, 2 * T), F32), pltpu.VMEM((1, 2 * T), F32),
                        pltpu.VMEM((hd + ONES_ROWS, 2 * T), F32)],
        compiler_params=_cparams(2),
        name="diff_attention",
    )(zqkv, zqkv, zqkv, vec(lq1), vec(lk1), vec(lq2), vec(lk2), subln_w.reshape(-1, 1).astype(F32))


def _mla_attn_kernel(qn_ref, qr_ref, kn_ref, kr_ref, v_ref, o_ref, kcat, vT_ref, qT_ref, s_ref, p_ref, a_ref,
                     m_ref, acc_ref, *, T):
    h = pl.program_id(0)
    qi = pl.program_id(1)
    eye = _eye_bf16(LANES)
    dv = v_ref.shape[1]

    @pl.when(qi == 0)
    def _():
        kcat[:, 0:MLA_NOPE_DIM] = kn_ref[...]
        kcat[:, MLA_NOPE_DIM:] = kr_ref[...]
        _store_vT(vT_ref, v_ref, eye, T)

    qr = qr_ref[...]
    lane = lax.broadcasted_iota(jnp.int32, qr.shape, 1)
    lo = (h % 2) * MLA_ROPE_DIM
    mine = jnp.logical_and(lane >= lo, lane < lo + MLA_ROPE_DIM)
    qT_ref[0:MLA_NOPE_DIM, :] = _transpose_bf16(qn_ref[...], eye)
    qT_ref[MLA_NOPE_DIM:, :] = _transpose_bf16(jnp.where(mine, qr, jnp.zeros_like(qr)), eye)

    _causal_attn_loop(qi, kcat, vT_ref, qT_ref, s_ref, p_ref, a_ref, m_ref, acc_ref, T=T, n_maps=1)

    oT = (acc_ref[0:dv, :] / acc_ref[dv:dv + 1, :]).astype(BF16)
    for r in range(T // LANES):
        o_ref[r * LANES:(r + 1) * LANES, :] = _transpose_bf16(
            oT[:, r * LANES:(r + 1) * LANES], eye).astype(o_ref.dtype)


def _mla_attn(qm, kv, kr_dup, *, T):
    S = qm.shape[0]
    H = MLA_HEADS
    return pl.pallas_call(
        functools.partial(_mla_attn_kernel, T=T),
        grid=(H, S // T),
        in_specs=[
            pl.BlockSpec((T, MLA_NOPE_DIM), lambda h, qi: (qi, h)),
            pl.BlockSpec((T, LANES), lambda h, qi: (qi, H + h // 2)),
            pl.BlockSpec((S, MLA_NOPE_DIM), lambda h, qi: (0, 2 * h)),
            pl.BlockSpec((S, LANES), lambda h, qi: (0, 0)),
            pl.BlockSpec((S, MLA_V_DIM), lambda h, qi: (0, 2 * h + 1)),
        ],
        out_specs=pl.BlockSpec((T, MLA_V_DIM), lambda h, qi: (qi, h)),
        out_shape=jax.ShapeDtypeStruct((S, H * MLA_V_DIM), BF16),
        scratch_shapes=[pltpu.VMEM((S, MLA_NOPE_DIM + LANES), BF16),
                        pltpu.VMEM((S // T, MLA_V_DIM + ONES_ROWS, T), BF16),
                        pltpu.VMEM((MLA_NOPE_DIM + LANES, T), BF16),
                        pltpu.VMEM((2, T, T), F32), pltpu.VMEM((2, T, T), BF16), pltpu.VMEM((2, 1, T), F32),
                        pltpu.VMEM((1, T), F32), pltpu.VMEM((MLA_V_DIM + ONES_ROWS, T), F32)],
        compiler_params=_cparams(2),
        name="mla_attention",
    )(qm, qm, kv, kr_dup, kv)


def _block_forward(x2d, pos_col, l, norm_mix_w, w_in, lq1, lk1, lq2, lk2, subln_w, q_norm_w, w_uq,
                   kv_norm_w, w_ukv, w_o_diff, w_o_mla, w_out, norm_ffn_w, w_up, conv_w, conv_b, w_down,
                   *, tm=1024, tn=512, t_attn=512, tn_ffn=256, tm_down=512):
    S, D = x2d.shape
    H = DIFF_HEADS
    qkv_w = 3 * H * 2 * DIFF_HEAD_DIM
    q_rank = w_uq.shape[0]
    kv_rank = w_ukv.shape[0]
    lat_w = q_rank + kv_rank
    main_w = qkv_w + lat_w
    gate_start = main_w + MLA_ROPE_DIM
    lam_init = 0.8 - 0.6 * math.exp(-0.3 * l)

    cos, sin = _rope_tables(pos_col)
    h = _rmsnorm(x2d, norm_mix_w, BF16)

    z = _mm([dict(a=h, a_blk=0, K=D, w=w_in, w_row_blk=0, w_col_blk=0)], N=main_w, tm=tm, tn=tn,
            out_dtype=BF16, name="in_proj_main", epilogue="rope_lt",
            epi_arg=((2 * H * 2 * DIFF_HEAD_DIM) // tn, (H * 2 * DIFF_HEAD_DIM) // tn,
                     DIFF_HEAD_DIM ** -0.5 * LOG2_E), rope=(cos, sin))
    kr_dup = _mm([dict(a=h, a_blk=0, K=D, w=w_in, w_row_blk=0, w_col_blk=main_w // LANES)], N=LANES,
                 tm=tm, tn=LANES, out_dtype=BF16, name="in_proj_krope", epilogue="krope_dup",
                 rope=(cos, sin))
    gates = _mm([dict(a=h, a_blk=0, K=D, w=w_in[:, gate_start:], w_row_blk=0, w_col_blk=0)], N=2 * D,
                tm=tm, tn=tn, out_dtype=BF16, name="in_proj_gates", epilogue="sigmoid")

    o_d = _diff_attn(z, lq1, lk1, lq2, lk2, subln_w, lam_init=lam_init, T=t_attn)

    qk_dim = MLA_NOPE_DIM + MLA_ROPE_DIM
    w_uq3 = w_uq.reshape(q_rank, MLA_HEADS, qk_dim)
    w_uq_perm = jnp.concatenate([w_uq3[:, :, :MLA_NOPE_DIM].reshape(q_rank, -1),
                                 w_uq3[:, :, MLA_NOPE_DIM:].reshape(q_rank, -1)], axis=1)
    qm = _mm([dict(a=z, a_blk=qkv_w // q_rank, K=q_rank, w=w_uq_perm, w_row_blk=0, w_col_blk=0)],
             N=MLA_HEADS * qk_dim, tm=tm, tn=tn, out_dtype=BF16, name="mla_q_up", norm_w=q_norm_w,
             epilogue="scale_rope_ge", epi_arg=(qk_dim ** -0.5 * LOG2_E, (MLA_HEADS * MLA_NOPE_DIM) // tn),
             rope=(cos, sin))
    kv = _mm([dict(a=z, a_blk=(qkv_w + q_rank) // kv_rank, K=kv_rank, w=w_ukv, w_row_blk=0, w_col_blk=0)],
             N=w_ukv.shape[1], tm=tm, tn=tn, out_dtype=BF16, name="mla_kv_up", norm_w=kv_norm_w)
    o_m = _mla_attn(qm, kv, kr_dup, T=t_attn)

    y = _mm([dict(a=o_d, a_blk=0, K=o_d.shape[1], w=w_o_diff, w_row_blk=0, w_col_blk=0, gate=(gates, 0)),
             dict(a=o_m, a_blk=0, K=o_m.shape[1], w=w_o_mla, w_row_blk=0, w_col_blk=0, gate=(gates, D // tn))],
            N=D, tm=tm, tn=tn, out_dtype=BF16, name="branch_merge")
    x1 = _mm([dict(a=y, a_blk=0, K=D, w=w_out, w_row_blk=0, w_col_blk=0)], N=D, tm=tm, tn=tn,
             out_dtype=F32, name="out_proj", addend=x2d)

    h2 = _rmsnorm(x1, norm_ffn_w, BF16)
    act = _ffn_up(h2, w_up, conv_w, conv_b, tm=tm, tn=tn_ffn)
    d_ff = act.shape[1]
    k_half = d_ff // 2
    p0 = _mm([dict(a=act, a_blk=0, K=k_half, w=w_down, w_row_blk=0, w_col_blk=0)], N=D, tm=tm_down, tn=tn,
             out_dtype=F32, name="ffn_down_lo", addend=x1)
    x2 = _mm([dict(a=act, a_blk=1, K=k_half, w=w_down, w_row_blk=1, w_col_blk=0)], N=D, tm=tm_down, tn=tn,
             out_dtype=F32, name="ffn_down_hi", addend=p0)
    return x2


def kernel(x, positions, norm_mix_w, w_in, diff_lambda_q1, diff_lambda_k1, diff_lambda_q2, diff_lambda_k2, diff_subln_w, mla_q_norm_w, mla_w_uq, mla_kv_norm_w, mla_w_ukv, w_o_diff, w_o_mla, w_out, norm_ffn_w, ffn_w_up, ffn_conv_w, ffn_conv_b, ffn_w_down, final_norm_w):
    B, S, D = x.shape
    assert B == 1
    x2d = x.reshape(S, D)
    pos_col = positions.reshape(S, 1)
    for l in range(w_in.shape[0]):
        x2d = _block_forward(
            x2d, pos_col, l, norm_mix_w[l], w_in[l], diff_lambda_q1[l], diff_lambda_k1[l], diff_lambda_q2[l],
            diff_lambda_k2[l], diff_subln_w[l], mla_q_norm_w[l], mla_w_uq[l], mla_kv_norm_w[l], mla_w_ukv[l],
            w_o_diff[l], w_o_mla[l], w_out[l], norm_ffn_w[l], ffn_w_up[l], ffn_conv_w[l], ffn_conv_b[l],
            ffn_w_down[l])
    out = _rmsnorm(x2d, final_norm_w, F32)
    return out.reshape(B, S, D)
```

```python
import functools
import math

import jax
import jax.numpy as jnp
from jax import lax
from jax.experimental import pallas as pl
from jax.experimental.pallas import tpu as pltpu

BF16 = jnp.bfloat16
F32 = jnp.float32

LANES = 128
V7X_VMEM_LIMIT_BYTES = 56 << 20

DIFF_HEADS = 16
DIFF_HEAD_DIM = 64
MLA_HEADS = 16
MLA_NOPE_DIM = 128
MLA_ROPE_DIM = 64
MLA_V_DIM = 128
ROPE_THETA = 10000.0
NORM_EPS = 1e-6
DIFF_SUBLN_EPS = 1e-5
CONV_WIDTH = 3
MASK_VALUE = -1e30
LOG2_E = math.log2(math.e)


def _cparams(n_axes, flags=None):
    return pltpu.CompilerParams(
        dimension_semantics=("arbitrary",) * n_axes,
        vmem_limit_bytes=V7X_VMEM_LIMIT_BYTES,
        flags=flags,
    )


def _rope_table_kernel(pos_ref, freq_ref, sign_ref, cos_ref, sin_ref):
    ang = pos_ref[...].astype(F32) * freq_ref[...]
    cos_ref[...] = jnp.cos(ang)
    sin_ref[...] = jnp.sin(ang) * sign_ref[...]


def _rope_tables(pos_col, tm=512):
    S = pos_col.shape[0]
    half = MLA_ROPE_DIM // 2
    inv_freq = ROPE_THETA ** (-jnp.arange(0, MLA_ROPE_DIM, 2, dtype=F32) / MLA_ROPE_DIM)
    freq = jnp.tile(inv_freq, LANES // half).reshape(1, LANES)
    sign = jnp.tile(jnp.concatenate([-jnp.ones((half,), F32), jnp.ones((half,), F32)]),
                    LANES // (2 * half)).reshape(1, LANES)
    return pl.pallas_call(
        _rope_table_kernel,
        grid=(S // tm,),
        in_specs=[pl.BlockSpec((tm, 1), lambda i: (i, 0)),
                  pl.BlockSpec((1, LANES), lambda i: (0, 0)),
                  pl.BlockSpec((1, LANES), lambda i: (0, 0))],
        out_specs=[pl.BlockSpec((tm, LANES), lambda i: (i, 0)),
                   pl.BlockSpec((tm, LANES), lambda i: (i, 0))],
        out_shape=[jax.ShapeDtypeStruct((S, LANES), F32)] * 2,
        compiler_params=_cparams(1),
        name="rope_tables",
    )(pos_col, freq, sign)


def _rope_partner(zc):
    lane = lax.broadcasted_iota(jnp.int32, zc.shape, 1)
    first_half = (lane & 32) == 0
    return jnp.where(first_half, pltpu.roll(zc, 96, 1), pltpu.roll(zc, 32, 1))


def _rope_lanes(z, cos, sin):
    outs = []
    for c in range(z.shape[1] // LANES):
        zc = z[:, c * LANES:(c + 1) * LANES]
        outs.append(zc * cos + _rope_partner(zc) * sin)
    return outs[0] if len(outs) == 1 else jnp.concatenate(outs, axis=1)


def _rmsnorm_kernel(x_ref, w_ref, o_ref, *, eps):
    xf = x_ref[...].astype(F32)
    ms = jnp.mean(xf * xf, axis=1, keepdims=True)
    o_ref[...] = (xf * lax.rsqrt(ms + eps) * w_ref[...]).astype(o_ref.dtype)


def _rmsnorm(x, w, out_dtype, eps=NORM_EPS, tm=256):
    M, D = x.shape
    return pl.pallas_call(
        functools.partial(_rmsnorm_kernel, eps=eps),
        grid=(M // tm,),
        in_specs=[pl.BlockSpec((tm, D), lambda i: (i, 0)),
                  pl.BlockSpec((1, D), lambda i: (0, 0))],
        out_specs=pl.BlockSpec((tm, D), lambda i: (i, 0)),
        out_shape=jax.ShapeDtypeStruct((M, D), out_dtype),
        compiler_params=_cparams(1),
        name="rmsnorm",
    )(x, w.reshape(1, D).astype(F32))


def _tile_walk(ni, n_steps):
    def cur(s):
        c = jnp.minimum(s, n_steps - 1)
        return c % ni, c // ni

    def prev(s):
        p = jnp.maximum(s - 1, 0)
        return p % ni, p // ni

    return cur, prev


MM_SUB_ROWS = 256
MXU_DIM = 256


def _mm_kernel(*refs, n_pairs, has_gate, has_norm, has_addend, has_rope, epilogue, epi_arg, eps, ni, n_steps,
               tm, sub, w_transposed):
    refs = list(refs)
    pair_refs = []
    for _ in range(n_pairs):
        a_ref = refs.pop(0)
        w_ref = refs.pop(0)
        g_ref = refs.pop(0) if has_gate else None
        pair_refs.append((a_ref, w_ref, g_ref))
    nw_ref = refs.pop(0) if has_norm else None
    add_ref = refs.pop(0) if has_addend else None
    cos_ref = refs.pop(0) if has_rope else None
    sin_ref = refs.pop(0) if has_rope else None
    o_ref = refs.pop(0)
    wbf_refs = refs[:n_pairs]
    raw_refs = refs[n_pairs:]

    s = pl.program_id(0)
    cur, prev = _tile_walk(ni, n_steps)
    i_cur, _ = cur(s)
    _, j = prev(s)

    @pl.when(s == 0)
    def _():
        for raw in raw_refs:
            raw[...] = jnp.zeros(raw.shape, F32)

    @pl.when(i_cur == 0)
    def _():
        for (_, w_ref, _), wbf in zip(pair_refs, wbf_refs):
            if w_transposed:
                eye = _eye_bf16(MXU_DIM)
                for kc in range(wbf.shape[0] // MXU_DIM):
                    ks = slice(kc * MXU_DIM, (kc + 1) * MXU_DIM)
                    wbf[ks, :] = _nt_dot(eye, w_ref[:, ks].astype(BF16)).astype(BF16)
            else:
                wbf[...] = w_ref[...].astype(BF16)

    def finish_previous(rows, roped):
        if has_gate:
            acc = None
            for (_, _, g_ref), raw in zip(pair_refs, raw_refs):
                d = raw[rows, :] * g_ref[rows, :].astype(F32)
                acc = d if acc is None else acc + d
        else:
            acc = raw_refs[0][rows, :]
        if has_addend:
            acc = acc + add_ref[rows, :]
        if epilogue == "sigmoid":
            out = 1.0 / (1.0 + jnp.exp(-acc))
        elif epilogue == "rope_lt":
            _, n_scaled, scale = epi_arg
            out = acc
            if roped:
                out = _rope_lanes(acc, cos_ref[rows, :], sin_ref[rows, :]) * jnp.where(j < n_scaled, scale, 1.0)
        elif epilogue == "scale_rope_ge":
            out = acc * epi_arg[0]
            if roped:
                out = _rope_lanes(out, cos_ref[rows, :], sin_ref[rows, :])
        elif epilogue == "krope_dup":
            lane = lax.broadcasted_iota(jnp.int32, acc.shape, 1)
            kr = jnp.where(lane < MLA_ROPE_DIM, acc, 0.0)
            r = kr * cos_ref[rows, :] + _rope_partner(kr) * sin_ref[rows, :]
            out = r + pltpu.roll(r, MLA_ROPE_DIM, 1)
        else:
            out = acc
        o_ref[rows, :] = out.astype(o_ref.dtype)

    def multiply_current(rows):
        acc = None
        for p, ((a_ref, _, _), wbf) in enumerate(zip(pair_refs, wbf_refs)):
            a = a_ref[rows, :]
            if has_norm:
                af = a.astype(F32)
                ms = jnp.mean(af * af, axis=1, keepdims=True)
                a = (af * lax.rsqrt(ms + eps) * nw_ref[...]).astype(BF16)
            d = jnp.dot(a, wbf[...], preferred_element_type=F32)
            if has_gate:
                raw_refs[p][rows, :] = d
            else:
                acc = d if acc is None else acc + d
        if not has_gate:
            raw_refs[0][rows, :] = acc

    def step(roped):
        for r in range(tm // sub):
            rows = slice(r * sub, (r + 1) * sub)
            finish_previous(rows, roped)
            multiply_current(rows)

    if epilogue == "rope_lt":
        pl.when(j < epi_arg[0])(lambda: step(True))
        pl.when(j >= epi_arg[0])(lambda: step(False))
    elif epilogue == "scale_rope_ge":
        pl.when(j >= epi_arg[1])(lambda: step(True))
        pl.when(j < epi_arg[1])(lambda: step(False))
    else:
        step(False)


def _mm(pairs, *, N, tm, tn, out_dtype, name, epilogue="none", epi_arg=None,
        norm_w=None, addend=None, rope=None, eps=NORM_EPS, w_transposed=False):
    M = pairs[0]["a"].shape[0]
    ni, nj = M // tm, N // tn
    n_steps = ni * nj
    cur, prev = _tile_walk(ni, n_steps)
    has_gate = pairs[0].get("gate") is not None
    args, in_specs, wbf_scratch = [], [], []

    def at_cur(fn):
        return lambda s: fn(*cur(s))

    def at_prev(fn):
        return lambda s: fn(*prev(s))

    for p in pairs:
        K = p["K"]
        args.append(p["a"])
        in_specs.append(pl.BlockSpec((tm, K), at_cur(functools.partial(lambda i, j, b: (i, b), b=p["a_blk"]))))
        args.append(p["w"])
        if not w_transposed:
            in_specs.append(pl.BlockSpec((K, tn), at_cur(functools.partial(
                lambda i, j, r, c: (r, c + j), r=p["w_row_blk"], c=p["w_col_blk"]))))
        elif "w_elem_off" in p:
            in_specs.append(pl.BlockSpec((pl.Element(tn), pl.Element(K)), at_cur(functools.partial(
                lambda i, j, r, off, k: (pl.multiple_of(off + tn * j, math.gcd(off, tn)), r * k),
                r=p["w_row_blk"], off=p["w_elem_off"], k=K))))
        else:
            in_specs.append(pl.BlockSpec((tn, K), at_cur(functools.partial(
                lambda i, j, r, c: (c + j, r), r=p["w_row_blk"], c=p["w_col_blk"]))))
        if has_gate:
            g, g_off = p["gate"]
            args.append(g)
            in_specs.append(pl.BlockSpec((tm, tn), at_prev(functools.partial(lambda i, j, c: (i, c + j), c=g_off))))
        wbf_scratch.append(pltpu.VMEM((K, tn), BF16))
    if norm_w is not None:
        args.append(norm_w.reshape(1, -1).astype(F32))
        in_specs.append(pl.BlockSpec((1, norm_w.shape[-1]), lambda s: (0, 0)))
    if addend is not None:
        args.append(addend)
        in_specs.append(pl.BlockSpec((tm, tn), at_prev(lambda i, j: (i, j))))
    if rope is not None:
        for t in rope:
            args.append(t)
            in_specs.append(pl.BlockSpec((tm, LANES), at_prev(lambda i, j: (i, 0))))
    raw_scratch = [pltpu.VMEM((tm, tn), F32)] * (len(pairs) if has_gate else 1)
    kern = functools.partial(
        _mm_kernel, n_pairs=len(pairs), has_gate=has_gate, has_norm=norm_w is not None,
        has_addend=addend is not None, has_rope=rope is not None, epilogue=epilogue, epi_arg=epi_arg, eps=eps,
        ni=ni, n_steps=n_steps, tm=tm, sub=min(MM_SUB_ROWS, tm), w_transposed=w_transposed)
    return pl.pallas_call(
        kern,
        grid=(n_steps + 1,),
        in_specs=in_specs,
        out_specs=pl.BlockSpec((tm, tn), at_prev(lambda i, j: (i, j))),
        out_shape=jax.ShapeDtypeStruct((M, N), out_dtype),
        scratch_shapes=wbf_scratch + raw_scratch,
        compiler_params=_cparams(1),
        name=name,
    )(*args)


CONV_HALO = 8


def _ffn_up_kernel(a_ref, wg_ref, wv_ref, cwg_ref, cwv_ref, cbg_ref, cbv_ref, o_ref,
                   wgbf, wvbf, ug_buf, uv_buf, *, tm, ni, n_steps):
    s = pl.program_id(0)
    cur, _ = _tile_walk(ni, n_steps)
    i_cur, _ = cur(s)

    @pl.when(s == 0)
    def _():
        ug_buf[...] = jnp.zeros(ug_buf.shape, F32)
        uv_buf[...] = jnp.zeros(uv_buf.shape, F32)

    @pl.when(i_cur == 0)
    def _():
        wgbf[...] = wg_ref[...].astype(BF16)
        wvbf[...] = wv_ref[...].astype(BF16)

    sub = min(MM_SUB_ROWS, tm)

    def conv(buf, r0, cw_ref, cb_ref):
        lo = CONV_HALO + r0
        out = cb_ref[...] + cw_ref[0:1, :] * buf[lo - 2:lo - 2 + sub, :]
        out = out + cw_ref[1:2, :] * buf[lo - 1:lo - 1 + sub, :]
        return out + cw_ref[2:3, :] * buf[lo:lo + sub, :]

    keep = jnp.where(i_cur == 0, 0.0, 1.0)
    halo_g = ug_buf[tm:tm + CONV_HALO, :] * keep
    halo_v = uv_buf[tm:tm + CONV_HALO, :] * keep

    for r in reversed(range(tm // sub)):
        r0 = r * sub
        g = conv(ug_buf, r0, cwg_ref, cbg_ref)
        v = conv(uv_buf, r0, cwv_ref, cbv_ref)
        o_ref[r0:r0 + sub, :] = (g / (1.0 + jnp.exp(-g)) * v).astype(o_ref.dtype)
        a = a_ref[r0:r0 + sub, :]
        ug_buf[CONV_HALO + r0:CONV_HALO + r0 + sub, :] = jnp.dot(a, wgbf[...], preferred_element_type=F32)
        uv_buf[CONV_HALO + r0:CONV_HALO + r0 + sub, :] = jnp.dot(a, wvbf[...], preferred_element_type=F32)

    ug_buf[0:CONV_HALO, :] = halo_g
    uv_buf[0:CONV_HALO, :] = halo_v


def _ffn_up(h, w_up, conv_w, conv_b, *, tm, tn):
    M, K = h.shape
    d_ff = w_up.shape[1] // 2
    ni, nj = M // tm, d_ff // tn
    n_steps = ni * nj
    cur, prev = _tile_walk(ni, n_steps)
    cb = conv_b.reshape(1, -1)

    def at_cur(fn):
        return lambda s: fn(*cur(s))

    def at_prev(fn):
        return lambda s: fn(*prev(s))

    return pl.pallas_call(
        functools.partial(_ffn_up_kernel, tm=tm, ni=ni, n_steps=n_steps),
        grid=(n_steps + 1,),
        in_specs=[
            pl.BlockSpec((tm, K), at_cur(lambda i, j: (i, 0))),
            pl.BlockSpec((K, tn), at_cur(lambda i, j: (0, j))),
            pl.BlockSpec((K, tn), at_cur(lambda i, j: (0, nj + j))),
            pl.BlockSpec((CONV_WIDTH, tn), at_prev(lambda i, j: (0, j))),
            pl.BlockSpec((CONV_WIDTH, tn), at_prev(lambda i, j: (0, nj + j))),
            pl.BlockSpec((1, tn), at_prev(lambda i, j: (0, j))),
            pl.BlockSpec((1, tn), at_prev(lambda i, j: (0, nj + j))),
        ],
        out_specs=pl.BlockSpec((tm, tn), at_prev(lambda i, j: (i, j))),
        out_shape=jax.ShapeDtypeStruct((M, d_ff), BF16),
        scratch_shapes=[pltpu.VMEM((K, tn), BF16), pltpu.VMEM((K, tn), BF16),
                        pltpu.VMEM((CONV_HALO + tm, tn), F32), pltpu.VMEM((CONV_HALO + tm, tn), F32)],
        compiler_params=_cparams(1),
        name="ffn_up_conv_gate",
    )(h, w_up, w_up, conv_w, conv_w, cb, cb)


ATTN_CHUNK = 256


def _nt_dot(a, b):
    return lax.dot_general(a, b, (((1,), (1,)), ((), ())), preferred_element_type=F32)


def _eye_bf16(n):
    r = lax.broadcasted_iota(jnp.int32, (n, n), 0)
    c = lax.broadcasted_iota(jnp.int32, (n, n), 1)
    return jnp.where(r == c, 1.0, 0.0).astype(BF16)


def _transpose_bf16(x, eye):
    return _nt_dot(eye, x).astype(BF16)


SUBLANES = 8
REDUCE_WAYS = 8


def _reduce_rows(x, op, final):
    n = x.shape[0]
    groups = [x[r * SUBLANES:(r + 1) * SUBLANES] for r in range(n // SUBLANES)]
    ways = min(REDUCE_WAYS, len(groups))
    parts = groups[:ways]
    for g, blk in enumerate(groups[ways:]):
        parts[g % ways] = op(parts[g % ways], blk)
    while len(parts) > 1:
        parts = [op(parts[i], parts[i + 1]) if i + 1 < len(parts) else parts[i] for i in range(0, len(parts), 2)]
    return final(parts[0], axis=0, keepdims=True)


def _softmax_chunk(load_s, m_ref, cols, mask_q0):
    def scores():
        sT = load_s()
        if mask_q0 is not None:
            key = lax.broadcasted_iota(jnp.int32, sT.shape, 0)
            qq = lax.broadcasted_iota(jnp.int32, sT.shape, 1) + mask_q0
            sT = jnp.where(qq >= key, sT, MASK_VALUE)
        return sT

    m_prev = m_ref[:, cols]
    m_new = jnp.maximum(m_prev, _reduce_rows(scores(), jnp.maximum, jnp.max))
    p = jnp.exp2(scores() - m_new)
    alpha = jnp.exp2(m_prev - m_new)
    m_ref[:, cols] = m_new
    return p.astype(BF16), alpha


ONES_ROWS = 16


def _store_vT(vT_ref, v_ref, eye, T):
    dv = v_ref.shape[1]
    for jb in range(vT_ref.shape[0]):
        vT_ref[jb, 0:dv, :] = _transpose_bf16(v_ref[jb * T:(jb + 1) * T, :], eye)
        vT_ref[jb, dv:dv + ONES_ROWS, :] = jnp.ones((ONES_ROWS, T), BF16)


def _causal_attn_loop(qi, k_ref, vT_ref, qT_ref, s_ref, p_ref, a_ref, m_ref, acc_ref, *, T, n_maps):
    CW = min(ATTN_CHUNK, T)
    per_map = T // CW
    chunks = [(slice(c * CW, (c + 1) * CW), (c % per_map) * CW) for c in range(n_maps * per_map)]
    m_ref[...] = jnp.full(m_ref.shape, MASK_VALUE, F32)
    acc_ref[...] = jnp.zeros(acc_ref.shape, F32)

    def qk(blk, slot, cols):
        k = k_ref[pl.ds(pl.multiple_of(blk * T, T), T), :]
        s_ref[slot, :, cols] = jnp.dot(k, qT_ref[:, cols], preferred_element_type=F32)

    def softmax(slot, cols, q0, diagonal):
        n = q0 + CW if diagonal else T
        p, alpha = _softmax_chunk(lambda: s_ref[slot, 0:n, cols], m_ref, cols, q0 if diagonal else None)
        p_ref[slot, 0:n, cols] = p
        a_ref[slot, :, cols] = alpha

    def pv(blk, slot, cols, q0, diagonal):
        n = q0 + CW if diagonal else T
        acc_ref[:, cols] = acc_ref[:, cols] * a_ref[slot, :, cols] + jnp.dot(
            vT_ref[blk, :, 0:n], p_ref[slot, 0:n, cols], preferred_element_type=F32)

    def step(t, slot):
        for cols, q0 in chunks:
            pv(t - 2, slot, cols, q0, False)
            qk(t, slot, cols)
            softmax(1 - slot, cols, q0, False)

    def drain(slot):
        for cols, q0 in chunks:
            softmax(slot, cols, q0, True)
        for cols, q0 in chunks:
            pv(qi, slot, cols, q0, True)

    for cols, _ in chunks:
        qk(0, 0, cols)

    @pl.when(qi == 0)
    def _():
        drain(0)

    @pl.when(qi >= 1)
    def _():
        for cols, q0 in chunks:
            qk(1, 1, cols)
            softmax(0, cols, q0, False)

    def body(u, carry):
        t = 2 + 2 * u
        step(t, 0)
        step(t + 1, 1)
        return carry

    lax.fori_loop(0, lax.shift_right_arithmetic(qi - 1, 1), body, 0)

    @pl.when(jnp.logical_and(qi >= 2, qi % 2 == 0))
    def _():
        step(qi, 0)
        for cols, q0 in chunks:
            pv(qi - 1, 1, cols, q0, False)
        drain(0)

    @pl.when(qi % 2 == 1)
    def _():
        for cols, q0 in chunks:
            pv(qi - 1, 0, cols, q0, False)
        drain(1)


def _diff_attn_kernel(q_ref, k_ref, v_ref, lq1_ref, lk1_ref, lq2_ref, lk2_ref, sw_ref, o_ref,
                      vT_ref, qT_ref, s_ref, p_ref, a_ref, m_ref, acc_ref, *, T, lam_init):
    qi = pl.program_id(1)
    eye = _eye_bf16(LANES)
    dv = v_ref.shape[1]

    @pl.when(qi == 0)
    def _():
        _store_vT(vT_ref, v_ref, eye, T)

    q = q_ref[...]
    lane = lax.broadcasted_iota(jnp.int32, q.shape, 1)
    zero = jnp.zeros_like(q)
    qT_ref[:, 0:T] = _transpose_bf16(jnp.where(lane < DIFF_HEAD_DIM, q, zero), eye)
    qT_ref[:, T:2 * T] = _transpose_bf16(jnp.where(lane >= DIFF_HEAD_DIM, q, zero), eye)

    _causal_attn_loop(qi, k_ref, vT_ref, qT_ref, s_ref, p_ref, a_ref, m_ref, acc_ref, T=T, n_maps=2)

    oT = acc_ref[0:dv, :] / acc_ref[dv:dv + 1, :]
    lam = (jnp.exp(jnp.sum(lq1_ref[...] * lk1_ref[...], axis=1, keepdims=True))
           - jnp.exp(jnp.sum(lq2_ref[...] * lk2_ref[...], axis=1, keepdims=True)) + lam_init)
    odT = oT[:, 0:T] - lam * oT[:, T:2 * T]
    ms = jnp.mean(odT * odT, axis=0, keepdims=True)
    outT = (odT * lax.rsqrt(ms + DIFF_SUBLN_EPS) * sw_ref[...] * (1.0 - lam_init)).astype(BF16)
    for r in range(T // LANES):
        o_ref[r * LANES:(r + 1) * LANES, :] = _transpose_bf16(
            outT[:, r * LANES:(r + 1) * LANES], eye).astype(o_ref.dtype)


def _diff_attn(zqkv, lq1, lk1, lq2, lk2, subln_w, *, lam_init, T):
    S = zqkv.shape[0]
    H = DIFF_HEADS
    hd = 2 * DIFF_HEAD_DIM
    vec = lambda a: a.reshape(1, -1).astype(F32)
    small = lambda n: pl.BlockSpec((1, n), lambda h, qi: (0, 0))
    return pl.pallas_call(
        functools.partial(_diff_attn_kernel, T=T, lam_init=lam_init),
        grid=(H, S // T),
        in_specs=[
            pl.BlockSpec((T, hd), lambda h, qi: (qi, h)),
            pl.BlockSpec((S, hd), lambda h, qi: (0, H + h)),
            pl.BlockSpec((S, hd), lambda h, qi: (0, 2 * H + h)),
            small(DIFF_HEAD_DIM), small(DIFF_HEAD_DIM), small(DIFF_HEAD_DIM), small(DIFF_HEAD_DIM),
            pl.BlockSpec((hd, 1), lambda h, qi: (0, 0)),
        ],
        out_specs=pl.BlockSpec((T, hd), lambda h, qi: (qi, h)),
        out_shape=jax.ShapeDtypeStruct((S, H * hd), BF16),
        scratch_shapes=[pltpu.VMEM((S // T, hd + ONES_ROWS, T), BF16), pltpu.VMEM((hd, 2 * T), BF16),
                        pltpu.VMEM((2, T, 2 * T), F32), pltpu.VMEM((2, T, 2 * T), BF16),
                        pltpu.VMEM((2, 1, 2 * T), F32), pltpu.VMEM((1, 2 * T), F32),
                        pltpu.VMEM((hd + ONES_ROWS, 2 * T), F32)],
        compiler_params=_cparams(2),
        name="diff_attention",
    )(zqkv, zqkv, zqkv, vec(lq1), vec(lk1), vec(lq2), vec(lk2), subln_w.reshape(-1, 1).astype(F32))


def _mla_attn_kernel(qn_ref, qr_ref, kn_ref, kr_ref, v_ref, o_ref, kcat, vT_ref, qT_ref, s_ref, p_ref, a_ref,
                     m_ref, acc_ref, *, T):
    h = pl.program_id(0)
    qi = pl.program_id(1)
    eye = _eye_bf16(LANES)
    dv = v_ref.shape[1]

    @pl.when(qi == 0)
    def _():
        kcat[:, 0:MLA_NOPE_DIM] = kn_ref[...]
        kcat[:, MLA_NOPE_DIM:] = kr_ref[...]
        _store_vT(vT_ref, v_ref, eye, T)

    qr = qr_ref[...]
    lane = lax.broadcasted_iota(jnp.int32, qr.shape, 1)
    lo = (h % 2) * MLA_ROPE_DIM
    mine = jnp.logical_and(lane >= lo, lane < lo + MLA_ROPE_DIM)
    qT_ref[0:MLA_NOPE_DIM, :] = _transpose_bf16(qn_ref[...], eye)
    qT_ref[MLA_NOPE_DIM:, :] = _transpose_bf16(jnp.where(mine, qr, jnp.zeros_like(qr)), eye)

    _causal_attn_loop(qi, kcat, vT_ref, qT_ref, s_ref, p_ref, a_ref, m_ref, acc_ref, T=T, n_maps=1)

    oT = (acc_ref[0:dv, :] / acc_ref[dv:dv + 1, :]).astype(BF16)
    for r in range(T // LANES):
        o_ref[r * LANES:(r + 1) * LANES, :] = _transpose_bf16(
            oT[:, r * LANES:(r + 1) * LANES], eye).astype(o_ref.dtype)


def _mla_attn(qm, kv, kr_dup, *, T):
    S = qm.shape[0]
    H = MLA_HEADS
    return pl.pallas_call(
        functools.partial(_mla_attn_kernel, T=T),
        grid=(H, S // T),
        in_specs=[
            pl.BlockSpec((T, MLA_NOPE_DIM), lambda h, qi: (qi, h)),
            pl.BlockSpec((T, LANES), lambda h, qi: (qi, H + h // 2)),
            pl.BlockSpec((S, MLA_NOPE_DIM), lambda h, qi: (0, 2 * h)),
            pl.BlockSpec((S, LANES), lambda h, qi: (0, 0)),
            pl.BlockSpec((S, MLA_V_DIM), lambda h, qi: (0, 2 * h + 1)),
        ],
        out_specs=pl.BlockSpec((T, MLA_V_DIM), lambda h, qi: (qi, h)),
        out_shape=jax.ShapeDtypeStruct((S, H * MLA_V_DIM), BF16),
        scratch_shapes=[pltpu.VMEM((S, MLA_NOPE_DIM + LANES), BF16),
                        pltpu.VMEM((S // T, MLA_V_DIM + ONES_ROWS, T), BF16),
                        pltpu.VMEM((MLA_NOPE_DIM + LANES, T), BF16),
                        pltpu.VMEM((2, T, T), F32), pltpu.VMEM((2, T, T), BF16), pltpu.VMEM((2, 1, T), F32),
                        pltpu.VMEM((1, T), F32), pltpu.VMEM((MLA_V_DIM + ONES_ROWS, T), F32)],
        compiler_params=_cparams(2),
        name="mla_attention",
    )(qm, qm, kv, kr_dup, kv)


def _block_forward(x2d, pos_col, l, norm_mix_w, w_in, lq1, lk1, lq2, lk2, subln_w, q_norm_w, w_uq,
                   kv_norm_w, w_ukv, w_o_diff, w_o_mla, w_out, norm_ffn_w, w_up, conv_w, conv_b, w_down,
                   *, tm=1024, tn=512, t_attn=512, tn_ffn=256, tm_down=512, tm_wide=512, tn_wide=1024):
    S, D = x2d.shape
    H = DIFF_HEADS
    qkv_w = 3 * H * 2 * DIFF_HEAD_DIM
    q_rank = w_uq.shape[0]
    kv_rank = w_ukv.shape[0]
    lat_w = q_rank + kv_rank
    main_w = qkv_w + lat_w
    gate_start = main_w + MLA_ROPE_DIM
    lam_init = 0.8 - 0.6 * math.exp(-0.3 * l)

    cos, sin = _rope_tables(pos_col)
    h = _rmsnorm(x2d, norm_mix_w, BF16)

    w_in_t = jnp.swapaxes(w_in, 0, 1)
    z = _mm([dict(a=h, a_blk=0, K=D, w=w_in_t, w_row_blk=0, w_col_blk=0)], N=main_w, tm=tm, tn=tn,
            out_dtype=BF16, name="in_proj_main", epilogue="rope_lt",
            epi_arg=((2 * H * 2 * DIFF_HEAD_DIM) // tn, (H * 2 * DIFF_HEAD_DIM) // tn,
                     DIFF_HEAD_DIM ** -0.5 * LOG2_E), rope=(cos, sin), w_transposed=True)
    kr_dup = _mm([dict(a=h, a_blk=0, K=D, w=w_in_t, w_row_blk=0, w_col_blk=main_w // LANES)], N=LANES,
                 tm=tm, tn=LANES, out_dtype=BF16, name="in_proj_krope", epilogue="krope_dup",
                 rope=(cos, sin), w_transposed=True)
    gates = _mm([dict(a=h, a_blk=0, K=D, w=w_in_t, w_row_blk=0, w_elem_off=gate_start)], N=2 * D,
                tm=tm_wide, tn=tn_wide, out_dtype=BF16, name="in_proj_gates", epilogue="sigmoid",
                w_transposed=True)

    o_d = _diff_attn(z, lq1, lk1, lq2, lk2, subln_w, lam_init=lam_init, T=t_attn)

    qk_dim = MLA_NOPE_DIM + MLA_ROPE_DIM
    w_uq3 = w_uq.reshape(q_rank, MLA_HEADS, qk_dim)
    w_uq_perm = jnp.concatenate([w_uq3[:, :, :MLA_NOPE_DIM].reshape(q_rank, -1),
                                 w_uq3[:, :, MLA_NOPE_DIM:].reshape(q_rank, -1)], axis=1)
    qm = _mm([dict(a=z, a_blk=qkv_w // q_rank, K=q_rank, w=w_uq_perm, w_row_blk=0, w_col_blk=0)],
             N=MLA_HEADS * qk_dim, tm=tm, tn=tn, out_dtype=BF16, name="mla_q_up", norm_w=q_norm_w,
             epilogue="scale_rope_ge", epi_arg=(qk_dim ** -0.5 * LOG2_E, (MLA_HEADS * MLA_NOPE_DIM) // tn),
             rope=(cos, sin))
    kv = _mm([dict(a=z, a_blk=(qkv_w + q_rank) // kv_rank, K=kv_rank, w=w_ukv, w_row_blk=0, w_col_blk=0)],
             N=w_ukv.shape[1], tm=tm, tn=tn, out_dtype=BF16, name="mla_kv_up", norm_w=kv_norm_w)
    o_m = _mla_attn(qm, kv, kr_dup, T=t_attn)

    y = _mm([dict(a=o_d, a_blk=0, K=o_d.shape[1], w=w_o_diff, w_row_blk=0, w_col_blk=0, gate=(gates, 0)),
             dict(a=o_m, a_blk=0, K=o_m.shape[1], w=w_o_mla, w_row_blk=0, w_col_blk=0, gate=(gates, D // tn))],
            N=D, tm=tm, tn=tn, out_dtype=BF16, name="branch_merge")
    x1 = _mm([dict(a=y, a_blk=0, K=D, w=w_out, w_row_blk=0, w_col_blk=0)], N=D, tm=tm, tn=tn,
             out_dtype=F32, name="out_proj", addend=x2d)

    h2 = _rmsnorm(x1, norm_ffn_w, BF16)
    act = _ffn_up(h2, w_up, conv_w, conv_b, tm=tm, tn=tn_ffn)
    d_ff = act.shape[1]
    k_half = d_ff // 2
    p0 = _mm([dict(a=act, a_blk=0, K=k_half, w=w_down, w_row_blk=0, w_col_blk=0)], N=D, tm=tm_down, tn=tn,
             out_dtype=F32, name="ffn_down_lo", addend=x1)
    x2 = _mm([dict(a=act, a_blk=1, K=k_half, w=w_down, w_row_blk=1, w_col_blk=0)], N=D, tm=tm_down, tn=tn,
             out_dtype=F32, name="ffn_down_hi", addend=p0)
    return x2


def kernel(x, positions, norm_mix_w, w_in, diff_lambda_q1, diff_lambda_k1, diff_lambda_q2, diff_lambda_k2, diff_subln_w, mla_q_norm_w, mla_w_uq, mla_kv_norm_w, mla_w_ukv, w_o_diff, w_o_mla, w_out, norm_ffn_w, ffn_w_up, ffn_conv_w, ffn_conv_b, ffn_w_down, final_norm_w):
    B, S, D = x.shape
    assert B == 1
    x2d = x.reshape(S, D)
    pos_col = positions.reshape(S, 1)
    for l in range(w_in.shape[0]):
        x2d = _block_forward(
            x2d, pos_col, l, norm_mix_w[l], w_in[l], diff_lambda_q1[l], diff_lambda_k1[l], diff_lambda_q2[l],
            diff_lambda_k2[l], diff_subln_w[l], mla_q_norm_w[l], mla_w_uq[l], mla_kv_norm_w[l], mla_w_ukv[l],
            w_o_diff[l], w_o_mla[l], w_out[l], norm_ffn_w[l], ffn_w_up[l], ffn_conv_w[l], ffn_conv_b[l],
            ffn_w_down[l])
    out = _rmsnorm(x2d, final_norm_w, F32)
    return out.reshape(B, S, D)
```

```python
import functools
import math

import jax
import jax.numpy as jnp
from jax import lax
from jax.experimental import pallas as pl
from jax.experimental.pallas import tpu as pltpu

BF16 = jnp.bfloat16
F32 = jnp.float32

LANES = 128
V7X_VMEM_LIMIT_BYTES = 56 << 20

DIFF_HEADS = 16
DIFF_HEAD_DIM = 64
MLA_HEADS = 16
MLA_NOPE_DIM = 128
MLA_ROPE_DIM = 64
MLA_V_DIM = 128
ROPE_THETA = 10000.0
NORM_EPS = 1e-6
DIFF_SUBLN_EPS = 1e-5
CONV_WIDTH = 3
MASK_VALUE = -1e30
LOG2_E = math.log2(math.e)


def _cparams(n_axes, flags=None):
    return pltpu.CompilerParams(
        dimension_semantics=("arbitrary",) * n_axes,
        vmem_limit_bytes=V7X_VMEM_LIMIT_BYTES,
        flags=flags,
    )


def _rope_table_kernel(pos_ref, freq_ref, sign_ref, cos_ref, sin_ref):
    ang = pos_ref[...].astype(F32) * freq_ref[...]
    cos_ref[...] = jnp.cos(ang)
    sin_ref[...] = jnp.sin(ang) * sign_ref[...]


def _rope_tables(pos_col, tm=512):
    S = pos_col.shape[0]
    half = MLA_ROPE_DIM // 2
    inv_freq = ROPE_THETA ** (-jnp.arange(0, MLA_ROPE_DIM, 2, dtype=F32) / MLA_ROPE_DIM)
    freq = jnp.tile(inv_freq, LANES // half).reshape(1, LANES)
    sign = jnp.tile(jnp.concatenate([-jnp.ones((half,), F32), jnp.ones((half,), F32)]),
                    LANES // (2 * half)).reshape(1, LANES)
    return pl.pallas_call(
        _rope_table_kernel,
        grid=(S // tm,),
        in_specs=[pl.BlockSpec((tm, 1), lambda i: (i, 0)),
                  pl.BlockSpec((1, LANES), lambda i: (0, 0)),
                  pl.BlockSpec((1, LANES), lambda i: (0, 0))],
        out_specs=[pl.BlockSpec((tm, LANES), lambda i: (i, 0)),
                   pl.BlockSpec((tm, LANES), lambda i: (i, 0))],
        out_shape=[jax.ShapeDtypeStruct((S, LANES), F32)] * 2,
        compiler_params=_cparams(1),
        name="rope_tables",
    )(pos_col, freq, sign)


def _rope_partner(zc):
    lane = lax.broadcasted_iota(jnp.int32, zc.shape, 1)
    first_half = (lane & 32) == 0
    return jnp.where(first_half, pltpu.roll(zc, 96, 1), pltpu.roll(zc, 32, 1))


def _rope_lanes(z, cos, sin):
    outs = []
    for c in range(z.shape[1] // LANES):
        zc = z[:, c * LANES:(c + 1) * LANES]
        outs.append(zc * cos + _rope_partner(zc) * sin)
    return outs[0] if len(outs) == 1 else jnp.concatenate(outs, axis=1)


def _rmsnorm_kernel(x_ref, w_ref, o_ref, *, eps):
    xf = x_ref[...].astype(F32)
    ms = jnp.mean(xf * xf, axis=1, keepdims=True)
    o_ref[...] = (xf * lax.rsqrt(ms + eps) * w_ref[...]).astype(o_ref.dtype)


def _rmsnorm(x, w, out_dtype, eps=NORM_EPS, tm=256):
    M, D = x.shape
    return pl.pallas_call(
        functools.partial(_rmsnorm_kernel, eps=eps),
        grid=(M // tm,),
        in_specs=[pl.BlockSpec((tm, D), lambda i: (i, 0)),
                  pl.BlockSpec((1, D), lambda i: (0, 0))],
        out_specs=pl.BlockSpec((tm, D), lambda i: (i, 0)),
        out_shape=jax.ShapeDtypeStruct((M, D), out_dtype),
        compiler_params=_cparams(1),
        name="rmsnorm",
    )(x, w.reshape(1, D).astype(F32))


def _tile_walk(ni, n_steps):
    def cur(s):
        c = jnp.minimum(s, n_steps - 1)
        return c % ni, c // ni

    def prev(s):
        p = jnp.maximum(s - 1, 0)
        return p % ni, p // ni

    return cur, prev


MM_SUB_ROWS = 256
MXU_DIM = 256


def _mm_kernel(*refs, n_pairs, has_gate, has_norm, has_addend, has_rope, epilogue, epi_arg, eps, ni, n_steps,
               tm, sub, w_transposed):
    refs = list(refs)
    pair_refs = []
    for _ in range(n_pairs):
        a_ref = refs.pop(0)
        w_ref = refs.pop(0)
        g_ref = refs.pop(0) if has_gate else None
        pair_refs.append((a_ref, w_ref, g_ref))
    nw_ref = refs.pop(0) if has_norm else None
    add_ref = refs.pop(0) if has_addend else None
    cos_ref = refs.pop(0) if has_rope else None
    sin_ref = refs.pop(0) if has_rope else None
    o_ref = refs.pop(0)
    wbf_refs = refs[:n_pairs]
    raw_refs = refs[n_pairs:]

    s = pl.program_id(0)
    cur, prev = _tile_walk(ni, n_steps)
    i_cur, _ = cur(s)
    _, j = prev(s)

    @pl.when(s == 0)
    def _():
        for raw in raw_refs:
            raw[...] = jnp.zeros(raw.shape, F32)

    @pl.when(i_cur == 0)
    def _():
        for (_, w_ref, _), wbf in zip(pair_refs, wbf_refs):
            if w_transposed:
                eye = _eye_bf16(MXU_DIM)
                for kc in range(wbf.shape[0] // MXU_DIM):
                    ks = slice(kc * MXU_DIM, (kc + 1) * MXU_DIM)
                    wbf[ks, :] = _nt_dot(eye, w_ref[:, ks].astype(BF16)).astype(BF16)
            else:
                wbf[...] = w_ref[...].astype(BF16)

    def finish_previous(rows, roped):
        if has_gate:
            acc = None
            for (_, _, g_ref), raw in zip(pair_refs, raw_refs):
                d = raw[rows, :] * g_ref[rows, :].astype(F32)
                acc = d if acc is None else acc + d
        else:
            acc = raw_refs[0][rows, :]
        if has_addend:
            acc = acc + add_ref[rows, :]
        if epilogue == "sigmoid":
            out = 1.0 / (1.0 + jnp.exp(-acc))
        elif epilogue == "rope_lt":
            _, n_scaled, scale = epi_arg
            out = acc
            if roped:
                out = _rope_lanes(acc, cos_ref[rows, :], sin_ref[rows, :]) * jnp.where(j < n_scaled, scale, 1.0)
        elif epilogue == "scale_rope_ge":
            out = acc * epi_arg[0]
            if roped:
                out = _rope_lanes(out, cos_ref[rows, :], sin_ref[rows, :])
        elif epilogue == "krope_dup":
            lane = lax.broadcasted_iota(jnp.int32, acc.shape, 1)
            kr = jnp.where(lane < MLA_ROPE_DIM, acc, 0.0)
            r = kr * cos_ref[rows, :] + _rope_partner(kr) * sin_ref[rows, :]
            out = r + pltpu.roll(r, MLA_ROPE_DIM, 1)
        else:
            out = acc
        o_ref[rows, :] = out.astype(o_ref.dtype)

    def multiply_current(rows):
        acc = None
        for p, ((a_ref, _, _), wbf) in enumerate(zip(pair_refs, wbf_refs)):
            a = a_ref[rows, :]
            if has_norm:
                af = a.astype(F32)
                ms = jnp.mean(af * af, axis=1, keepdims=True)
                a = (af * lax.rsqrt(ms + eps) * nw_ref[...]).astype(BF16)
            d = jnp.dot(a, wbf[...], preferred_element_type=F32)
            if has_gate:
                raw_refs[p][rows, :] = d
            else:
                acc = d if acc is None else acc + d
        if not has_gate:
            raw_refs[0][rows, :] = acc

    def step(roped):
        for r in range(tm // sub):
            rows = slice(r * sub, (r + 1) * sub)
            finish_previous(rows, roped)
            multiply_current(rows)

    if epilogue == "rope_lt":
        pl.when(j < epi_arg[0])(lambda: step(True))
        pl.when(j >= epi_arg[0])(lambda: step(False))
    elif epilogue == "scale_rope_ge":
        pl.when(j >= epi_arg[1])(lambda: step(True))
        pl.when(j < epi_arg[1])(lambda: step(False))
    else:
        step(False)


def _mm(pairs, *, N, tm, tn, out_dtype, name, epilogue="none", epi_arg=None,
        norm_w=None, addend=None, rope=None, eps=NORM_EPS, w_transposed=False):
    M = pairs[0]["a"].shape[0]
    ni, nj = M // tm, N // tn
    n_steps = ni * nj
    cur, prev = _tile_walk(ni, n_steps)
    has_gate = pairs[0].get("gate") is not None
    args, in_specs, wbf_scratch = [], [], []

    def at_cur(fn):
        return lambda s: fn(*cur(s))

    def at_prev(fn):
        return lambda s: fn(*prev(s))

    for p in pairs:
        K = p["K"]
        args.append(p["a"])
        in_specs.append(pl.BlockSpec((tm, K), at_cur(functools.partial(lambda i, j, b: (i, b), b=p["a_blk"]))))
        args.append(p["w"])
        if not w_transposed:
            in_specs.append(pl.BlockSpec((K, tn), at_cur(functools.partial(
                lambda i, j, r, c: (r, c + j), r=p["w_row_blk"], c=p["w_col_blk"]))))
        elif "w_elem_off" in p:
            in_specs.append(pl.BlockSpec((pl.Element(tn), pl.Element(K)), at_cur(functools.partial(
                lambda i, j, r, off, k: (pl.multiple_of(off + tn * j, math.gcd(off, tn)), r * k),
                r=p["w_row_blk"], off=p["w_elem_off"], k=K))))
        else:
            in_specs.append(pl.BlockSpec((tn, K), at_cur(functools.partial(
                lambda i, j, r, c: (c + j, r), r=p["w_row_blk"], c=p["w_col_blk"]))))
        if has_gate:
            g, g_off = p["gate"]
            args.append(g)
            in_specs.append(pl.BlockSpec((tm, tn), at_prev(functools.partial(lambda i, j, c: (i, c + j), c=g_off))))
        wbf_scratch.append(pltpu.VMEM((K, tn), BF16))
    if norm_w is not None:
        args.append(norm_w.reshape(1, -1).astype(F32))
        in_specs.append(pl.BlockSpec((1, norm_w.shape[-1]), lambda s: (0, 0)))
    if addend is not None:
        args.append(addend)
        in_specs.append(pl.BlockSpec((tm, tn), at_prev(lambda i, j: (i, j))))
    if rope is not None:
        for t in rope:
            args.append(t)
            in_specs.append(pl.BlockSpec((tm, LANES), at_prev(lambda i, j: (i, 0))))
    raw_scratch = [pltpu.VMEM((tm, tn), F32)] * (len(pairs) if has_gate else 1)
    kern = functools.partial(
        _mm_kernel, n_pairs=len(pairs), has_gate=has_gate, has_norm=norm_w is not None,
        has_addend=addend is not None, has_rope=rope is not None, epilogue=epilogue, epi_arg=epi_arg, eps=eps,
        ni=ni, n_steps=n_steps, tm=tm, sub=min(MM_SUB_ROWS, tm), w_transposed=w_transposed)
    return pl.pallas_call(
        kern,
        grid=(n_steps + 1,),
        in_specs=in_specs,
        out_specs=pl.BlockSpec((tm, tn), at_prev(lambda i, j: (i, j))),
        out_shape=jax.ShapeDtypeStruct((M, N), out_dtype),
        scratch_shapes=wbf_scratch + raw_scratch,
        compiler_params=_cparams(1),
        name=name,
    )(*args)


CONV_HALO = 8


def _ffn_up_kernel(a_ref, wg_ref, wv_ref, cwg_ref, cwv_ref, cbg_ref, cbv_ref, o_ref,
                   wgbf, wvbf, ug_buf, uv_buf, *, tm, ni, n_steps):
    s = pl.program_id(0)
    cur, _ = _tile_walk(ni, n_steps)
    i_cur, _ = cur(s)

    @pl.when(s == 0)
    def _():
        ug_buf[...] = jnp.zeros(ug_buf.shape, F32)
        uv_buf[...] = jnp.zeros(uv_buf.shape, F32)

    @pl.when(i_cur == 0)
    def _():
        wgbf[...] = wg_ref[...].astype(BF16)
        wvbf[...] = wv_ref[...].astype(BF16)

    sub = min(MM_SUB_ROWS, tm)

    def conv(buf, r0, cw_ref, cb_ref):
        lo = CONV_HALO + r0
        out = cb_ref[...] + cw_ref[0:1, :] * buf[lo - 2:lo - 2 + sub, :]
        out = out + cw_ref[1:2, :] * buf[lo - 1:lo - 1 + sub, :]
        return out + cw_ref[2:3, :] * buf[lo:lo + sub, :]

    keep = jnp.where(i_cur == 0, 0.0, 1.0)
    halo_g = ug_buf[tm:tm + CONV_HALO, :] * keep
    halo_v = uv_buf[tm:tm + CONV_HALO, :] * keep

    for r in reversed(range(tm // sub)):
        r0 = r * sub
        g = conv(ug_buf, r0, cwg_ref, cbg_ref)
        v = conv(uv_buf, r0, cwv_ref, cbv_ref)
        o_ref[r0:r0 + sub, :] = (g / (1.0 + jnp.exp(-g)) * v).astype(o_ref.dtype)
        a = a_ref[r0:r0 + sub, :]
        ug_buf[CONV_HALO + r0:CONV_HALO + r0 + sub, :] = jnp.dot(a, wgbf[...], preferred_element_type=F32)
        uv_buf[CONV_HALO + r0:CONV_HALO + r0 + sub, :] = jnp.dot(a, wvbf[...], preferred_element_type=F32)

    ug_buf[0:CONV_HALO, :] = halo_g
    uv_buf[0:CONV_HALO, :] = halo_v


def _ffn_up(h, w_up, conv_w, conv_b, *, tm, tn):
    M, K = h.shape
    d_ff = w_up.shape[1] // 2
    ni, nj = M // tm, d_ff // tn
    n_steps = ni * nj
    cur, prev = _tile_walk(ni, n_steps)
    cb = conv_b.reshape(1, -1)

    def at_cur(fn):
        return lambda s: fn(*cur(s))

    def at_prev(fn):
        return lambda s: fn(*prev(s))

    return pl.pallas_call(
        functools.partial(_ffn_up_kernel, tm=tm, ni=ni, n_steps=n_steps),
        grid=(n_steps + 1,),
        in_specs=[
            pl.BlockSpec((tm, K), at_cur(lambda i, j: (i, 0))),
            pl.BlockSpec((K, tn), at_cur(lambda i, j: (0, j))),
            pl.BlockSpec((K, tn), at_cur(lambda i, j: (0, nj + j))),
            pl.BlockSpec((CONV_WIDTH, tn), at_prev(lambda i, j: (0, j))),
            pl.BlockSpec((CONV_WIDTH, tn), at_prev(lambda i, j: (0, nj + j))),
            pl.BlockSpec((1, tn), at_prev(lambda i, j: (0, j))),
            pl.BlockSpec((1, tn), at_prev(lambda i, j: (0, nj + j))),
        ],
        out_specs=pl.BlockSpec((tm, tn), at_prev(lambda i, j: (i, j))),
        out_shape=jax.ShapeDtypeStruct((M, d_ff), BF16),
        scratch_shapes=[pltpu.VMEM((K, tn), BF16), pltpu.VMEM((K, tn), BF16),
                        pltpu.VMEM((CONV_HALO + tm, tn), F32), pltpu.VMEM((CONV_HALO + tm, tn), F32)],
        compiler_params=_cparams(1),
        name="ffn_up_conv_gate",
    )(h, w_up, w_up, conv_w, conv_w, cb, cb)


ATTN_CHUNK = 256


def _nt_dot(a, b):
    return lax.dot_general(a, b, (((1,), (1,)), ((), ())), preferred_element_type=F32)


def _eye_bf16(n):
    r = lax.broadcasted_iota(jnp.int32, (n, n), 0)
    c = lax.broadcasted_iota(jnp.int32, (n, n), 1)
    return jnp.where(r == c, 1.0, 0.0).astype(BF16)


def _transpose_bf16(x, eye):
    return _nt_dot(eye, x).astype(BF16)


SUBLANES = 8
REDUCE_WAYS = 8


def _reduce_rows(x, op, final):
    n = x.shape[0]
    groups = [x[r * SUBLANES:(r + 1) * SUBLANES] for r in range(n // SUBLANES)]
    ways = min(REDUCE_WAYS, len(groups))
    parts = groups[:ways]
    for g, blk in enumerate(groups[ways:]):
        parts[g % ways] = op(parts[g % ways], blk)
    while len(parts) > 1:
        parts = [op(parts[i], parts[i + 1]) if i + 1 < len(parts) else parts[i] for i in range(0, len(parts), 2)]
    return final(parts[0], axis=0, keepdims=True)


def _softmax_chunk(load_s, m_ref, c, mask_q0):
    def scores():
        sT = load_s()
        if mask_q0 is not None:
            key = lax.broadcasted_iota(jnp.int32, sT.shape, 0)
            qq = lax.broadcasted_iota(jnp.int32, sT.shape, 1) + mask_q0
            sT = jnp.where(qq >= key, sT, MASK_VALUE)
        return sT

    m_prev = m_ref[c]
    m_new = jnp.maximum(m_prev, _reduce_rows(scores(), jnp.maximum, jnp.max))
    p = jnp.exp2(scores() - m_new)
    alpha = jnp.exp2(m_prev - m_new)
    m_ref[c] = m_new
    return p.astype(BF16), alpha


ONES_ROWS = 16


def _store_vT(vT_ref, v_ref, eye, T):
    dv = v_ref.shape[1]
    for jb in range(vT_ref.shape[0]):
        vT_ref[jb, 0:dv, :] = _transpose_bf16(v_ref[jb * T:(jb + 1) * T, :], eye)
        vT_ref[jb, dv:dv + ONES_ROWS, :] = jnp.ones((ONES_ROWS, T), BF16)


def _causal_attn_loop(qi, k_ref, vT_ref, qT_ref, s_ref, p_ref, a_ref, m_ref, acc_ref, *, T, n_maps):
    CW = qT_ref.shape[2]
    per_map = T // CW
    chunks = [(c, (c % per_map) * CW) for c in range(n_maps * per_map)]
    m_ref[...] = jnp.full(m_ref.shape, MASK_VALUE, F32)
    acc_ref[...] = jnp.zeros(acc_ref.shape, F32)

    def qk(blk, slot, c):
        k = k_ref[pl.ds(pl.multiple_of(blk * T, T), T), :]
        s_ref[slot, c] = jnp.dot(k, qT_ref[c], preferred_element_type=F32)

    def softmax(slot, c, q0, diagonal):
        n = q0 + CW if diagonal else T
        p, alpha = _softmax_chunk(lambda: s_ref[slot, c, 0:n, :], m_ref, c, q0 if diagonal else None)
        p_ref[slot, c, 0:n, :] = p
        a_ref[slot, c] = alpha

    def pv(blk, slot, c, q0, diagonal):
        n = q0 + CW if diagonal else T
        acc_ref[c] = acc_ref[c] * a_ref[slot, c] + jnp.dot(
            vT_ref[blk, :, 0:n], p_ref[slot, c, 0:n, :], preferred_element_type=F32)

    def step(t, slot):
        for c, q0 in chunks:
            pv(t - 2, slot, c, q0, False)
            qk(t, slot, c)
            softmax(1 - slot, c, q0, False)

    def drain(slot):
        for c, q0 in chunks:
            softmax(slot, c, q0, True)
        for c, q0 in chunks:
            pv(qi, slot, c, q0, True)

    for c, _ in chunks:
        qk(0, 0, c)

    @pl.when(qi == 0)
    def _():
        drain(0)

    @pl.when(qi >= 1)
    def _():
        for c, q0 in chunks:
            qk(1, 1, c)
            softmax(0, c, q0, False)

    def body(u, carry):
        t = 2 + 2 * u
        step(t, 0)
        step(t + 1, 1)
        return carry

    lax.fori_loop(0, lax.shift_right_arithmetic(qi - 1, 1), body, 0)

    @pl.when(jnp.logical_and(qi >= 2, qi % 2 == 0))
    def _():
        step(qi, 0)
        for c, q0 in chunks:
            pv(qi - 1, 1, c, q0, False)
        drain(0)

    @pl.when(qi % 2 == 1)
    def _():
        for c, q0 in chunks:
            pv(qi - 1, 0, c, q0, False)
        drain(1)


def _diff_attn_kernel(q_ref, k_ref, v_ref, lq1_ref, lk1_ref, lq2_ref, lk2_ref, sw_ref, o_ref,
                      vT_ref, qT_ref, s_ref, p_ref, a_ref, m_ref, acc_ref, *, T, lam_init):
    qi = pl.program_id(1)
    eye = _eye_bf16(LANES)
    dv = v_ref.shape[1]

    @pl.when(qi == 0)
    def _():
        _store_vT(vT_ref, v_ref, eye, T)

    CW = qT_ref.shape[2]
    per_map = T // CW
    q = q_ref[...]
    lane = lax.broadcasted_iota(jnp.int32, q.shape, 1)
    zero = jnp.zeros_like(q)
    for half, keep in enumerate((lane < DIFF_HEAD_DIM, lane >= DIFF_HEAD_DIM)):
        qh = jnp.where(keep, q, zero)
        for c in range(per_map):
            qT_ref[half * per_map + c] = _transpose_bf16(qh[c * CW:(c + 1) * CW, :], eye)

    _causal_attn_loop(qi, k_ref, vT_ref, qT_ref, s_ref, p_ref, a_ref, m_ref, acc_ref, T=T, n_maps=2)

    lam = (jnp.exp(jnp.sum(lq1_ref[...] * lk1_ref[...], axis=1, keepdims=True))
           - jnp.exp(jnp.sum(lq2_ref[...] * lk2_ref[...], axis=1, keepdims=True)) + lam_init)
    for c in range(per_map):
        o1 = acc_ref[c, 0:dv, :] / acc_ref[c, dv:dv + 1, :]
        o2 = acc_ref[per_map + c, 0:dv, :] / acc_ref[per_map + c, dv:dv + 1, :]
        odT = o1 - lam * o2
        ms = jnp.mean(odT * odT, axis=0, keepdims=True)
        outT = (odT * lax.rsqrt(ms + DIFF_SUBLN_EPS) * sw_ref[...] * (1.0 - lam_init)).astype(BF16)
        for r in range(CW // LANES):
            r0 = c * CW + r * LANES
            o_ref[r0:r0 + LANES, :] = _transpose_bf16(outT[:, r * LANES:(r + 1) * LANES], eye).astype(o_ref.dtype)


def _diff_attn(zqkv, lq1, lk1, lq2, lk2, subln_w, *, lam_init, T):
    S = zqkv.shape[0]
    H = DIFF_HEADS
    hd = 2 * DIFF_HEAD_DIM
    cw = min(ATTN_CHUNK, T)
    nc = 2 * (T // cw)
    vec = lambda a: a.reshape(1, -1).astype(F32)
    small = lambda n: pl.BlockSpec((1, n), lambda h, qi: (0, 0))
    return pl.pallas_call(
        functools.partial(_diff_attn_kernel, T=T, lam_init=lam_init),
        grid=(H, S // T),
        in_specs=[
            pl.BlockSpec((T, hd), lambda h, qi: (qi, h)),
            pl.BlockSpec((S, hd), lambda h, qi: (0, H + h)),
            pl.BlockSpec((S, hd), lambda h, qi: (0, 2 * H + h)),
            small(DIFF_HEAD_DIM), small(DIFF_HEAD_DIM), small(DIFF_HEAD_DIM), small(DIFF_HEAD_DIM),
            pl.BlockSpec((hd, 1), lambda h, qi: (0, 0)),
        ],
        out_specs=pl.BlockSpec((T, hd), lambda h, qi: (qi, h)),
        out_shape=jax.ShapeDtypeStruct((S, H * hd), BF16),
        scratch_shapes=[pltpu.VMEM((S // T, hd + ONES_ROWS, T), BF16), pltpu.VMEM((nc, hd, cw), BF16),
                        pltpu.VMEM((2, nc, T, cw), F32), pltpu.VMEM((2, nc, T, cw), BF16),
                        pltpu.VMEM((2, nc, 1, cw), F32), pltpu.VMEM((nc, 1, cw), F32),
                        pltpu.VMEM((nc, hd + ONES_ROWS, cw), F32)],
        compiler_params=_cparams(2),
        name="diff_attention",
    )(zqkv, zqkv, zqkv, vec(lq1), vec(lk1), vec(lq2), vec(lk2), subln_w.reshape(-1, 1).astype(F32))


def _mla_attn_kernel(qn_ref, qr_ref, kn_ref, kr_ref, v_ref, o_ref, kcat, vT_ref, qT_ref, s_ref, p_ref, a_ref,
                     m_ref, acc_ref, *, T):
    h = pl.program_id(0)
    qi = pl.program_id(1)
    eye = _eye_bf16(LANES)
    dv = v_ref.shape[1]

    @pl.when(qi == 0)
    def _():
        kcat[:, 0:MLA_NOPE_DIM] = kn_ref[...]
        kcat[:, MLA_NOPE_DIM:] = kr_ref[...]
        _store_vT(vT_ref, v_ref, eye, T)

    qr = qr_ref[...]
    lane = lax.broadcasted_iota(jnp.int32, qr.shape, 1)
    lo = (h % 2) * MLA_ROPE_DIM
    mine = jnp.logical_and(lane >= lo, lane < lo + MLA_ROPE_DIM)
    CW = qT_ref.shape[2]
    qn = qn_ref[...]
    qrm = jnp.where(mine, qr, jnp.zeros_like(qr))
    for c in range(T // CW):
        qT_ref[c, 0:MLA_NOPE_DIM, :] = _transpose_bf16(qn[c * CW:(c + 1) * CW, :], eye)
        qT_ref[c, MLA_NOPE_DIM:, :] = _transpose_bf16(qrm[c * CW:(c + 1) * CW, :], eye)

    _causal_attn_loop(qi, kcat, vT_ref, qT_ref, s_ref, p_ref, a_ref, m_ref, acc_ref, T=T, n_maps=1)

    for c in range(T // CW):
        oT = (acc_ref[c, 0:dv, :] / acc_ref[c, dv:dv + 1, :]).astype(BF16)
        for r in range(CW // LANES):
            r0 = c * CW + r * LANES
            o_ref[r0:r0 + LANES, :] = _transpose_bf16(oT[:, r * LANES:(r + 1) * LANES], eye).astype(o_ref.dtype)


def _mla_attn(qm, kv, kr_dup, *, T):
    S = qm.shape[0]
    H = MLA_HEADS
    cw = min(ATTN_CHUNK, T)
    nc = T // cw
    return pl.pallas_call(
        functools.partial(_mla_attn_kernel, T=T),
        grid=(H, S // T),
        in_specs=[
            pl.BlockSpec((T, MLA_NOPE_DIM), lambda h, qi: (qi, h)),
            pl.BlockSpec((T, LANES), lambda h, qi: (qi, H + h // 2)),
            pl.BlockSpec((S, MLA_NOPE_DIM), lambda h, qi: (0, 2 * h)),
            pl.BlockSpec((S, LANES), lambda h, qi: (0, 0)),
            pl.BlockSpec((S, MLA_V_DIM), lambda h, qi: (0, 2 * h + 1)),
        ],
        out_specs=pl.BlockSpec((T, MLA_V_DIM), lambda h, qi: (qi, h)),
        out_shape=jax.ShapeDtypeStruct((S, H * MLA_V_DIM), BF16),
        scratch_shapes=[pltpu.VMEM((S, MLA_NOPE_DIM + LANES), BF16),
                        pltpu.VMEM((S // T, MLA_V_DIM + ONES_ROWS, T), BF16),
                        pltpu.VMEM((nc, MLA_NOPE_DIM + LANES, cw), BF16),
                        pltpu.VMEM((2, nc, T, cw), F32), pltpu.VMEM((2, nc, T, cw), BF16),
                        pltpu.VMEM((2, nc, 1, cw), F32), pltpu.VMEM((nc, 1, cw), F32),
                        pltpu.VMEM((nc, MLA_V_DIM + ONES_ROWS, cw), F32)],
        compiler_params=_cparams(2),
        name="mla_attention",
    )(qm, qm, kv, kr_dup, kv)


def _block_forward(x2d, pos_col, l, norm_mix_w, w_in, lq1, lk1, lq2, lk2, subln_w, q_norm_w, w_uq,
                   kv_norm_w, w_ukv, w_o_diff, w_o_mla, w_out, norm_ffn_w, w_up, conv_w, conv_b, w_down,
                   *, tm=1024, tn=512, t_attn=512, tn_ffn=256, tm_down=512, tm_wide=512, tn_wide=1024):
    S, D = x2d.shape
    H = DIFF_HEADS
    qkv_w = 3 * H * 2 * DIFF_HEAD_DIM
    q_rank = w_uq.shape[0]
    kv_rank = w_ukv.shape[0]
    lat_w = q_rank + kv_rank
    main_w = qkv_w + lat_w
    gate_start = main_w + MLA_ROPE_DIM
    lam_init = 0.8 - 0.6 * math.exp(-0.3 * l)

    cos, sin = _rope_tables(pos_col)
    h = _rmsnorm(x2d, norm_mix_w, BF16)

    w_in_t = jnp.swapaxes(w_in, 0, 1)
    z = _mm([dict(a=h, a_blk=0, K=D, w=w_in_t, w_row_blk=0, w_col_blk=0)], N=main_w, tm=tm, tn=tn,
            out_dtype=BF16, name="in_proj_main", epilogue="rope_lt",
            epi_arg=((2 * H * 2 * DIFF_HEAD_DIM) // tn, (H * 2 * DIFF_HEAD_DIM) // tn,
                     DIFF_HEAD_DIM ** -0.5 * LOG2_E), rope=(cos, sin), w_transposed=True)
    kr_dup = _mm([dict(a=h, a_blk=0, K=D, w=w_in_t, w_row_blk=0, w_col_blk=main_w // LANES)], N=LANES,
                 tm=tm, tn=LANES, out_dtype=BF16, name="in_proj_krope", epilogue="krope_dup",
                 rope=(cos, sin), w_transposed=True)
    gates = _mm([dict(a=h, a_blk=0, K=D, w=w_in_t, w_row_blk=0, w_elem_off=gate_start)], N=2 * D,
                tm=tm_wide, tn=tn_wide, out_dtype=BF16, name="in_proj_gates", epilogue="sigmoid",
                w_transposed=True)

    o_d = _diff_attn(z, lq1, lk1, lq2, lk2, subln_w, lam_init=lam_init, T=t_attn)

    qk_dim = MLA_NOPE_DIM + MLA_ROPE_DIM
    w_uq3 = w_uq.reshape(q_rank, MLA_HEADS, qk_dim)
    w_uq_perm = jnp.concatenate([w_uq3[:, :, :MLA_NOPE_DIM].reshape(q_rank, -1),
                                 w_uq3[:, :, MLA_NOPE_DIM:].reshape(q_rank, -1)], axis=1)
    qm = _mm([dict(a=z, a_blk=qkv_w // q_rank, K=q_rank, w=w_uq_perm, w_row_blk=0, w_col_blk=0)],
             N=MLA_HEADS * qk_dim, tm=tm, tn=tn, out_dtype=BF16, name="mla_q_up", norm_w=q_norm_w,
             epilogue="scale_rope_ge", epi_arg=(qk_dim ** -0.5 * LOG2_E, (MLA_HEADS * MLA_NOPE_DIM) // tn),
             rope=(cos, sin))
    kv = _mm([dict(a=z, a_blk=(qkv_w + q_rank) // kv_rank, K=kv_rank, w=w_ukv, w_row_blk=0, w_col_blk=0)],
             N=w_ukv.shape[1], tm=tm, tn=tn, out_dtype=BF16, name="mla_kv_up", norm_w=kv_norm_w)
    o_m = _mla_attn(qm, kv, kr_dup, T=t_attn)

    y = _mm([dict(a=o_d, a_blk=0, K=o_d.shape[1], w=w_o_diff, w_row_blk=0, w_col_blk=0, gate=(gates, 0)),
             dict(a=o_m, a_blk=0, K=o_m.shape[1], w=w_o_mla, w_row_blk=0, w_col_blk=0, gate=(gates, D // tn))],
            N=D, tm=tm, tn=tn, out_dtype=BF16, name="branch_merge")
    x1 = _mm([dict(a=y, a_blk=0, K=D, w=w_out, w_row_blk=0, w_col_blk=0)], N=D, tm=tm, tn=tn,
             out_dtype=F32, name="out_proj", addend=x2d)

    h2 = _rmsnorm(x1, norm_ffn_w, BF16)
    act = _ffn_up(h2, w_up, conv_w, conv_b, tm=tm, tn=tn_ffn)
    d_ff = act.shape[1]
    k_half = d_ff // 2
    p0 = _mm([dict(a=act, a_blk=0, K=k_half, w=w_down, w_row_blk=0, w_col_blk=0)], N=D, tm=tm_down, tn=tn,
             out_dtype=F32, name="ffn_down_lo", addend=x1)
    x2 = _mm([dict(a=act, a_blk=1, K=k_half, w=w_down, w_row_blk=1, w_col_blk=0)], N=D, tm=tm_down, tn=tn,
             out_dtype=F32, name="ffn_down_hi", addend=p0)
    return x2


def kernel(x, positions, norm_mix_w, w_in, diff_lambda_q1, diff_lambda_k1, diff_lambda_q2, diff_lambda_k2, diff_subln_w, mla_q_norm_w, mla_w_uq, mla_kv_norm_w, mla_w_ukv, w_o_diff, w_o_mla, w_out, norm_ffn_w, ffn_w_up, ffn_conv_w, ffn_conv_b, ffn_w_down, final_norm_w):
    B, S, D = x.shape
    assert B == 1
    x2d = x.reshape(S, D)
    pos_col = positions.reshape(S, 1)
    for l in range(w_in.shape[0]):
        x2d = _block_forward(
            x2d, pos_col, l, norm_mix_w[l], w_in[l], diff_lambda_q1[l], diff_lambda_k1[l], diff_lambda_q2[l],
            diff_lambda_k2[l], diff_subln_w[l], mla_q_norm_w[l], mla_w_uq[l], mla_kv_norm_w[l], mla_w_ukv[l],
            w_o_diff[l], w_o_mla[l], w_out[l], norm_ffn_w[l], ffn_w_up[l], ffn_conv_w[l], ffn_conv_b[l],
            ffn_w_down[l])
    out = _rmsnorm(x2d, final_norm_w, F32)
    return out.reshape(B, S, D)
```

```python
import functools
import math

import jax
import jax.numpy as jnp
from jax import lax
from jax.experimental import pallas as pl
from jax.experimental.pallas import tpu as pltpu

BF16 = jnp.bfloat16
F32 = jnp.float32

LANES = 128
V7X_VMEM_LIMIT_BYTES = 56 << 20

DIFF_HEADS = 16
DIFF_HEAD_DIM = 64
MLA_HEADS = 16
MLA_NOPE_DIM = 128
MLA_ROPE_DIM = 64
MLA_V_DIM = 128
ROPE_THETA = 10000.0
NORM_EPS = 1e-6
DIFF_SUBLN_EPS = 1e-5
CONV_WIDTH = 3
MASK_VALUE = -1e30
LOG2_E = math.log2(math.e)


def _cparams(n_axes, flags=None):
    return pltpu.CompilerParams(
        dimension_semantics=("arbitrary",) * n_axes,
        vmem_limit_bytes=V7X_VMEM_LIMIT_BYTES,
        flags=flags,
    )


def _rope_table_kernel(pos_ref, freq_ref, sign_ref, cos_ref, sin_ref):
    ang = pos_ref[...].astype(F32) * freq_ref[...]
    cos_ref[...] = jnp.cos(ang)
    sin_ref[...] = jnp.sin(ang) * sign_ref[...]


def _rope_tables(pos_col, tm=2048):
    S = pos_col.shape[0]
    half = MLA_ROPE_DIM // 2
    inv_freq = ROPE_THETA ** (-jnp.arange(0, MLA_ROPE_DIM, 2, dtype=F32) / MLA_ROPE_DIM)
    freq = jnp.tile(inv_freq, LANES // half).reshape(1, LANES)
    sign = jnp.tile(jnp.concatenate([-jnp.ones((half,), F32), jnp.ones((half,), F32)]),
                    LANES // (2 * half)).reshape(1, LANES)
    return pl.pallas_call(
        _rope_table_kernel,
        grid=(S // tm,),
        in_specs=[pl.BlockSpec((tm, 1), lambda i: (i, 0)),
                  pl.BlockSpec((1, LANES), lambda i: (0, 0)),
                  pl.BlockSpec((1, LANES), lambda i: (0, 0))],
        out_specs=[pl.BlockSpec((tm, LANES), lambda i: (i, 0)),
                   pl.BlockSpec((tm, LANES), lambda i: (i, 0))],
        out_shape=[jax.ShapeDtypeStruct((S, LANES), F32)] * 2,
        compiler_params=_cparams(1),
        name="rope_tables",
    )(pos_col, freq, sign)


def _rope_partner(zc):
    lane = lax.broadcasted_iota(jnp.int32, zc.shape, 1)
    first_half = (lane & 32) == 0
    return jnp.where(first_half, pltpu.roll(zc, 96, 1), pltpu.roll(zc, 32, 1))


def _rope_lanes(z, cos, sin):
    outs = []
    for c in range(z.shape[1] // LANES):
        zc = z[:, c * LANES:(c + 1) * LANES]
        outs.append(zc * cos + _rope_partner(zc) * sin)
    return outs[0] if len(outs) == 1 else jnp.concatenate(outs, axis=1)


def _rmsnorm_kernel(x_ref, w_ref, o_ref, *, eps):
    xf = x_ref[...].astype(F32)
    ms = jnp.mean(xf * xf, axis=1, keepdims=True)
    o_ref[...] = (xf * lax.rsqrt(ms + eps) * w_ref[...]).astype(o_ref.dtype)


def _rmsnorm(x, w, out_dtype, eps=NORM_EPS, tm=512):
    M, D = x.shape
    return pl.pallas_call(
        functools.partial(_rmsnorm_kernel, eps=eps),
        grid=(M // tm,),
        in_specs=[pl.BlockSpec((tm, D), lambda i: (i, 0)),
                  pl.BlockSpec((1, D), lambda i: (0, 0))],
        out_specs=pl.BlockSpec((tm, D), lambda i: (i, 0)),
        out_shape=jax.ShapeDtypeStruct((M, D), out_dtype),
        compiler_params=_cparams(1),
        name="rmsnorm",
    )(x, w.reshape(1, D).astype(F32))


def _tile_walk(ni, n_steps):
    def cur(s):
        c = jnp.minimum(s, n_steps - 1)
        return c % ni, c // ni

    def prev(s):
        p = jnp.maximum(s - 1, 0)
        return p % ni, p // ni

    return cur, prev


MM_SUB_ROWS = 256
MXU_DIM = 256


def _mm_kernel(*refs, n_pairs, has_gate, has_norm, has_addend, has_rope, epilogue, epi_arg, eps, ni, n_steps,
               tm, sub, w_transposed):
    refs = list(refs)
    pair_refs = []
    for _ in range(n_pairs):
        a_ref = refs.pop(0)
        w_ref = refs.pop(0)
        g_ref = refs.pop(0) if has_gate else None
        pair_refs.append((a_ref, w_ref, g_ref))
    nw_ref = refs.pop(0) if has_norm else None
    add_ref = refs.pop(0) if has_addend else None
    cos_ref = refs.pop(0) if has_rope else None
    sin_ref = refs.pop(0) if has_rope else None
    o_ref = refs.pop(0)
    wbf_refs = refs[:n_pairs]
    raw_refs = refs[n_pairs:]

    s = pl.program_id(0)
    cur, prev = _tile_walk(ni, n_steps)
    i_cur, _ = cur(s)
    _, j = prev(s)

    @pl.when(s == 0)
    def _():
        for raw in raw_refs:
            raw[...] = jnp.zeros(raw.shape, F32)

    @pl.when(i_cur == 0)
    def _():
        for (_, w_ref, _), wbf in zip(pair_refs, wbf_refs):
            if w_transposed:
                eye = _eye_bf16(MXU_DIM)
                for kc in range(wbf.shape[0] // MXU_DIM):
                    ks = slice(kc * MXU_DIM, (kc + 1) * MXU_DIM)
                    wbf[ks, :] = _nt_dot(eye, w_ref[:, ks].astype(BF16)).astype(BF16)
            else:
                wbf[...] = w_ref[...].astype(BF16)

    def finish_previous(rows, roped):
        if has_gate:
            acc = None
            for (_, _, g_ref), raw in zip(pair_refs, raw_refs):
                d = raw[rows, :] * g_ref[rows, :].astype(F32)
                acc = d if acc is None else acc + d
        else:
            acc = raw_refs[0][rows, :]
        if has_addend:
            acc = acc + add_ref[rows, :]
        if epilogue == "sigmoid":
            out = 1.0 / (1.0 + jnp.exp(-acc))
        elif epilogue == "rope_lt":
            _, n_scaled, scale = epi_arg
            out = acc
            if roped:
                out = _rope_lanes(acc, cos_ref[rows, :], sin_ref[rows, :]) * jnp.where(j < n_scaled, scale, 1.0)
        elif epilogue == "scale_rope_ge":
            out = acc * epi_arg[0]
            if roped:
                out = _rope_lanes(out, cos_ref[rows, :], sin_ref[rows, :])
        elif epilogue == "krope_dup":
            lane = lax.broadcasted_iota(jnp.int32, acc.shape, 1)
            kr = jnp.where(lane < MLA_ROPE_DIM, acc, 0.0)
            r = kr * cos_ref[rows, :] + _rope_partner(kr) * sin_ref[rows, :]
            out = r + pltpu.roll(r, MLA_ROPE_DIM, 1)
        else:
            out = acc
        o_ref[rows, :] = out.astype(o_ref.dtype)

    def multiply_current(rows):
        acc = None
        for p, ((a_ref, _, _), wbf) in enumerate(zip(pair_refs, wbf_refs)):
            a = a_ref[rows, :]
            if has_norm:
                af = a.astype(F32)
                ms = jnp.mean(af * af, axis=1, keepdims=True)
                a = (af * lax.rsqrt(ms + eps) * nw_ref[...]).astype(BF16)
            d = jnp.dot(a, wbf[...], preferred_element_type=F32)
            if has_gate:
                raw_refs[p][rows, :] = d
            else:
                acc = d if acc is None else acc + d
        if not has_gate:
            raw_refs[0][rows, :] = acc

    def step(roped):
        for r in range(tm // sub):
            rows = slice(r * sub, (r + 1) * sub)
            finish_previous(rows, roped)
            multiply_current(rows)

    if epilogue == "rope_lt":
        pl.when(j < epi_arg[0])(lambda: step(True))
        pl.when(j >= epi_arg[0])(lambda: step(False))
    elif epilogue == "scale_rope_ge":
        pl.when(j >= epi_arg[1])(lambda: step(True))
        pl.when(j < epi_arg[1])(lambda: step(False))
    else:
        step(False)


def _mm(pairs, *, N, tm, tn, out_dtype, name, epilogue="none", epi_arg=None,
        norm_w=None, addend=None, rope=None, eps=NORM_EPS, w_transposed=False):
    M = pairs[0]["a"].shape[0]
    ni, nj = M // tm, N // tn
    n_steps = ni * nj
    cur, prev = _tile_walk(ni, n_steps)
    has_gate = pairs[0].get("gate") is not None
    args, in_specs, wbf_scratch = [], [], []

    def at_cur(fn):
        return lambda s: fn(*cur(s))

    def at_prev(fn):
        return lambda s: fn(*prev(s))

    for p in pairs:
        K = p["K"]
        args.append(p["a"])
        in_specs.append(pl.BlockSpec((tm, K), at_cur(functools.partial(lambda i, j, b: (i, b), b=p["a_blk"]))))
        args.append(p["w"])
        if not w_transposed:
            in_specs.append(pl.BlockSpec((K, tn), at_cur(functools.partial(
                lambda i, j, r, c: (r, c + j), r=p["w_row_blk"], c=p["w_col_blk"]))))
        elif "w_elem_off" in p:
            in_specs.append(pl.BlockSpec((pl.Element(tn), pl.Element(K)), at_cur(functools.partial(
                lambda i, j, r, off, k: (pl.multiple_of(off + tn * j, math.gcd(off, tn)), r * k),
                r=p["w_row_blk"], off=p["w_elem_off"], k=K))))
        else:
            in_specs.append(pl.BlockSpec((tn, K), at_cur(functools.partial(
                lambda i, j, r, c: (c + j, r), r=p["w_row_blk"], c=p["w_col_blk"]))))
        if has_gate:
            g, g_off = p["gate"]
            args.append(g)
            in_specs.append(pl.BlockSpec((tm, tn), at_prev(functools.partial(lambda i, j, c: (i, c + j), c=g_off))))
        wbf_scratch.append(pltpu.VMEM((K, tn), BF16))
    if norm_w is not None:
        args.append(norm_w.reshape(1, -1).astype(F32))
        in_specs.append(pl.BlockSpec((1, norm_w.shape[-1]), lambda s: (0, 0)))
    if addend is not None:
        args.append(addend)
        in_specs.append(pl.BlockSpec((tm, tn), at_prev(lambda i, j: (i, j))))
    if rope is not None:
        for t in rope:
            args.append(t)
            in_specs.append(pl.BlockSpec((tm, LANES), at_prev(lambda i, j: (i, 0))))
    raw_scratch = [pltpu.VMEM((tm, tn), F32)] * (len(pairs) if has_gate else 1)
    kern = functools.partial(
        _mm_kernel, n_pairs=len(pairs), has_gate=has_gate, has_norm=norm_w is not None,
        has_addend=addend is not None, has_rope=rope is not None, epilogue=epilogue, epi_arg=epi_arg, eps=eps,
        ni=ni, n_steps=n_steps, tm=tm, sub=min(MM_SUB_ROWS, tm), w_transposed=w_transposed)
    return pl.pallas_call(
        kern,
        grid=(n_steps + 1,),
        in_specs=in_specs,
        out_specs=pl.BlockSpec((tm, tn), at_prev(lambda i, j: (i, j))),
        out_shape=jax.ShapeDtypeStruct((M, N), out_dtype),
        scratch_shapes=wbf_scratch + raw_scratch,
        compiler_params=_cparams(1),
        name=name,
    )(*args)


CONV_HALO = 8


def _ffn_up_kernel(a_ref, wg_ref, wv_ref, cwg_ref, cwv_ref, cbg_ref, cbv_ref, o_ref,
                   wgbf, wvbf, ug_buf, uv_buf, *, tm, ni, n_steps):
    s = pl.program_id(0)
    cur, _ = _tile_walk(ni, n_steps)
    i_cur, _ = cur(s)

    @pl.when(s == 0)
    def _():
        ug_buf[...] = jnp.zeros(ug_buf.shape, F32)
        uv_buf[...] = jnp.zeros(uv_buf.shape, F32)

    @pl.when(i_cur == 0)
    def _():
        wgbf[...] = wg_ref[...].astype(BF16)
        wvbf[...] = wv_ref[...].astype(BF16)

    sub = min(MM_SUB_ROWS, tm)

    def conv(buf, r0, cw_ref, cb_ref):
        lo = CONV_HALO + r0
        out = cb_ref[...] + cw_ref[0:1, :] * buf[lo - 2:lo - 2 + sub, :]
        out = out + cw_ref[1:2, :] * buf[lo - 1:lo - 1 + sub, :]
        return out + cw_ref[2:3, :] * buf[lo:lo + sub, :]

    keep = jnp.where(i_cur == 0, 0.0, 1.0)
    halo_g = ug_buf[tm:tm + CONV_HALO, :] * keep
    halo_v = uv_buf[tm:tm + CONV_HALO, :] * keep

    for r in reversed(range(tm // sub)):
        r0 = r * sub
        g = conv(ug_buf, r0, cwg_ref, cbg_ref)
        v = conv(uv_buf, r0, cwv_ref, cbv_ref)
        o_ref[r0:r0 + sub, :] = (g / (1.0 + jnp.exp(-g)) * v).astype(o_ref.dtype)
        a = a_ref[r0:r0 + sub, :]
        ug_buf[CONV_HALO + r0:CONV_HALO + r0 + sub, :] = jnp.dot(a, wgbf[...], preferred_element_type=F32)
        uv_buf[CONV_HALO + r0:CONV_HALO + r0 + sub, :] = jnp.dot(a, wvbf[...], preferred_element_type=F32)

    ug_buf[0:CONV_HALO, :] = halo_g
    uv_buf[0:CONV_HALO, :] = halo_v


def _ffn_up(h, w_up, conv_w, conv_b, *, tm, tn):
    M, K = h.shape
    d_ff = w_up.shape[1] // 2
    ni, nj = M // tm, d_ff // tn
    n_steps = ni * nj
    cur, prev = _tile_walk(ni, n_steps)
    cb = conv_b.reshape(1, -1)

    def at_cur(fn):
        return lambda s: fn(*cur(s))

    def at_prev(fn):
        return lambda s: fn(*prev(s))

    return pl.pallas_call(
        functools.partial(_ffn_up_kernel, tm=tm, ni=ni, n_steps=n_steps),
        grid=(n_steps + 1,),
        in_specs=[
            pl.BlockSpec((tm, K), at_cur(lambda i, j: (i, 0))),
            pl.BlockSpec((K, tn), at_cur(lambda i, j: (0, j))),
            pl.BlockSpec((K, tn), at_cur(lambda i, j: (0, nj + j))),
            pl.BlockSpec((CONV_WIDTH, tn), at_prev(lambda i, j: (0, j))),
            pl.BlockSpec((CONV_WIDTH, tn), at_prev(lambda i, j: (0, nj + j))),
            pl.BlockSpec((1, tn), at_prev(lambda i, j: (0, j))),
            pl.BlockSpec((1, tn), at_prev(lambda i, j: (0, nj + j))),
        ],
        out_specs=pl.BlockSpec((tm, tn), at_prev(lambda i, j: (i, j))),
        out_shape=jax.ShapeDtypeStruct((M, d_ff), BF16),
        scratch_shapes=[pltpu.VMEM((K, tn), BF16), pltpu.VMEM((K, tn), BF16),
                        pltpu.VMEM((CONV_HALO + tm, tn), F32), pltpu.VMEM((CONV_HALO + tm, tn), F32)],
        compiler_params=_cparams(1),
        name="ffn_up_conv_gate",
    )(h, w_up, w_up, conv_w, conv_w, cb, cb)


ATTN_CHUNK = 256


def _nt_dot(a, b):
    return lax.dot_general(a, b, (((1,), (1,)), ((), ())), preferred_element_type=F32)


def _eye_bf16(n):
    r = lax.broadcasted_iota(jnp.int32, (n, n), 0)
    c = lax.broadcasted_iota(jnp.int32, (n, n), 1)
    return jnp.where(r == c, 1.0, 0.0).astype(BF16)


def _transpose_bf16(x, eye):
    return _nt_dot(eye, x).astype(BF16)


SUBLANES = 8
REDUCE_WAYS = 8


def _reduce_rows(x, op, final):
    n = x.shape[0]
    groups = [x[r * SUBLANES:(r + 1) * SUBLANES] for r in range(n // SUBLANES)]
    ways = min(REDUCE_WAYS, len(groups))
    parts = groups[:ways]
    for g, blk in enumerate(groups[ways:]):
        parts[g % ways] = op(parts[g % ways], blk)
    while len(parts) > 1:
        parts = [op(parts[i], parts[i + 1]) if i + 1 < len(parts) else parts[i] for i in range(0, len(parts), 2)]
    return final(parts[0], axis=0, keepdims=True)


def _softmax_chunk(load_s, m_ref, c, mask_q0):
    def scores():
        sT = load_s()
        if mask_q0 is not None:
            key = lax.broadcasted_iota(jnp.int32, sT.shape, 0)
            qq = lax.broadcasted_iota(jnp.int32, sT.shape, 1) + mask_q0
            sT = jnp.where(qq >= key, sT, MASK_VALUE)
        return sT

    m_prev = m_ref[c]
    m_new = jnp.maximum(m_prev, _reduce_rows(scores(), jnp.maximum, jnp.max))
    p = jnp.exp2(scores() - m_new)
    alpha = jnp.exp2(m_prev - m_new)
    m_ref[c] = m_new
    return p.astype(BF16), alpha


ONES_ROWS = 16


def _store_vT(vT_ref, v_ref, eye, T):
    dv = v_ref.shape[1]
    for jb in range(vT_ref.shape[0]):
        vT_ref[jb, 0:dv, :] = _transpose_bf16(v_ref[jb * T:(jb + 1) * T, :], eye)
        vT_ref[jb, dv:dv + ONES_ROWS, :] = jnp.ones((ONES_ROWS, T), BF16)


def _causal_attn_loop(qi, k_ref, vT_ref, qT_ref, s_ref, p_ref, a_ref, m_ref, acc_ref, *, T, n_maps):
    CW = qT_ref.shape[2]
    per_map = T // CW
    chunks = [(c, (c % per_map) * CW) for c in range(n_maps * per_map)]
    m_ref[...] = jnp.full(m_ref.shape, MASK_VALUE, F32)
    acc_ref[...] = jnp.zeros(acc_ref.shape, F32)

    def qk(blk, slot, c):
        k = k_ref[pl.ds(pl.multiple_of(blk * T, T), T), :]
        s_ref[slot, c] = jnp.dot(k, qT_ref[c], preferred_element_type=F32)

    def softmax(slot, c, q0, diagonal):
        n = q0 + CW if diagonal else T
        p, alpha = _softmax_chunk(lambda: s_ref[slot, c, 0:n, :], m_ref, c, q0 if diagonal else None)
        p_ref[slot, c, 0:n, :] = p
        a_ref[slot, c] = alpha

    def pv(blk, slot, c, q0, diagonal):
        n = q0 + CW if diagonal else T
        acc_ref[c] = acc_ref[c] * a_ref[slot, c] + jnp.dot(
            vT_ref[blk, :, 0:n], p_ref[slot, c, 0:n, :], preferred_element_type=F32)

    def step(t, slot):
        for c, q0 in chunks:
            pv(t - 2, slot, c, q0, False)
            qk(t, slot, c)
            softmax(1 - slot, c, q0, False)

    def drain(slot):
        for c, q0 in chunks:
            softmax(slot, c, q0, True)
        for c, q0 in chunks:
            pv(qi, slot, c, q0, True)

    for c, _ in chunks:
        qk(0, 0, c)

    @pl.when(qi == 0)
    def _():
        drain(0)

    @pl.when(qi >= 1)
    def _():
        for c, q0 in chunks:
            qk(1, 1, c)
            softmax(0, c, q0, False)

    def body(u, carry):
        t = 2 + 2 * u
        step(t, 0)
        step(t + 1, 1)
        return carry

    lax.fori_loop(0, lax.shift_right_arithmetic(qi - 1, 1), body, 0)

    @pl.when(jnp.logical_and(qi >= 2, qi % 2 == 0))
    def _():
        step(qi, 0)
        for c, q0 in chunks:
            pv(qi - 1, 1, c, q0, False)
        drain(0)

    @pl.when(qi % 2 == 1)
    def _():
        for c, q0 in chunks:
            pv(qi - 1, 0, c, q0, False)
        drain(1)


def _diff_attn_kernel(q_ref, k_ref, v_ref, lq1_ref, lk1_ref, lq2_ref, lk2_ref, sw_ref, o_ref,
                      vT_ref, qT_ref, s_ref, p_ref, a_ref, m_ref, acc_ref, *, T, lam_init):
    eye = _eye_bf16(LANES)
    dv = v_ref.shape[1]
    CW = qT_ref.shape[2]
    per_map = T // CW
    _store_vT(vT_ref, v_ref, eye, T)
    lam = (jnp.exp(jnp.sum(lq1_ref[...] * lk1_ref[...], axis=1, keepdims=True))
           - jnp.exp(jnp.sum(lq2_ref[...] * lk2_ref[...], axis=1, keepdims=True)) + lam_init)

    def q_block(qi, carry):
        row0 = pl.multiple_of(qi * T, T)
        q = q_ref[pl.ds(row0, T), :]
        lane = lax.broadcasted_iota(jnp.int32, q.shape, 1)
        zero = jnp.zeros_like(q)
        for half, keep in enumerate((lane < DIFF_HEAD_DIM, lane >= DIFF_HEAD_DIM)):
            qh = jnp.where(keep, q, zero)
            for c in range(per_map):
                qT_ref[half * per_map + c] = _transpose_bf16(qh[c * CW:(c + 1) * CW, :], eye)

        _causal_attn_loop(qi, k_ref, vT_ref, qT_ref, s_ref, p_ref, a_ref, m_ref, acc_ref, T=T, n_maps=2)

        for c in range(per_map):
            o1 = acc_ref[c, 0:dv, :] / acc_ref[c, dv:dv + 1, :]
            o2 = acc_ref[per_map + c, 0:dv, :] / acc_ref[per_map + c, dv:dv + 1, :]
            odT = o1 - lam * o2
            ms = jnp.mean(odT * odT, axis=0, keepdims=True)
            outT = (odT * lax.rsqrt(ms + DIFF_SUBLN_EPS) * sw_ref[...] * (1.0 - lam_init)).astype(BF16)
            for r in range(CW // LANES):
                r0 = pl.multiple_of(row0 + c * CW + r * LANES, LANES)
                o_ref[pl.ds(r0, LANES), :] = _transpose_bf16(
                    outT[:, r * LANES:(r + 1) * LANES], eye).astype(o_ref.dtype)
        return carry

    lax.fori_loop(0, q_ref.shape[0] // T, q_block, 0)


def _diff_attn(zqkv, lq1, lk1, lq2, lk2, subln_w, *, lam_init, T):
    S = zqkv.shape[0]
    H = DIFF_HEADS
    hd = 2 * DIFF_HEAD_DIM
    cw = min(ATTN_CHUNK, T)
    nc = 2 * (T // cw)
    vec = lambda a: a.reshape(1, -1).astype(F32)
    small = lambda n: pl.BlockSpec((1, n), lambda h: (0, 0))
    return pl.pallas_call(
        functools.partial(_diff_attn_kernel, T=T, lam_init=lam_init),
        grid=(H,),
        in_specs=[
            pl.BlockSpec((S, hd), lambda h: (0, h)),
            pl.BlockSpec((S, hd), lambda h: (0, H + h)),
            pl.BlockSpec((S, hd), lambda h: (0, 2 * H + h)),
            small(DIFF_HEAD_DIM), small(DIFF_HEAD_DIM), small(DIFF_HEAD_DIM), small(DIFF_HEAD_DIM),
            pl.BlockSpec((hd, 1), lambda h: (0, 0)),
        ],
        out_specs=pl.BlockSpec((S, hd), lambda h: (0, h)),
        out_shape=jax.ShapeDtypeStruct((S, H * hd), BF16),
        scratch_shapes=[pltpu.VMEM((S // T, hd + ONES_ROWS, T), BF16), pltpu.VMEM((nc, hd, cw), BF16),
                        pltpu.VMEM((2, nc, T, cw), F32), pltpu.VMEM((2, nc, T, cw), BF16),
                        pltpu.VMEM((2, nc, 1, cw), F32), pltpu.VMEM((nc, 1, cw), F32),
                        pltpu.VMEM((nc, hd + ONES_ROWS, cw), F32)],
        compiler_params=_cparams(1),
        name="diff_attention",
    )(zqkv, zqkv, zqkv, vec(lq1), vec(lk1), vec(lq2), vec(lk2), subln_w.reshape(-1, 1).astype(F32))


def _mla_attn_kernel(qn_ref, qr_ref, kn_ref, kr_ref, v_ref, o_ref, kcat, vT_ref, qT_ref, s_ref, p_ref, a_ref,
                     m_ref, acc_ref, *, T):
    h = pl.program_id(0)
    eye = _eye_bf16(LANES)
    dv = v_ref.shape[1]
    CW = qT_ref.shape[2]

    kcat[:, 0:MLA_NOPE_DIM] = kn_ref[...]
    kcat[:, MLA_NOPE_DIM:] = kr_ref[...]
    _store_vT(vT_ref, v_ref, eye, T)

    def q_block(qi, carry):
        row0 = pl.multiple_of(qi * T, T)
        qr = qr_ref[pl.ds(row0, T), :]
        lane = lax.broadcasted_iota(jnp.int32, qr.shape, 1)
        lo = (h % 2) * MLA_ROPE_DIM
        mine = jnp.logical_and(lane >= lo, lane < lo + MLA_ROPE_DIM)
        qn = qn_ref[pl.ds(row0, T), :]
        qrm = jnp.where(mine, qr, jnp.zeros_like(qr))
        for c in range(T // CW):
            qT_ref[c, 0:MLA_NOPE_DIM, :] = _transpose_bf16(qn[c * CW:(c + 1) * CW, :], eye)
            qT_ref[c, MLA_NOPE_DIM:, :] = _transpose_bf16(qrm[c * CW:(c + 1) * CW, :], eye)

        _causal_attn_loop(qi, kcat, vT_ref, qT_ref, s_ref, p_ref, a_ref, m_ref, acc_ref, T=T, n_maps=1)

        for c in range(T // CW):
            oT = (acc_ref[c, 0:dv, :] / acc_ref[c, dv:dv + 1, :]).astype(BF16)
            for r in range(CW // LANES):
                r0 = pl.multiple_of(row0 + c * CW + r * LANES, LANES)
                o_ref[pl.ds(r0, LANES), :] = _transpose_bf16(
                    oT[:, r * LANES:(r + 1) * LANES], eye).astype(o_ref.dtype)
        return carry

    lax.fori_loop(0, qn_ref.shape[0] // T, q_block, 0)


def _mla_attn(qm, kv, kr_dup, *, T):
    S = qm.shape[0]
    H = MLA_HEADS
    cw = min(ATTN_CHUNK, T)
    nc = T // cw
    return pl.pallas_call(
        functools.partial(_mla_attn_kernel, T=T),
        grid=(H,),
        in_specs=[
            pl.BlockSpec((S, MLA_NOPE_DIM), lambda h: (0, h)),
            pl.BlockSpec((S, LANES), lambda h: (0, H + h // 2)),
            pl.BlockSpec((S, MLA_NOPE_DIM), lambda h: (0, 2 * h)),
            pl.BlockSpec((S, LANES), lambda h: (0, 0)),
            pl.BlockSpec((S, MLA_V_DIM), lambda h: (0, 2 * h + 1)),
        ],
        out_specs=pl.BlockSpec((S, MLA_V_DIM), lambda h: (0, h)),
        out_shape=jax.ShapeDtypeStruct((S, H * MLA_V_DIM), BF16),
        scratch_shapes=[pltpu.VMEM((S, MLA_NOPE_DIM + LANES), BF16),
                        pltpu.VMEM((S // T, MLA_V_DIM + ONES_ROWS, T), BF16),
                        pltpu.VMEM((nc, MLA_NOPE_DIM + LANES, cw), BF16),
                        pltpu.VMEM((2, nc, T, cw), F32), pltpu.VMEM((2, nc, T, cw), BF16),
                        pltpu.VMEM((2, nc, 1, cw), F32), pltpu.VMEM((nc, 1, cw), F32),
                        pltpu.VMEM((nc, MLA_V_DIM + ONES_ROWS, cw), F32)],
        compiler_params=_cparams(1),
        name="mla_attention",
    )(qm, qm, kv, kr_dup, kv)


def _block_forward(x2d, pos_col, l, norm_mix_w, w_in, lq1, lk1, lq2, lk2, subln_w, q_norm_w, w_uq,
                   kv_norm_w, w_ukv, w_o_diff, w_o_mla, w_out, norm_ffn_w, w_up, conv_w, conv_b, w_down,
                   *, tm=1024, tn=512, t_attn=512, tn_ffn=256, tm_down=512, tm_wide=512, tn_wide=1024,
                   tm_lat=2048, tn_lat=1024):
    S, D = x2d.shape
    H = DIFF_HEADS
    qkv_w = 3 * H * 2 * DIFF_HEAD_DIM
    q_rank = w_uq.shape[0]
    kv_rank = w_ukv.shape[0]
    lat_w = q_rank + kv_rank
    main_w = qkv_w + lat_w
    gate_start = main_w + MLA_ROPE_DIM
    lam_init = 0.8 - 0.6 * math.exp(-0.3 * l)

    cos, sin = _rope_tables(pos_col)
    h = _rmsnorm(x2d, norm_mix_w, BF16)

    w_in_t = jnp.swapaxes(w_in, 0, 1)
    z = _mm([dict(a=h, a_blk=0, K=D, w=w_in_t, w_row_blk=0, w_col_blk=0)], N=main_w, tm=tm, tn=tn,
            out_dtype=BF16, name="in_proj_main", epilogue="rope_lt",
            epi_arg=((2 * H * 2 * DIFF_HEAD_DIM) // tn, (H * 2 * DIFF_HEAD_DIM) // tn,
                     DIFF_HEAD_DIM ** -0.5 * LOG2_E), rope=(cos, sin), w_transposed=True)
    kr_dup = _mm([dict(a=h, a_blk=0, K=D, w=w_in_t, w_row_blk=0, w_col_blk=main_w // LANES)], N=LANES,
                 tm=tm, tn=LANES, out_dtype=BF16, name="in_proj_krope", epilogue="krope_dup",
                 rope=(cos, sin), w_transposed=True)
    gates = _mm([dict(a=h, a_blk=0, K=D, w=w_in_t, w_row_blk=0, w_elem_off=gate_start)], N=2 * D,
                tm=tm_wide, tn=tn_wide, out_dtype=BF16, name="in_proj_gates", epilogue="sigmoid",
                w_transposed=True)

    o_d = _diff_attn(z, lq1, lk1, lq2, lk2, subln_w, lam_init=lam_init, T=t_attn)

    qk_dim = MLA_NOPE_DIM + MLA_ROPE_DIM
    w_uq3 = w_uq.reshape(q_rank, MLA_HEADS, qk_dim)
    w_uq_perm = jnp.concatenate([w_uq3[:, :, :MLA_NOPE_DIM].reshape(q_rank, -1),
                                 w_uq3[:, :, MLA_NOPE_DIM:].reshape(q_rank, -1)], axis=1)
    qm = _mm([dict(a=z, a_blk=qkv_w // q_rank, K=q_rank, w=w_uq_perm, w_row_blk=0, w_col_blk=0)],
             N=MLA_HEADS * qk_dim, tm=tm_lat, tn=tn_lat, out_dtype=BF16, name="mla_q_up", norm_w=q_norm_w,
             epilogue="scale_rope_ge",
             epi_arg=(qk_dim ** -0.5 * LOG2_E, (MLA_HEADS * MLA_NOPE_DIM) // tn_lat), rope=(cos, sin))
    kv = _mm([dict(a=z, a_blk=(qkv_w + q_rank) // kv_rank, K=kv_rank, w=w_ukv, w_row_blk=0, w_col_blk=0)],
             N=w_ukv.shape[1], tm=tm_lat, tn=tn_lat, out_dtype=BF16, name="mla_kv_up", norm_w=kv_norm_w)
    o_m = _mla_attn(qm, kv, kr_dup, T=t_attn)

    y = _mm([dict(a=o_d, a_blk=0, K=o_d.shape[1], w=w_o_diff, w_row_blk=0, w_col_blk=0, gate=(gates, 0)),
             dict(a=o_m, a_blk=0, K=o_m.shape[1], w=w_o_mla, w_row_blk=0, w_col_blk=0, gate=(gates, D // tn))],
            N=D, tm=tm, tn=tn, out_dtype=BF16, name="branch_merge")
    x1 = _mm([dict(a=y, a_blk=0, K=D, w=w_out, w_row_blk=0, w_col_blk=0)], N=D, tm=tm, tn=tn,
             out_dtype=F32, name="out_proj", addend=x2d)

    h2 = _rmsnorm(x1, norm_ffn_w, BF16)
    act = _ffn_up(h2, w_up, conv_w, conv_b, tm=tm, tn=tn_ffn)
    d_ff = act.shape[1]
    k_half = d_ff // 2
    p0 = _mm([dict(a=act, a_blk=0, K=k_half, w=w_down, w_row_blk=0, w_col_blk=0)], N=D, tm=tm_down, tn=tn,
             out_dtype=F32, name="ffn_down_lo", addend=x1)
    x2 = _mm([dict(a=act, a_blk=1, K=k_half, w=w_down, w_row_blk=1, w_col_blk=0)], N=D, tm=tm_down, tn=tn,
             out_dtype=F32, name="ffn_down_hi", addend=p0)
    return x2


def kernel(x, positions, norm_mix_w, w_in, diff_lambda_q1, diff_lambda_k1, diff_lambda_q2, diff_lambda_k2, diff_subln_w, mla_q_norm_w, mla_w_uq, mla_kv_norm_w, mla_w_ukv, w_o_diff, w_o_mla, w_out, norm_ffn_w, ffn_w_up, ffn_conv_w, ffn_conv_b, ffn_w_down, final_norm_w):
    B, S, D = x.shape
    assert B == 1
    x2d = x.reshape(S, D)
    pos_col = positions.reshape(S, 1)
    for l in range(w_in.shape[0]):
        x2d = _block_forward(
            x2d, pos_col, l, norm_mix_w[l], w_in[l], diff_lambda_q1[l], diff_lambda_k1[l], diff_lambda_q2[l],
            diff_lambda_k2[l], diff_subln_w[l], mla_q_norm_w[l], mla_w_uq[l], mla_kv_norm_w[l], mla_w_ukv[l],
            w_o_diff[l], w_o_mla[l], w_out[l], norm_ffn_w[l], ffn_w_up[l], ffn_conv_w[l], ffn_conv_b[l],
            ffn_w_down[l])
    out = _rmsnorm(x2d, final_norm_w, F32)
    return out.reshape(B, S, D)
```

```python
import functools
import math

import jax
import jax.numpy as jnp
from jax import lax
from jax.experimental import pallas as pl
from jax.experimental.pallas import tpu as pltpu

BF16 = jnp.bfloat16
F32 = jnp.float32

LANES = 128
V7X_VMEM_LIMIT_BYTES = 56 << 20

DIFF_HEADS = 16
DIFF_HEAD_DIM = 64
MLA_HEADS = 16
MLA_NOPE_DIM = 128
MLA_ROPE_DIM = 64
MLA_V_DIM = 128
ROPE_THETA = 10000.0
NORM_EPS = 1e-6
DIFF_SUBLN_EPS = 1e-5
CONV_WIDTH = 3
MASK_VALUE = -1e30
LOG2_E = math.log2(math.e)


def _cparams(n_axes, flags=None):
    return pltpu.CompilerParams(
        dimension_semantics=("arbitrary",) * n_axes,
        vmem_limit_bytes=V7X_VMEM_LIMIT_BYTES,
        flags=flags,
    )


def _rope_table_kernel(pos_ref, freq_ref, sign_ref, cos_ref, sin_ref):
    ang = pos_ref[...].astype(F32) * freq_ref[...]
    cos_ref[...] = jnp.cos(ang)
    sin_ref[...] = jnp.sin(ang) * sign_ref[...]


def _rope_tables(pos_col, tm=2048):
    S = pos_col.shape[0]
    half = MLA_ROPE_DIM // 2
    inv_freq = ROPE_THETA ** (-jnp.arange(0, MLA_ROPE_DIM, 2, dtype=F32) / MLA_ROPE_DIM)
    freq = jnp.tile(inv_freq, LANES // half).reshape(1, LANES)
    sign = jnp.tile(jnp.concatenate([-jnp.ones((half,), F32), jnp.ones((half,), F32)]),
                    LANES // (2 * half)).reshape(1, LANES)
    return pl.pallas_call(
        _rope_table_kernel,
        grid=(S // tm,),
        in_specs=[pl.BlockSpec((tm, 1), lambda i: (i, 0)),
                  pl.BlockSpec((1, LANES), lambda i: (0, 0)),
                  pl.BlockSpec((1, LANES), lambda i: (0, 0))],
        out_specs=[pl.BlockSpec((tm, LANES), lambda i: (i, 0)),
                   pl.BlockSpec((tm, LANES), lambda i: (i, 0))],
        out_shape=[jax.ShapeDtypeStruct((S, LANES), F32)] * 2,
        compiler_params=_cparams(1),
        name="rope_tables",
    )(pos_col, freq, sign)


def _rope_partner(zc):
    lane = lax.broadcasted_iota(jnp.int32, zc.shape, 1)
    first_half = (lane & 32) == 0
    return jnp.where(first_half, pltpu.roll(zc, 96, 1), pltpu.roll(zc, 32, 1))


def _rope_lanes(z, cos, sin):
    outs = []
    for c in range(z.shape[1] // LANES):
        zc = z[:, c * LANES:(c + 1) * LANES]
        outs.append(zc * cos + _rope_partner(zc) * sin)
    return outs[0] if len(outs) == 1 else jnp.concatenate(outs, axis=1)


def _rmsnorm_kernel(x_ref, w_ref, o_ref, *, eps):
    xf = x_ref[...].astype(F32)
    ms = jnp.mean(xf * xf, axis=1, keepdims=True)
    o_ref[...] = (xf * lax.rsqrt(ms + eps) * w_ref[...]).astype(o_ref.dtype)


def _rmsnorm(x, w, out_dtype, eps=NORM_EPS, tm=512):
    M, D = x.shape
    return pl.pallas_call(
        functools.partial(_rmsnorm_kernel, eps=eps),
        grid=(M // tm,),
        in_specs=[pl.BlockSpec((tm, D), lambda i: (i, 0)),
                  pl.BlockSpec((1, D), lambda i: (0, 0))],
        out_specs=pl.BlockSpec((tm, D), lambda i: (i, 0)),
        out_shape=jax.ShapeDtypeStruct((M, D), out_dtype),
        compiler_params=_cparams(1),
        name="rmsnorm",
    )(x, w.reshape(1, D).astype(F32))


def _tile_walk(ni, n_steps):
    def cur(s):
        c = jnp.minimum(s, n_steps - 1)
        return c % ni, c // ni

    def prev(s):
        p = jnp.maximum(s - 1, 0)
        return p % ni, p // ni

    return cur, prev


MM_SUB_ROWS = 256
MXU_DIM = 256


def _mm_kernel(*refs, n_pairs, has_gate, has_norm, has_addend, has_rope, epilogue, epi_arg, eps, ni, n_steps,
               tm, sub, w_transposed):
    refs = list(refs)
    pair_refs = []
    for _ in range(n_pairs):
        a_ref = refs.pop(0)
        w_ref = refs.pop(0)
        g_ref = refs.pop(0) if has_gate else None
        pair_refs.append((a_ref, w_ref, g_ref))
    nw_ref = refs.pop(0) if has_norm else None
    add_ref = refs.pop(0) if has_addend else None
    cos_ref = refs.pop(0) if has_rope else None
    sin_ref = refs.pop(0) if has_rope else None
    o_ref = refs.pop(0)
    wbf_refs = refs[:n_pairs]
    raw_refs = refs[n_pairs:]

    s = pl.program_id(0)
    cur, prev = _tile_walk(ni, n_steps)
    i_cur, _ = cur(s)
    _, j = prev(s)

    @pl.when(s == 0)
    def _():
        for raw in raw_refs:
            raw[...] = jnp.zeros(raw.shape, F32)

    @pl.when(i_cur == 0)
    def _():
        for (_, w_ref, _), wbf in zip(pair_refs, wbf_refs):
            if w_transposed:
                eye = _eye_bf16(MXU_DIM)
                for kc in range(wbf.shape[0] // MXU_DIM):
                    ks = slice(kc * MXU_DIM, (kc + 1) * MXU_DIM)
                    wbf[ks, :] = _nt_dot(eye, w_ref[:, ks].astype(BF16)).astype(BF16)
            else:
                wbf[...] = w_ref[...].astype(BF16)

    def finish_previous(rows, roped):
        if has_gate:
            acc = None
            for (_, _, g_ref), raw in zip(pair_refs, raw_refs):
                d = raw[rows, :] * g_ref[rows, :].astype(F32)
                acc = d if acc is None else acc + d
        else:
            acc = raw_refs[0][rows, :]
        if has_addend:
            acc = acc + add_ref[rows, :]
        if epilogue == "sigmoid":
            out = 1.0 / (1.0 + jnp.exp(-acc))
        elif epilogue == "rope_lt":
            _, n_scaled, scale = epi_arg
            out = acc
            if roped:
                out = _rope_lanes(acc, cos_ref[rows, :], sin_ref[rows, :]) * jnp.where(j < n_scaled, scale, 1.0)
        elif epilogue == "scale_rope_ge":
            out = acc * epi_arg[0]
            if roped:
                out = _rope_lanes(out, cos_ref[rows, :], sin_ref[rows, :])
        elif epilogue == "krope_dup":
            lane = lax.broadcasted_iota(jnp.int32, acc.shape, 1)
            kr = jnp.where(lane < MLA_ROPE_DIM, acc, 0.0)
            r = kr * cos_ref[rows, :] + _rope_partner(kr) * sin_ref[rows, :]
            out = r + pltpu.roll(r, MLA_ROPE_DIM, 1)
        else:
            out = acc
        o_ref[rows, :] = out.astype(o_ref.dtype)

    def multiply_current(rows):
        acc = None
        for p, ((a_ref, _, _), wbf) in enumerate(zip(pair_refs, wbf_refs)):
            a = a_ref[rows, :]
            if has_norm:
                af = a.astype(F32)
                ms = jnp.mean(af * af, axis=1, keepdims=True)
                a = (af * lax.rsqrt(ms + eps) * nw_ref[...]).astype(BF16)
            d = jnp.dot(a, wbf[...], preferred_element_type=F32)
            if has_gate:
                raw_refs[p][rows, :] = d
            else:
                acc = d if acc is None else acc + d
        if not has_gate:
            raw_refs[0][rows, :] = acc

    def step(roped):
        for r in range(tm // sub):
            rows = slice(r * sub, (r + 1) * sub)
            finish_previous(rows, roped)
            multiply_current(rows)

    if epilogue == "rope_lt":
        pl.when(j < epi_arg[0])(lambda: step(True))
        pl.when(j >= epi_arg[0])(lambda: step(False))
    elif epilogue == "scale_rope_ge":
        pl.when(j >= epi_arg[1])(lambda: step(True))
        pl.when(j < epi_arg[1])(lambda: step(False))
    else:
        step(False)


def _mm(pairs, *, N, tm, tn, out_dtype, name, epilogue="none", epi_arg=None,
        norm_w=None, addend=None, rope=None, eps=NORM_EPS, w_transposed=False):
    M = pairs[0]["a"].shape[0]
    ni, nj = M // tm, N // tn
    n_steps = ni * nj
    cur, prev = _tile_walk(ni, n_steps)
    has_gate = pairs[0].get("gate") is not None
    args, in_specs, wbf_scratch = [], [], []

    def at_cur(fn):
        return lambda s: fn(*cur(s))

    def at_prev(fn):
        return lambda s: fn(*prev(s))

    for p in pairs:
        K = p["K"]
        args.append(p["a"])
        in_specs.append(pl.BlockSpec((tm, K), at_cur(functools.partial(lambda i, j, b: (i, b), b=p["a_blk"]))))
        args.append(p["w"])
        if not w_transposed:
            in_specs.append(pl.BlockSpec((K, tn), at_cur(functools.partial(
                lambda i, j, r, c: (r, c + j), r=p["w_row_blk"], c=p["w_col_blk"]))))
        elif "w_elem_off" in p:
            in_specs.append(pl.BlockSpec((pl.Element(tn), pl.Element(K)), at_cur(functools.partial(
                lambda i, j, r, off, k: (pl.multiple_of(off + tn * j, math.gcd(off, tn)), r * k),
                r=p["w_row_blk"], off=p["w_elem_off"], k=K))))
        else:
            in_specs.append(pl.BlockSpec((tn, K), at_cur(functools.partial(
                lambda i, j, r, c: (c + j, r), r=p["w_row_blk"], c=p["w_col_blk"]))))
        if has_gate:
            g, g_off = p["gate"]
            args.append(g)
            in_specs.append(pl.BlockSpec((tm, tn), at_prev(functools.partial(lambda i, j, c: (i, c + j), c=g_off))))
        wbf_scratch.append(pltpu.VMEM((K, tn), BF16))
    if norm_w is not None:
        args.append(norm_w.reshape(1, -1).astype(F32))
        in_specs.append(pl.BlockSpec((1, norm_w.shape[-1]), lambda s: (0, 0)))
    if addend is not None:
        args.append(addend)
        in_specs.append(pl.BlockSpec((tm, tn), at_prev(lambda i, j: (i, j))))
    if rope is not None:
        for t in rope:
            args.append(t)
            in_specs.append(pl.BlockSpec((tm, LANES), at_prev(lambda i, j: (i, 0))))
    raw_scratch = [pltpu.VMEM((tm, tn), F32)] * (len(pairs) if has_gate else 1)
    kern = functools.partial(
        _mm_kernel, n_pairs=len(pairs), has_gate=has_gate, has_norm=norm_w is not None,
        has_addend=addend is not None, has_rope=rope is not None, epilogue=epilogue, epi_arg=epi_arg, eps=eps,
        ni=ni, n_steps=n_steps, tm=tm, sub=min(MM_SUB_ROWS, tm), w_transposed=w_transposed)
    return pl.pallas_call(
        kern,
        grid=(n_steps + 1,),
        in_specs=in_specs,
        out_specs=pl.BlockSpec((tm, tn), at_prev(lambda i, j: (i, j))),
        out_shape=jax.ShapeDtypeStruct((M, N), out_dtype),
        scratch_shapes=wbf_scratch + raw_scratch,
        compiler_params=_cparams(1),
        name=name,
    )(*args)


CONV_HALO = 8


def _ffn_up_kernel(a_ref, wg_ref, wv_ref, cwg_ref, cwv_ref, cbg_ref, cbv_ref, o_ref,
                   wgbf, wvbf, ug_buf, uv_buf, *, tm, ni, n_steps):
    s = pl.program_id(0)
    cur, _ = _tile_walk(ni, n_steps)
    i_cur, _ = cur(s)

    @pl.when(s == 0)
    def _():
        ug_buf[...] = jnp.zeros(ug_buf.shape, F32)
        uv_buf[...] = jnp.zeros(uv_buf.shape, F32)

    @pl.when(i_cur == 0)
    def _():
        wgbf[...] = wg_ref[...].astype(BF16)
        wvbf[...] = wv_ref[...].astype(BF16)

    sub = min(MM_SUB_ROWS, tm)

    def conv(buf, r0, cw_ref, cb_ref):
        lo = CONV_HALO + r0
        out = cb_ref[...] + cw_ref[0:1, :] * buf[lo - 2:lo - 2 + sub, :]
        out = out + cw_ref[1:2, :] * buf[lo - 1:lo - 1 + sub, :]
        return out + cw_ref[2:3, :] * buf[lo:lo + sub, :]

    keep = jnp.where(i_cur == 0, 0.0, 1.0)
    halo_g = ug_buf[tm:tm + CONV_HALO, :] * keep
    halo_v = uv_buf[tm:tm + CONV_HALO, :] * keep

    for r in reversed(range(tm // sub)):
        r0 = r * sub
        g = conv(ug_buf, r0, cwg_ref, cbg_ref)
        v = conv(uv_buf, r0, cwv_ref, cbv_ref)
        o_ref[r0:r0 + sub, :] = (g / (1.0 + jnp.exp(-g)) * v).astype(o_ref.dtype)
        a = a_ref[r0:r0 + sub, :]
        ug_buf[CONV_HALO + r0:CONV_HALO + r0 + sub, :] = jnp.dot(a, wgbf[...], preferred_element_type=F32)
        uv_buf[CONV_HALO + r0:CONV_HALO + r0 + sub, :] = jnp.dot(a, wvbf[...], preferred_element_type=F32)

    ug_buf[0:CONV_HALO, :] = halo_g
    uv_buf[0:CONV_HALO, :] = halo_v


def _ffn_up(h, w_up, conv_w, conv_b, *, tm, tn):
    M, K = h.shape
    d_ff = w_up.shape[1] // 2
    ni, nj = M // tm, d_ff // tn
    n_steps = ni * nj
    cur, prev = _tile_walk(ni, n_steps)
    cb = conv_b.reshape(1, -1)

    def at_cur(fn):
        return lambda s: fn(*cur(s))

    def at_prev(fn):
        return lambda s: fn(*prev(s))

    return pl.pallas_call(
        functools.partial(_ffn_up_kernel, tm=tm, ni=ni, n_steps=n_steps),
        grid=(n_steps + 1,),
        in_specs=[
            pl.BlockSpec((tm, K), at_cur(lambda i, j: (i, 0))),
            pl.BlockSpec((K, tn), at_cur(lambda i, j: (0, j))),
            pl.BlockSpec((K, tn), at_cur(lambda i, j: (0, nj + j))),
            pl.BlockSpec((CONV_WIDTH, tn), at_prev(lambda i, j: (0, j))),
            pl.BlockSpec((CONV_WIDTH, tn), at_prev(lambda i, j: (0, nj + j))),
            pl.BlockSpec((1, tn), at_prev(lambda i, j: (0, j))),
            pl.BlockSpec((1, tn), at_prev(lambda i, j: (0, nj + j))),
        ],
        out_specs=pl.BlockSpec((tm, tn), at_prev(lambda i, j: (i, j))),
        out_shape=jax.ShapeDtypeStruct((M, d_ff), BF16),
        scratch_shapes=[pltpu.VMEM((K, tn), BF16), pltpu.VMEM((K, tn), BF16),
                        pltpu.VMEM((CONV_HALO + tm, tn), F32), pltpu.VMEM((CONV_HALO + tm, tn), F32)],
        compiler_params=_cparams(1),
        name="ffn_up_conv_gate",
    )(h, w_up, w_up, conv_w, conv_w, cb, cb)


ATTN_CHUNK = 256
DIFF_STAGE_ORDER = "q|sp"
MLA_STAGE_ORDER = "pqs"


def _nt_dot(a, b):
    return lax.dot_general(a, b, (((1,), (1,)), ((), ())), preferred_element_type=F32)


def _eye_bf16(n):
    r = lax.broadcasted_iota(jnp.int32, (n, n), 0)
    c = lax.broadcasted_iota(jnp.int32, (n, n), 1)
    return jnp.where(r == c, 1.0, 0.0).astype(BF16)


def _transpose_bf16(x, eye):
    return _nt_dot(eye, x).astype(BF16)


SUBLANES = 8
REDUCE_WAYS = 8


def _reduce_rows(x, op, final):
    n = x.shape[0]
    groups = [x[r * SUBLANES:(r + 1) * SUBLANES] for r in range(n // SUBLANES)]
    ways = min(REDUCE_WAYS, len(groups))
    parts = groups[:ways]
    for g, blk in enumerate(groups[ways:]):
        parts[g % ways] = op(parts[g % ways], blk)
    while len(parts) > 1:
        parts = [op(parts[i], parts[i + 1]) if i + 1 < len(parts) else parts[i] for i in range(0, len(parts), 2)]
    return final(parts[0], axis=0, keepdims=True)


def _softmax_chunk(load_s, m_ref, c, mask_q0):
    def scores():
        sT = load_s()
        if mask_q0 is not None:
            key = lax.broadcasted_iota(jnp.int32, sT.shape, 0)
            qq = lax.broadcasted_iota(jnp.int32, sT.shape, 1) + mask_q0
            sT = jnp.where(qq >= key, sT, MASK_VALUE)
        return sT

    m_prev = m_ref[c]
    m_new = jnp.maximum(m_prev, _reduce_rows(scores(), jnp.maximum, jnp.max))
    p = jnp.exp2(scores() - m_new)
    alpha = jnp.exp2(m_prev - m_new)
    m_ref[c] = m_new
    return p.astype(BF16), alpha


ONES_ROWS = 16


def _store_vT(vT_ref, v_ref, eye, T):
    dv = v_ref.shape[1]
    for jb in range(vT_ref.shape[0]):
        vT_ref[jb, 0:dv, :] = _transpose_bf16(v_ref[jb * T:(jb + 1) * T, :], eye)
        vT_ref[jb, dv:dv + ONES_ROWS, :] = jnp.ones((ONES_ROWS, T), BF16)


def _causal_attn_loop(qi, k_ref, vT_ref, qT_ref, s_ref, p_ref, a_ref, m_ref, acc_ref, *, T, n_maps, order):
    CW = qT_ref.shape[2]
    per_map = T // CW
    chunks = [(c, (c % per_map) * CW) for c in range(n_maps * per_map)]
    m_ref[...] = jnp.full(m_ref.shape, MASK_VALUE, F32)
    acc_ref[...] = jnp.zeros(acc_ref.shape, F32)

    def qk(blk, slot, c):
        k = k_ref[pl.ds(pl.multiple_of(blk * T, T), T), :]
        s_ref[slot, c] = jnp.dot(k, qT_ref[c], preferred_element_type=F32)

    def softmax(slot, c, q0, diagonal):
        n = q0 + CW if diagonal else T
        p, alpha = _softmax_chunk(lambda: s_ref[slot, c, 0:n, :], m_ref, c, q0 if diagonal else None)
        p_ref[slot, c, 0:n, :] = p
        a_ref[slot, c] = alpha

    def pv(blk, slot, c, q0, diagonal):
        n = q0 + CW if diagonal else T
        acc_ref[c] = acc_ref[c] * a_ref[slot, c] + jnp.dot(
            vT_ref[blk, :, 0:n], p_ref[slot, c, 0:n, :], preferred_element_type=F32)

    def step(t, slot):
        stage = {"q": lambda c, q0: qk(t, slot, c),
                 "s": lambda c, q0: softmax(1 - slot, c, q0, False),
                 "p": lambda c, q0: pv(t - 2, slot, c, q0, False)}
        for group in order.split("|"):
            for c, q0 in chunks:
                for name in group:
                    stage[name](c, q0)

    def drain(slot):
        for c, q0 in chunks:
            softmax(slot, c, q0, True)
            pv(qi, slot, c, q0, True)

    for c, _ in chunks:
        qk(0, 0, c)

    @pl.when(qi == 0)
    def _():
        drain(0)

    @pl.when(qi >= 1)
    def _():
        for c, q0 in chunks:
            qk(1, 1, c)
            softmax(0, c, q0, False)

    def body(u, carry):
        t = 2 + 2 * u
        step(t, 0)
        step(t + 1, 1)
        return carry

    lax.fori_loop(0, lax.shift_right_arithmetic(qi - 1, 1), body, 0)

    @pl.when(jnp.logical_and(qi >= 2, qi % 2 == 0))
    def _():
        step(qi, 0)
        for c, q0 in chunks:
            pv(qi - 1, 1, c, q0, False)
        drain(0)

    @pl.when(qi % 2 == 1)
    def _():
        for c, q0 in chunks:
            pv(qi - 1, 0, c, q0, False)
        drain(1)


def _diff_attn_kernel(q_ref, k_ref, v_ref, lq1_ref, lk1_ref, lq2_ref, lk2_ref, sw_ref, o_ref,
                      vT_ref, qT_ref, s_ref, p_ref, a_ref, m_ref, acc_ref, *, T, lam_init):
    eye = _eye_bf16(LANES)
    dv = v_ref.shape[1]
    CW = qT_ref.shape[2]
    per_map = T // CW
    _store_vT(vT_ref, v_ref, eye, T)
    lam = (jnp.exp(jnp.sum(lq1_ref[...] * lk1_ref[...], axis=1, keepdims=True))
           - jnp.exp(jnp.sum(lq2_ref[...] * lk2_ref[...], axis=1, keepdims=True)) + lam_init)

    def q_block(qi, carry):
        row0 = pl.multiple_of(qi * T, T)
        q = q_ref[pl.ds(row0, T), :]
        lane = lax.broadcasted_iota(jnp.int32, q.shape, 1)
        zero = jnp.zeros_like(q)
        for half, keep in enumerate((lane < DIFF_HEAD_DIM, lane >= DIFF_HEAD_DIM)):
            qh = jnp.where(keep, q, zero)
            for c in range(per_map):
                qT_ref[half * per_map + c] = _transpose_bf16(qh[c * CW:(c + 1) * CW, :], eye)

        _causal_attn_loop(qi, k_ref, vT_ref, qT_ref, s_ref, p_ref, a_ref, m_ref, acc_ref, T=T, n_maps=2,
                          order=DIFF_STAGE_ORDER)

        for c in range(per_map):
            o1 = acc_ref[c, 0:dv, :] / acc_ref[c, dv:dv + 1, :]
            o2 = acc_ref[per_map + c, 0:dv, :] / acc_ref[per_map + c, dv:dv + 1, :]
            odT = o1 - lam * o2
            ms = jnp.mean(odT * odT, axis=0, keepdims=True)
            outT = (odT * lax.rsqrt(ms + DIFF_SUBLN_EPS) * sw_ref[...] * (1.0 - lam_init)).astype(BF16)
            for r in range(CW // LANES):
                r0 = pl.multiple_of(row0 + c * CW + r * LANES, LANES)
                o_ref[pl.ds(r0, LANES), :] = _transpose_bf16(
                    outT[:, r * LANES:(r + 1) * LANES], eye).astype(o_ref.dtype)
        return carry

    lax.fori_loop(0, q_ref.shape[0] // T, q_block, 0)


def _diff_attn(zqkv, lq1, lk1, lq2, lk2, subln_w, *, lam_init, T):
    S = zqkv.shape[0]
    H = DIFF_HEADS
    hd = 2 * DIFF_HEAD_DIM
    cw = min(ATTN_CHUNK, T)
    nc = 2 * (T // cw)
    vec = lambda a: a.reshape(1, -1).astype(F32)
    small = lambda n: pl.BlockSpec((1, n), lambda h: (0, 0))
    return pl.pallas_call(
        functools.partial(_diff_attn_kernel, T=T, lam_init=lam_init),
        grid=(H,),
        in_specs=[
            pl.BlockSpec((S, hd), lambda h: (0, h)),
            pl.BlockSpec((S, hd), lambda h: (0, H + h)),
            pl.BlockSpec((S, hd), lambda h: (0, 2 * H + h)),
            small(DIFF_HEAD_DIM), small(DIFF_HEAD_DIM), small(DIFF_HEAD_DIM), small(DIFF_HEAD_DIM),
            pl.BlockSpec((hd, 1), lambda h: (0, 0)),
        ],
        out_specs=pl.BlockSpec((S, hd), lambda h: (0, h)),
        out_shape=jax.ShapeDtypeStruct((S, H * hd), BF16),
        scratch_shapes=[pltpu.VMEM((S // T, hd + ONES_ROWS, T), BF16), pltpu.VMEM((nc, hd, cw), BF16),
                        pltpu.VMEM((2, nc, T, cw), F32), pltpu.VMEM((2, nc, T, cw), BF16),
                        pltpu.VMEM((2, nc, 1, cw), F32), pltpu.VMEM((nc, 1, cw), F32),
                        pltpu.VMEM((nc, hd + ONES_ROWS, cw), F32)],
        compiler_params=_cparams(1),
        name="diff_attention",
    )(zqkv, zqkv, zqkv, vec(lq1), vec(lk1), vec(lq2), vec(lk2), subln_w.reshape(-1, 1).astype(F32))


def _mla_attn_kernel(qn_ref, qr_ref, kn_ref, kr_ref, v_ref, o_ref, kcat, vT_ref, qT_ref, s_ref, p_ref, a_ref,
                     m_ref, acc_ref, *, T):
    h = pl.program_id(0)
    eye = _eye_bf16(LANES)
    dv = v_ref.shape[1]
    CW = qT_ref.shape[2]

    kcat[:, 0:MLA_NOPE_DIM] = kn_ref[...]
    kcat[:, MLA_NOPE_DIM:] = kr_ref[...]
    _store_vT(vT_ref, v_ref, eye, T)

    def q_block(qi, carry):
        row0 = pl.multiple_of(qi * T, T)
        qr = qr_ref[pl.ds(row0, T), :]
        lane = lax.broadcasted_iota(jnp.int32, qr.shape, 1)
        lo = (h % 2) * MLA_ROPE_DIM
        mine = jnp.logical_and(lane >= lo, lane < lo + MLA_ROPE_DIM)
        qn = qn_ref[pl.ds(row0, T), :]
        qrm = jnp.where(mine, qr, jnp.zeros_like(qr))
        for c in range(T // CW):
            qT_ref[c, 0:MLA_NOPE_DIM, :] = _transpose_bf16(qn[c * CW:(c + 1) * CW, :], eye)
            qT_ref[c, MLA_NOPE_DIM:, :] = _transpose_bf16(qrm[c * CW:(c + 1) * CW, :], eye)

        _causal_attn_loop(qi, kcat, vT_ref, qT_ref, s_ref, p_ref, a_ref, m_ref, acc_ref, T=T, n_maps=1,
                          order=MLA_STAGE_ORDER)

        for c in range(T // CW):
            oT = (acc_ref[c, 0:dv, :] / acc_ref[c, dv:dv + 1, :]).astype(BF16)
            for r in range(CW // LANES):
                r0 = pl.multiple_of(row0 + c * CW + r * LANES, LANES)
                o_ref[pl.ds(r0, LANES), :] = _transpose_bf16(
                    oT[:, r * LANES:(r + 1) * LANES], eye).astype(o_ref.dtype)
        return carry

    lax.fori_loop(0, qn_ref.shape[0] // T, q_block, 0)


def _mla_attn(qm, kv, kr_dup, *, T):
    S = qm.shape[0]
    H = MLA_HEADS
    cw = min(ATTN_CHUNK, T)
    nc = T // cw
    return pl.pallas_call(
        functools.partial(_mla_attn_kernel, T=T),
        grid=(H,),
        in_specs=[
            pl.BlockSpec((S, MLA_NOPE_DIM), lambda h: (0, h)),
            pl.BlockSpec((S, LANES), lambda h: (0, H + h // 2)),
            pl.BlockSpec((S, MLA_NOPE_DIM), lambda h: (0, 2 * h)),
            pl.BlockSpec((S, LANES), lambda h: (0, 0)),
            pl.BlockSpec((S, MLA_V_DIM), lambda h: (0, 2 * h + 1)),
        ],
        out_specs=pl.BlockSpec((S, MLA_V_DIM), lambda h: (0, h)),
        out_shape=jax.ShapeDtypeStruct((S, H * MLA_V_DIM), BF16),
        scratch_shapes=[pltpu.VMEM((S, MLA_NOPE_DIM + LANES), BF16),
                        pltpu.VMEM((S // T, MLA_V_DIM + ONES_ROWS, T), BF16),
                        pltpu.VMEM((nc, MLA_NOPE_DIM + LANES, cw), BF16),
                        pltpu.VMEM((2, nc, T, cw), F32), pltpu.VMEM((2, nc, T, cw), BF16),
                        pltpu.VMEM((2, nc, 1, cw), F32), pltpu.VMEM((nc, 1, cw), F32),
                        pltpu.VMEM((nc, MLA_V_DIM + ONES_ROWS, cw), F32)],
        compiler_params=_cparams(1),
        name="mla_attention",
    )(qm, qm, kv, kr_dup, kv)


def _block_forward(x2d, pos_col, l, norm_mix_w, w_in, lq1, lk1, lq2, lk2, subln_w, q_norm_w, w_uq,
                   kv_norm_w, w_ukv, w_o_diff, w_o_mla, w_out, norm_ffn_w, w_up, conv_w, conv_b, w_down,
                   *, tm=1024, tn=512, t_attn=512, tn_ffn=256, tm_down=512, tm_wide=512, tn_wide=1024,
                   tm_lat=2048, tn_lat=1024):
    S, D = x2d.shape
    H = DIFF_HEADS
    qkv_w = 3 * H * 2 * DIFF_HEAD_DIM
    q_rank = w_uq.shape[0]
    kv_rank = w_ukv.shape[0]
    lat_w = q_rank + kv_rank
    main_w = qkv_w + lat_w
    gate_start = main_w + MLA_ROPE_DIM
    lam_init = 0.8 - 0.6 * math.exp(-0.3 * l)

    cos, sin = _rope_tables(pos_col)
    h = _rmsnorm(x2d, norm_mix_w, BF16)

    w_in_t = jnp.swapaxes(w_in, 0, 1)
    z = _mm([dict(a=h, a_blk=0, K=D, w=w_in_t, w_row_blk=0, w_col_blk=0)], N=main_w, tm=tm, tn=tn,
            out_dtype=BF16, name="in_proj_main", epilogue="rope_lt",
            epi_arg=((2 * H * 2 * DIFF_HEAD_DIM) // tn, (H * 2 * DIFF_HEAD_DIM) // tn,
                     DIFF_HEAD_DIM ** -0.5 * LOG2_E), rope=(cos, sin), w_transposed=True)
    kr_dup = _mm([dict(a=h, a_blk=0, K=D, w=w_in_t, w_row_blk=0, w_col_blk=main_w // LANES)], N=LANES,
                 tm=tm, tn=LANES, out_dtype=BF16, name="in_proj_krope", epilogue="krope_dup",
                 rope=(cos, sin), w_transposed=True)
    gates = _mm([dict(a=h, a_blk=0, K=D, w=w_in_t, w_row_blk=0, w_elem_off=gate_start)], N=2 * D,
                tm=tm_wide, tn=tn_wide, out_dtype=BF16, name="in_proj_gates", epilogue="sigmoid",
                w_transposed=True)

    o_d = _diff_attn(z, lq1, lk1, lq2, lk2, subln_w, lam_init=lam_init, T=t_attn)

    qk_dim = MLA_NOPE_DIM + MLA_ROPE_DIM
    w_uq3 = w_uq.reshape(q_rank, MLA_HEADS, qk_dim)
    w_uq_perm = jnp.concatenate([w_uq3[:, :, :MLA_NOPE_DIM].reshape(q_rank, -1),
                                 w_uq3[:, :, MLA_NOPE_DIM:].reshape(q_rank, -1)], axis=1)
    qm = _mm([dict(a=z, a_blk=qkv_w // q_rank, K=q_rank, w=w_uq_perm, w_row_blk=0, w_col_blk=0)],
             N=MLA_HEADS * qk_dim, tm=tm_lat, tn=tn_lat, out_dtype=BF16, name="mla_q_up", norm_w=q_norm_w,
             epilogue="scale_rope_ge",
             epi_arg=(qk_dim ** -0.5 * LOG2_E, (MLA_HEADS * MLA_NOPE_DIM) // tn_lat), rope=(cos, sin))
    kv = _mm([dict(a=z, a_blk=(qkv_w + q_rank) // kv_rank, K=kv_rank, w=w_ukv, w_row_blk=0, w_col_blk=0)],
             N=w_ukv.shape[1], tm=tm_lat, tn=tn_lat, out_dtype=BF16, name="mla_kv_up", norm_w=kv_norm_w)
    o_m = _mla_attn(qm, kv, kr_dup, T=t_attn)

    y = _mm([dict(a=o_d, a_blk=0, K=o_d.shape[1], w=w_o_diff, w_row_blk=0, w_col_blk=0, gate=(gates, 0)),
             dict(a=o_m, a_blk=0, K=o_m.shape[1], w=w_o_mla, w_row_blk=0, w_col_blk=0, gate=(gates, D // tn))],
            N=D, tm=tm, tn=tn, out_dtype=BF16, name="branch_merge")
    x1 = _mm([dict(a=y, a_blk=0, K=D, w=w_out, w_row_blk=0, w_col_blk=0)], N=D, tm=tm, tn=tn,
             out_dtype=F32, name="out_proj", addend=x2d)

    h2 = _rmsnorm(x1, norm_ffn_w, BF16)
    act = _ffn_up(h2, w_up, conv_w, conv_b, tm=tm, tn=tn_ffn)
    d_ff = act.shape[1]
    k_half = d_ff // 2
    p0 = _mm([dict(a=act, a_blk=0, K=k_half, w=w_down, w_row_blk=0, w_col_blk=0)], N=D, tm=tm_down, tn=tn,
             out_dtype=F32, name="ffn_down_lo", addend=x1)
    x2 = _mm([dict(a=act, a_blk=1, K=k_half, w=w_down, w_row_blk=1, w_col_blk=0)], N=D, tm=tm_down, tn=tn,
             out_dtype=F32, name="ffn_down_hi", addend=p0)
    return x2


def kernel(x, positions, norm_mix_w, w_in, diff_lambda_q1, diff_lambda_k1, diff_lambda_q2, diff_lambda_k2, diff_subln_w, mla_q_norm_w, mla_w_uq, mla_kv_norm_w, mla_w_ukv, w_o_diff, w_o_mla, w_out, norm_ffn_w, ffn_w_up, ffn_conv_w, ffn_conv_b, ffn_w_down, final_norm_w):
    B, S, D = x.shape
    assert B == 1
    x2d = x.reshape(S, D)
    pos_col = positions.reshape(S, 1)
    for l in range(w_in.shape[0]):
        x2d = _block_forward(
            x2d, pos_col, l, norm_mix_w[l], w_in[l], diff_lambda_q1[l], diff_lambda_k1[l], diff_lambda_q2[l],
            diff_lambda_k2[l], diff_subln_w[l], mla_q_norm_w[l], mla_w_uq[l], mla_kv_norm_w[l], mla_w_ukv[l],
            w_o_diff[l], w_o_mla[l], w_out[l], norm_ffn_w[l], ffn_w_up[l], ffn_conv_w[l], ffn_conv_b[l],
            ffn_w_down[l])
    out = _rmsnorm(x2d, final_norm_w, F32)
    return out.reshape(B, S, D)
```

```python
import functools
import math

import jax
import jax.numpy as jnp
from jax import lax
from jax.experimental import pallas as pl
from jax.experimental.pallas import tpu as pltpu

BF16 = jnp.bfloat16
F32 = jnp.float32

LANES = 128
V7X_VMEM_LIMIT_BYTES = 56 << 20

DIFF_HEADS = 16
DIFF_HEAD_DIM = 64
MLA_HEADS = 16
MLA_NOPE_DIM = 128
MLA_ROPE_DIM = 64
MLA_V_DIM = 128
ROPE_THETA = 10000.0
NORM_EPS = 1e-6
DIFF_SUBLN_EPS = 1e-5
CONV_WIDTH = 3
MASK_VALUE = -1e30
LOG2_E = math.log2(math.e)


def _cparams(n_axes, flags=None):
    return pltpu.CompilerParams(
        dimension_semantics=("arbitrary",) * n_axes,
        vmem_limit_bytes=V7X_VMEM_LIMIT_BYTES,
        flags=flags,
    )


def _rope_table_kernel(pos_ref, freq_ref, sign_ref, cos_ref, sin_ref):
    ang = pos_ref[...].astype(F32) * freq_ref[...]
    cos_ref[...] = jnp.cos(ang)
    sin_ref[...] = jnp.sin(ang) * sign_ref[...]


def _rope_tables(pos_col, tm=2048):
    S = pos_col.shape[0]
    half = MLA_ROPE_DIM // 2
    inv_freq = ROPE_THETA ** (-jnp.arange(0, MLA_ROPE_DIM, 2, dtype=F32) / MLA_ROPE_DIM)
    freq = jnp.tile(inv_freq, LANES // half).reshape(1, LANES)
    sign = jnp.tile(jnp.concatenate([-jnp.ones((half,), F32), jnp.ones((half,), F32)]),
                    LANES // (2 * half)).reshape(1, LANES)
    return pl.pallas_call(
        _rope_table_kernel,
        grid=(S // tm,),
        in_specs=[pl.BlockSpec((tm, 1), lambda i: (i, 0)),
                  pl.BlockSpec((1, LANES), lambda i: (0, 0)),
                  pl.BlockSpec((1, LANES), lambda i: (0, 0))],
        out_specs=[pl.BlockSpec((tm, LANES), lambda i: (i, 0)),
                   pl.BlockSpec((tm, LANES), lambda i: (i, 0))],
        out_shape=[jax.ShapeDtypeStruct((S, LANES), F32)] * 2,
        compiler_params=_cparams(1),
        name="rope_tables",
    )(pos_col, freq, sign)


def _rope_partner(zc):
    lane = lax.broadcasted_iota(jnp.int32, zc.shape, 1)
    first_half = (lane & 32) == 0
    return jnp.where(first_half, pltpu.roll(zc, 96, 1), pltpu.roll(zc, 32, 1))


def _rope_lanes(z, cos, sin):
    outs = []
    for c in range(z.shape[1] // LANES):
        zc = z[:, c * LANES:(c + 1) * LANES]
        outs.append(zc * cos + _rope_partner(zc) * sin)
    return outs[0] if len(outs) == 1 else jnp.concatenate(outs, axis=1)


def _rmsnorm_kernel(x_ref, w_ref, o_ref, *, eps):
    xf = x_ref[...].astype(F32)
    ms = jnp.mean(xf * xf, axis=1, keepdims=True)
    o_ref[...] = (xf * lax.rsqrt(ms + eps) * w_ref[...]).astype(o_ref.dtype)


def _rmsnorm(x, w, out_dtype, eps=NORM_EPS, tm=512):
    M, D = x.shape
    return pl.pallas_call(
        functools.partial(_rmsnorm_kernel, eps=eps),
        grid=(M // tm,),
        in_specs=[pl.BlockSpec((tm, D), lambda i: (i, 0)),
                  pl.BlockSpec((1, D), lambda i: (0, 0))],
        out_specs=pl.BlockSpec((tm, D), lambda i: (i, 0)),
        out_shape=jax.ShapeDtypeStruct((M, D), out_dtype),
        compiler_params=_cparams(1),
        name="rmsnorm",
    )(x, w.reshape(1, D).astype(F32))


def _tile_walk(ni, n_steps):
    def cur(s):
        c = jnp.minimum(s, n_steps - 1)
        return c % ni, c // ni

    def prev(s):
        p = jnp.maximum(s - 1, 0)
        return p % ni, p // ni

    return cur, prev


MM_SUB_ROWS = 128
MXU_DIM = 256


def _mm_kernel(*refs, n_pairs, has_gate, has_norm, has_addend, has_rope, epilogue, epi_arg, eps, ni, n_steps,
               tm, sub, w_transposed):
    refs = list(refs)
    pair_refs = []
    for _ in range(n_pairs):
        a_ref = refs.pop(0)
        w_ref = refs.pop(0)
        g_ref = refs.pop(0) if has_gate else None
        pair_refs.append((a_ref, w_ref, g_ref))
    nw_ref = refs.pop(0) if has_norm else None
    add_ref = refs.pop(0) if has_addend else None
    cos_ref = refs.pop(0) if has_rope else None
    sin_ref = refs.pop(0) if has_rope else None
    o_ref = refs.pop(0)
    wbf_refs = refs[:n_pairs]
    raw_refs = refs[n_pairs:]

    s = pl.program_id(0)
    cur, prev = _tile_walk(ni, n_steps)
    i_cur, _ = cur(s)
    _, j = prev(s)

    @pl.when(s == 0)
    def _():
        for raw in raw_refs:
            raw[...] = jnp.zeros(raw.shape, F32)

    @pl.when(i_cur == 0)
    def _():
        for (_, w_ref, _), wbf in zip(pair_refs, wbf_refs):
            if w_transposed:
                eye = _eye_bf16(MXU_DIM)
                for kc in range(wbf.shape[0] // MXU_DIM):
                    ks = slice(kc * MXU_DIM, (kc + 1) * MXU_DIM)
                    wbf[ks, :] = _nt_dot(eye, w_ref[:, ks].astype(BF16)).astype(BF16)
            else:
                wbf[...] = w_ref[...].astype(BF16)

    def finish_previous(rows, roped):
        if has_gate:
            acc = None
            for (_, _, g_ref), raw in zip(pair_refs, raw_refs):
                d = raw[rows, :] * g_ref[rows, :].astype(F32)
                acc = d if acc is None else acc + d
        else:
            acc = raw_refs[0][rows, :]
        if has_addend:
            acc = acc + add_ref[rows, :]
        if epilogue == "sigmoid":
            out = 1.0 / (1.0 + jnp.exp(-acc))
        elif epilogue == "rope_lt":
            _, n_scaled, scale = epi_arg
            out = acc
            if roped:
                out = _rope_lanes(acc, cos_ref[rows, :], sin_ref[rows, :]) * jnp.where(j < n_scaled, scale, 1.0)
        elif epilogue == "scale_rope_ge":
            out = acc * epi_arg[0]
            if roped:
                out = _rope_lanes(out, cos_ref[rows, :], sin_ref[rows, :])
        elif epilogue == "krope_dup":
            lane = lax.broadcasted_iota(jnp.int32, acc.shape, 1)
            kr = jnp.where(lane < MLA_ROPE_DIM, acc, 0.0)
            r = kr * cos_ref[rows, :] + _rope_partner(kr) * sin_ref[rows, :]
            out = r + pltpu.roll(r, MLA_ROPE_DIM, 1)
        else:
            out = acc
        o_ref[rows, :] = out.astype(o_ref.dtype)

    def multiply_current(rows):
        acc = None
        for p, ((a_ref, _, _), wbf) in enumerate(zip(pair_refs, wbf_refs)):
            a = a_ref[rows, :]
            if has_norm:
                af = a.astype(F32)
                ms = jnp.mean(af * af, axis=1, keepdims=True)
                a = (af * lax.rsqrt(ms + eps) * nw_ref[...]).astype(BF16)
            d = jnp.dot(a, wbf[...], preferred_element_type=F32)
            if has_gate:
                raw_refs[p][rows, :] = d
            else:
                acc = d if acc is None else acc + d
        if not has_gate:
            raw_refs[0][rows, :] = acc

    def step(roped):
        for r in range(tm // sub):
            rows = slice(r * sub, (r + 1) * sub)
            finish_previous(rows, roped)
            multiply_current(rows)

    if epilogue == "rope_lt":
        pl.when(j < epi_arg[0])(lambda: step(True))
        pl.when(j >= epi_arg[0])(lambda: step(False))
    elif epilogue == "scale_rope_ge":
        pl.when(j >= epi_arg[1])(lambda: step(True))
        pl.when(j < epi_arg[1])(lambda: step(False))
    else:
        step(False)


def _mm(pairs, *, N, tm, tn, out_dtype, name, epilogue="none", epi_arg=None,
        norm_w=None, addend=None, rope=None, eps=NORM_EPS, w_transposed=False):
    M = pairs[0]["a"].shape[0]
    ni, nj = M // tm, N // tn
    n_steps = ni * nj
    cur, prev = _tile_walk(ni, n_steps)
    has_gate = pairs[0].get("gate") is not None
    args, in_specs, wbf_scratch = [], [], []

    def at_cur(fn):
        return lambda s: fn(*cur(s))

    def at_prev(fn):
        return lambda s: fn(*prev(s))

    for p in pairs:
        K = p["K"]
        args.append(p["a"])
        in_specs.append(pl.BlockSpec((tm, K), at_cur(functools.partial(lambda i, j, b: (i, b), b=p["a_blk"]))))
        args.append(p["w"])
        if not w_transposed:
            in_specs.append(pl.BlockSpec((K, tn), at_cur(functools.partial(
                lambda i, j, r, c: (r, c + j), r=p["w_row_blk"], c=p["w_col_blk"]))))
        elif "w_elem_off" in p:
            in_specs.append(pl.BlockSpec((pl.Element(tn), pl.Element(K)), at_cur(functools.partial(
                lambda i, j, r, off, k: (pl.multiple_of(off + tn * j, math.gcd(off, tn)), r * k),
                r=p["w_row_blk"], off=p["w_elem_off"], k=K))))
        else:
            in_specs.append(pl.BlockSpec((tn, K), at_cur(functools.partial(
                lambda i, j, r, c: (c + j, r), r=p["w_row_blk"], c=p["w_col_blk"]))))
        if has_gate:
            g, g_off = p["gate"]
            args.append(g)
            in_specs.append(pl.BlockSpec((tm, tn), at_prev(functools.partial(lambda i, j, c: (i, c + j), c=g_off))))
        wbf_scratch.append(pltpu.VMEM((K, tn), BF16))
    if norm_w is not None:
        args.append(norm_w.reshape(1, -1).astype(F32))
        in_specs.append(pl.BlockSpec((1, norm_w.shape[-1]), lambda s: (0, 0)))
    if addend is not None:
        args.append(addend)
        in_specs.append(pl.BlockSpec((tm, tn), at_prev(lambda i, j: (i, j))))
    if rope is not None:
        for t in rope:
            args.append(t)
            in_specs.append(pl.BlockSpec((tm, LANES), at_prev(lambda i, j: (i, 0))))
    raw_scratch = [pltpu.VMEM((tm, tn), F32)] * (len(pairs) if has_gate else 1)
    kern = functools.partial(
        _mm_kernel, n_pairs=len(pairs), has_gate=has_gate, has_norm=norm_w is not None,
        has_addend=addend is not None, has_rope=rope is not None, epilogue=epilogue, epi_arg=epi_arg, eps=eps,
        ni=ni, n_steps=n_steps, tm=tm, sub=min(MM_SUB_ROWS, tm), w_transposed=w_transposed)
    return pl.pallas_call(
        kern,
        grid=(n_steps + 1,),
        in_specs=in_specs,
        out_specs=pl.BlockSpec((tm, tn), at_prev(lambda i, j: (i, j))),
        out_shape=jax.ShapeDtypeStruct((M, N), out_dtype),
        scratch_shapes=wbf_scratch + raw_scratch,
        compiler_params=_cparams(1),
        name=name,
    )(*args)


CONV_HALO = 8


def _ffn_up_kernel(a_ref, wg_ref, wv_ref, cwg_ref, cwv_ref, cbg_ref, cbv_ref, o_ref,
                   wgbf, wvbf, ug_buf, uv_buf, *, tm, ni, n_steps):
    s = pl.program_id(0)
    cur, _ = _tile_walk(ni, n_steps)
    i_cur, _ = cur(s)

    @pl.when(s == 0)
    def _():
        ug_buf[...] = jnp.zeros(ug_buf.shape, F32)
        uv_buf[...] = jnp.zeros(uv_buf.shape, F32)

    @pl.when(i_cur == 0)
    def _():
        wgbf[...] = wg_ref[...].astype(BF16)
        wvbf[...] = wv_ref[...].astype(BF16)

    sub = min(MM_SUB_ROWS, tm)

    def conv(buf, r0, cw_ref, cb_ref):
        lo = CONV_HALO + r0
        out = cb_ref[...] + cw_ref[0:1, :] * buf[lo - 2:lo - 2 + sub, :]
        out = out + cw_ref[1:2, :] * buf[lo - 1:lo - 1 + sub, :]
        return out + cw_ref[2:3, :] * buf[lo:lo + sub, :]

    keep = jnp.where(i_cur == 0, 0.0, 1.0)
    halo_g = ug_buf[tm:tm + CONV_HALO, :] * keep
    halo_v = uv_buf[tm:tm + CONV_HALO, :] * keep

    for r in reversed(range(tm // sub)):
        r0 = r * sub
        g = conv(ug_buf, r0, cwg_ref, cbg_ref)
        v = conv(uv_buf, r0, cwv_ref, cbv_ref)
        o_ref[r0:r0 + sub, :] = (g / (1.0 + jnp.exp(-g)) * v).astype(o_ref.dtype)
        a = a_ref[r0:r0 + sub, :]
        ug_buf[CONV_HALO + r0:CONV_HALO + r0 + sub, :] = jnp.dot(a, wgbf[...], preferred_element_type=F32)
        uv_buf[CONV_HALO + r0:CONV_HALO + r0 + sub, :] = jnp.dot(a, wvbf[...], preferred_element_type=F32)

    ug_buf[0:CONV_HALO, :] = halo_g
    uv_buf[0:CONV_HALO, :] = halo_v


def _ffn_up(h, w_up, conv_w, conv_b, *, tm, tn):
    M, K = h.shape
    d_ff = w_up.shape[1] // 2
    ni, nj = M // tm, d_ff // tn
    n_steps = ni * nj
    cur, prev = _tile_walk(ni, n_steps)
    cb = conv_b.reshape(1, -1)

    def at_cur(fn):
        return lambda s: fn(*cur(s))

    def at_prev(fn):
        return lambda s: fn(*prev(s))

    return pl.pallas_call(
        functools.partial(_ffn_up_kernel, tm=tm, ni=ni, n_steps=n_steps),
        grid=(n_steps + 1,),
        in_specs=[
            pl.BlockSpec((tm, K), at_cur(lambda i, j: (i, 0))),
            pl.BlockSpec((K, tn), at_cur(lambda i, j: (0, j))),
            pl.BlockSpec((K, tn), at_cur(lambda i, j: (0, nj + j))),
            pl.BlockSpec((CONV_WIDTH, tn), at_prev(lambda i, j: (0, j))),
            pl.BlockSpec((CONV_WIDTH, tn), at_prev(lambda i, j: (0, nj + j))),
            pl.BlockSpec((1, tn), at_prev(lambda i, j: (0, j))),
            pl.BlockSpec((1, tn), at_prev(lambda i, j: (0, nj + j))),
        ],
        out_specs=pl.BlockSpec((tm, tn), at_prev(lambda i, j: (i, j))),
        out_shape=jax.ShapeDtypeStruct((M, d_ff), BF16),
        scratch_shapes=[pltpu.VMEM((K, tn), BF16), pltpu.VMEM((K, tn), BF16),
                        pltpu.VMEM((CONV_HALO + tm, tn), F32), pltpu.VMEM((CONV_HALO + tm, tn), F32)],
        compiler_params=_cparams(1),
        name="ffn_up_conv_gate",
    )(h, w_up, w_up, conv_w, conv_w, cb, cb)


DIFF_CHUNK = 256
MLA_CHUNK = 256
DIFF_STAGE_ORDER = "q|sp"
MLA_STAGE_ORDER = "pqs"


def _nt_dot(a, b):
    return lax.dot_general(a, b, (((1,), (1,)), ((), ())), preferred_element_type=F32)


def _eye_bf16(n):
    r = lax.broadcasted_iota(jnp.int32, (n, n), 0)
    c = lax.broadcasted_iota(jnp.int32, (n, n), 1)
    return jnp.where(r == c, 1.0, 0.0).astype(BF16)


def _transpose_bf16(x, eye):
    return _nt_dot(eye, x).astype(BF16)


SUBLANES = 8
REDUCE_WAYS = 8


def _reduce_rows(x, op, final):
    n = x.shape[0]
    groups = [x[r * SUBLANES:(r + 1) * SUBLANES] for r in range(n // SUBLANES)]
    ways = min(REDUCE_WAYS, len(groups))
    parts = groups[:ways]
    for g, blk in enumerate(groups[ways:]):
        parts[g % ways] = op(parts[g % ways], blk)
    while len(parts) > 1:
        parts = [op(parts[i], parts[i + 1]) if i + 1 < len(parts) else parts[i] for i in range(0, len(parts), 2)]
    return final(parts[0], axis=0, keepdims=True)


def _softmax_chunk(load_s, m_ref, c, mask_q0):
    def scores():
        sT = load_s()
        if mask_q0 is not None:
            key = lax.broadcasted_iota(jnp.int32, sT.shape, 0)
            qq = lax.broadcasted_iota(jnp.int32, sT.shape, 1) + mask_q0
            sT = jnp.where(qq >= key, sT, MASK_VALUE)
        return sT

    m_prev = m_ref[c]
    m_new = jnp.maximum(m_prev, _reduce_rows(scores(), jnp.maximum, jnp.max))
    p = jnp.exp2(scores() - m_new)
    alpha = jnp.exp2(m_prev - m_new)
    m_ref[c] = m_new
    return p.astype(BF16), alpha


ONES_ROWS = 16


def _store_vT(vT_ref, v_ref, eye, T):
    dv = v_ref.shape[1]
    for jb in range(vT_ref.shape[0]):
        vT_ref[jb, 0:dv, :] = _transpose_bf16(v_ref[jb * T:(jb + 1) * T, :], eye)
        vT_ref[jb, dv:dv + ONES_ROWS, :] = jnp.ones((ONES_ROWS, T), BF16)


def _causal_attn_loop(qi, k_ref, vT_ref, qT_ref, s_ref, p_ref, a_ref, m_ref, acc_ref, *, T, n_maps, order):
    CW = qT_ref.shape[2]
    per_map = T // CW
    chunks = [(c, (c % per_map) * CW) for c in range(n_maps * per_map)]
    m_ref[...] = jnp.full(m_ref.shape, MASK_VALUE, F32)
    acc_ref[...] = jnp.zeros(acc_ref.shape, F32)

    def qk(blk, slot, c):
        k = k_ref[pl.ds(pl.multiple_of(blk * T, T), T), :]
        s_ref[slot, c] = jnp.dot(k, qT_ref[c], preferred_element_type=F32)

    def softmax(slot, c, q0, diagonal):
        n = q0 + CW if diagonal else T
        p, alpha = _softmax_chunk(lambda: s_ref[slot, c, 0:n, :], m_ref, c, q0 if diagonal else None)
        p_ref[slot, c, 0:n, :] = p
        a_ref[slot, c] = alpha

    def pv(blk, slot, c, q0, diagonal):
        n = q0 + CW if diagonal else T
        acc_ref[c] = acc_ref[c] * a_ref[slot, c] + jnp.dot(
            vT_ref[blk, :, 0:n], p_ref[slot, c, 0:n, :], preferred_element_type=F32)

    def step(t, slot):
        stage = {"q": lambda c, q0: qk(t, slot, c),
                 "s": lambda c, q0: softmax(1 - slot, c, q0, False),
                 "p": lambda c, q0: pv(t - 2, slot, c, q0, False)}
        for group in order.split("|"):
            for c, q0 in chunks:
                for name in group:
                    stage[name](c, q0)

    def drain(slot):
        for c, q0 in chunks:
            softmax(slot, c, q0, True)
            pv(qi, slot, c, q0, True)

    for c, _ in chunks:
        qk(0, 0, c)

    @pl.when(qi == 0)
    def _():
        drain(0)

    @pl.when(qi >= 1)
    def _():
        for c, q0 in chunks:
            qk(1, 1, c)
            softmax(0, c, q0, False)

    def body(u, carry):
        t = 2 + 2 * u
        step(t, 0)
        step(t + 1, 1)
        return carry

    lax.fori_loop(0, lax.shift_right_arithmetic(qi - 1, 1), body, 0)

    @pl.when(jnp.logical_and(qi >= 2, qi % 2 == 0))
    def _():
        step(qi, 0)
        for c, q0 in chunks:
            pv(qi - 1, 1, c, q0, False)
        drain(0)

    @pl.when(qi % 2 == 1)
    def _():
        for c, q0 in chunks:
            pv(qi - 1, 0, c, q0, False)
        drain(1)


def _diff_attn_kernel(q_ref, k_ref, v_ref, lq1_ref, lk1_ref, lq2_ref, lk2_ref, sw_ref, o_ref,
                      vT_ref, qT_ref, s_ref, p_ref, a_ref, m_ref, acc_ref, *, T, lam_init):
    eye = _eye_bf16(LANES)
    dv = v_ref.shape[1]
    CW = qT_ref.shape[2]
    per_map = T // CW
    _store_vT(vT_ref, v_ref, eye, T)
    lam = (jnp.exp(jnp.sum(lq1_ref[...] * lk1_ref[...], axis=1, keepdims=True))
           - jnp.exp(jnp.sum(lq2_ref[...] * lk2_ref[...], axis=1, keepdims=True)) + lam_init)

    def q_block(qi, carry):
        row0 = pl.multiple_of(qi * T, T)
        q = q_ref[pl.ds(row0, T), :]
        lane = lax.broadcasted_iota(jnp.int32, q.shape, 1)
        zero = jnp.zeros_like(q)
        for half, keep in enumerate((lane < DIFF_HEAD_DIM, lane >= DIFF_HEAD_DIM)):
            qh = jnp.where(keep, q, zero)
            for c in range(per_map):
                qT_ref[half * per_map + c] = _transpose_bf16(qh[c * CW:(c + 1) * CW, :], eye)

        _causal_attn_loop(qi, k_ref, vT_ref, qT_ref, s_ref, p_ref, a_ref, m_ref, acc_ref, T=T, n_maps=2,
                          order=DIFF_STAGE_ORDER)

        for c in range(per_map):
            o1 = acc_ref[c, 0:dv, :] / acc_ref[c, dv:dv + 1, :]
            o2 = acc_ref[per_map + c, 0:dv, :] / acc_ref[per_map + c, dv:dv + 1, :]
            odT = o1 - lam * o2
            ms = jnp.mean(odT * odT, axis=0, keepdims=True)
            outT = (odT * lax.rsqrt(ms + DIFF_SUBLN_EPS) * sw_ref[...] * (1.0 - lam_init)).astype(BF16)
            for r in range(CW // LANES):
                r0 = pl.multiple_of(row0 + c * CW + r * LANES, LANES)
                o_ref[pl.ds(r0, LANES), :] = _transpose_bf16(
                    outT[:, r * LANES:(r + 1) * LANES], eye).astype(o_ref.dtype)
        return carry

    lax.fori_loop(0, q_ref.shape[0] // T, q_block, 0)


def _diff_attn(zqkv, lq1, lk1, lq2, lk2, subln_w, *, lam_init, T):
    S = zqkv.shape[0]
    H = DIFF_HEADS
    hd = 2 * DIFF_HEAD_DIM
    cw = min(DIFF_CHUNK, T)
    nc = 2 * (T // cw)
    vec = lambda a: a.reshape(1, -1).astype(F32)
    small = lambda n: pl.BlockSpec((1, n), lambda h: (0, 0))
    return pl.pallas_call(
        functools.partial(_diff_attn_kernel, T=T, lam_init=lam_init),
        grid=(H,),
        in_specs=[
            pl.BlockSpec((S, hd), lambda h: (0, h)),
            pl.BlockSpec((S, hd), lambda h: (0, H + h)),
            pl.BlockSpec((S, hd), lambda h: (0, 2 * H + h)),
            small(DIFF_HEAD_DIM), small(DIFF_HEAD_DIM), small(DIFF_HEAD_DIM), small(DIFF_HEAD_DIM),
            pl.BlockSpec((hd, 1), lambda h: (0, 0)),
        ],
        out_specs=pl.BlockSpec((S, hd), lambda h: (0, h)),
        out_shape=jax.ShapeDtypeStruct((S, H * hd), BF16),
        scratch_shapes=[pltpu.VMEM((S // T, hd + ONES_ROWS, T), BF16), pltpu.VMEM((nc, hd, cw), BF16),
                        pltpu.VMEM((2, nc, T, cw), F32), pltpu.VMEM((2, nc, T, cw), BF16),
                        pltpu.VMEM((2, nc, 1, cw), F32), pltpu.VMEM((nc, 1, cw), F32),
                        pltpu.VMEM((nc, hd + ONES_ROWS, cw), F32)],
        compiler_params=_cparams(1),
        name="diff_attention",
    )(zqkv, zqkv, zqkv, vec(lq1), vec(lk1), vec(lq2), vec(lk2), subln_w.reshape(-1, 1).astype(F32))


def _mla_attn_kernel(qn_ref, qr_ref, kn_ref, kr_ref, v_ref, o_ref, kcat, vT_ref, qT_ref, s_ref, p_ref, a_ref,
                     m_ref, acc_ref, *, T):
    h = pl.program_id(0)
    eye = _eye_bf16(LANES)
    dv = v_ref.shape[1]
    CW = qT_ref.shape[2]

    kcat[:, 0:MLA_NOPE_DIM] = kn_ref[...]
    kcat[:, MLA_NOPE_DIM:] = kr_ref[...]
    _store_vT(vT_ref, v_ref, eye, T)

    def q_block(qi, carry):
        row0 = pl.multiple_of(qi * T, T)
        qr = qr_ref[pl.ds(row0, T), :]
        lane = lax.broadcasted_iota(jnp.int32, qr.shape, 1)
        lo = (h % 2) * MLA_ROPE_DIM
        mine = jnp.logical_and(lane >= lo, lane < lo + MLA_ROPE_DIM)
        qn = qn_ref[pl.ds(row0, T), :]
        qrm = jnp.where(mine, qr, jnp.zeros_like(qr))
        for c in range(T // CW):
            qT_ref[c, 0:MLA_NOPE_DIM, :] = _transpose_bf16(qn[c * CW:(c + 1) * CW, :], eye)
            qT_ref[c, MLA_NOPE_DIM:, :] = _transpose_bf16(qrm[c * CW:(c + 1) * CW, :], eye)

        _causal_attn_loop(qi, kcat, vT_ref, qT_ref, s_ref, p_ref, a_ref, m_ref, acc_ref, T=T, n_maps=1,
                          order=MLA_STAGE_ORDER)

        for c in range(T // CW):
            oT = (acc_ref[c, 0:dv, :] / acc_ref[c, dv:dv + 1, :]).astype(BF16)
            for r in range(CW // LANES):
                r0 = pl.multiple_of(row0 + c * CW + r * LANES, LANES)
                o_ref[pl.ds(r0, LANES), :] = _transpose_bf16(
                    oT[:, r * LANES:(r + 1) * LANES], eye).astype(o_ref.dtype)
        return carry

    lax.fori_loop(0, qn_ref.shape[0] // T, q_block, 0)


def _mla_attn(qm, kv, kr_dup, *, T):
    S = qm.shape[0]
    H = MLA_HEADS
    cw = min(MLA_CHUNK, T)
    nc = T // cw
    return pl.pallas_call(
        functools.partial(_mla_attn_kernel, T=T),
        grid=(H,),
        in_specs=[
            pl.BlockSpec((S, MLA_NOPE_DIM), lambda h: (0, h)),
            pl.BlockSpec((S, LANES), lambda h: (0, H + h // 2)),
            pl.BlockSpec((S, MLA_NOPE_DIM), lambda h: (0, 2 * h)),
            pl.BlockSpec((S, LANES), lambda h: (0, 0)),
            pl.BlockSpec((S, MLA_V_DIM), lambda h: (0, 2 * h + 1)),
        ],
        out_specs=pl.BlockSpec((S, MLA_V_DIM), lambda h: (0, h)),
        out_shape=jax.ShapeDtypeStruct((S, H * MLA_V_DIM), BF16),
        scratch_shapes=[pltpu.VMEM((S, MLA_NOPE_DIM + LANES), BF16),
                        pltpu.VMEM((S // T, MLA_V_DIM + ONES_ROWS, T), BF16),
                        pltpu.VMEM((nc, MLA_NOPE_DIM + LANES, cw), BF16),
                        pltpu.VMEM((2, nc, T, cw), F32), pltpu.VMEM((2, nc, T, cw), BF16),
                        pltpu.VMEM((2, nc, 1, cw), F32), pltpu.VMEM((nc, 1, cw), F32),
                        pltpu.VMEM((nc, MLA_V_DIM + ONES_ROWS, cw), F32)],
        compiler_params=_cparams(1),
        name="mla_attention",
    )(qm, qm, kv, kr_dup, kv)


def _block_forward(x2d, pos_col, l, norm_mix_w, w_in, lq1, lk1, lq2, lk2, subln_w, q_norm_w, w_uq,
                   kv_norm_w, w_ukv, w_o_diff, w_o_mla, w_out, norm_ffn_w, w_up, conv_w, conv_b, w_down,
                   *, tm=1024, tn=512, t_attn=512, tn_ffn=256, tm_down=512, tm_wide=512, tn_wide=1024,
                   tm_lat=2048, tn_lat=1024):
    S, D = x2d.shape
    H = DIFF_HEADS
    qkv_w = 3 * H * 2 * DIFF_HEAD_DIM
    q_rank = w_uq.shape[0]
    kv_rank = w_ukv.shape[0]
    lat_w = q_rank + kv_rank
    main_w = qkv_w + lat_w
    gate_start = main_w + MLA_ROPE_DIM
    lam_init = 0.8 - 0.6 * math.exp(-0.3 * l)

    cos, sin = _rope_tables(pos_col)
    h = _rmsnorm(x2d, norm_mix_w, BF16)

    w_in_t = jnp.swapaxes(w_in, 0, 1)
    z = _mm([dict(a=h, a_blk=0, K=D, w=w_in_t, w_row_blk=0, w_col_blk=0)], N=main_w, tm=tm, tn=tn,
            out_dtype=BF16, name="in_proj_main", epilogue="rope_lt",
            epi_arg=((2 * H * 2 * DIFF_HEAD_DIM) // tn, (H * 2 * DIFF_HEAD_DIM) // tn,
                     DIFF_HEAD_DIM ** -0.5 * LOG2_E), rope=(cos, sin), w_transposed=True)
    kr_dup = _mm([dict(a=h, a_blk=0, K=D, w=w_in_t, w_row_blk=0, w_col_blk=main_w // LANES)], N=LANES,
                 tm=tm, tn=LANES, out_dtype=BF16, name="in_proj_krope", epilogue="krope_dup",
                 rope=(cos, sin), w_transposed=True)
    gates = _mm([dict(a=h, a_blk=0, K=D, w=w_in_t, w_row_blk=0, w_elem_off=gate_start)], N=2 * D,
                tm=tm_wide, tn=tn_wide, out_dtype=BF16, name="in_proj_gates", epilogue="sigmoid",
                w_transposed=True)

    o_d = _diff_attn(z, lq1, lk1, lq2, lk2, subln_w, lam_init=lam_init, T=t_attn)

    qk_dim = MLA_NOPE_DIM + MLA_ROPE_DIM
    w_uq3 = w_uq.reshape(q_rank, MLA_HEADS, qk_dim)
    w_uq_perm = jnp.concatenate([w_uq3[:, :, :MLA_NOPE_DIM].reshape(q_rank, -1),
                                 w_uq3[:, :, MLA_NOPE_DIM:].reshape(q_rank, -1)], axis=1)
    qm = _mm([dict(a=z, a_blk=qkv_w // q_rank, K=q_rank, w=w_uq_perm, w_row_blk=0, w_col_blk=0)],
             N=MLA_HEADS * qk_dim, tm=tm_lat, tn=tn_lat, out_dtype=BF16, name="mla_q_up", norm_w=q_norm_w,
             epilogue="scale_rope_ge",
             epi_arg=(qk_dim ** -0.5 * LOG2_E, (MLA_HEADS * MLA_NOPE_DIM) // tn_lat), rope=(cos, sin))
    kv = _mm([dict(a=z, a_blk=(qkv_w + q_rank) // kv_rank, K=kv_rank, w=w_ukv, w_row_blk=0, w_col_blk=0)],
             N=w_ukv.shape[1], tm=tm_lat, tn=tn_lat, out_dtype=BF16, name="mla_kv_up", norm_w=kv_norm_w)
    o_m = _mla_attn(qm, kv, kr_dup, T=t_attn)

    y = _mm([dict(a=o_d, a_blk=0, K=o_d.shape[1], w=w_o_diff, w_row_blk=0, w_col_blk=0, gate=(gates, 0)),
             dict(a=o_m, a_blk=0, K=o_m.shape[1], w=w_o_mla, w_row_blk=0, w_col_blk=0, gate=(gates, D // tn))],
            N=D, tm=tm, tn=tn, out_dtype=BF16, name="branch_merge")
    x1 = _mm([dict(a=y, a_blk=0, K=D, w=w_out, w_row_blk=0, w_col_blk=0)], N=D, tm=tm, tn=tn,
             out_dtype=F32, name="out_proj", addend=x2d)

    h2 = _rmsnorm(x1, norm_ffn_w, BF16)
    act = _ffn_up(h2, w_up, conv_w, conv_b, tm=tm, tn=tn_ffn)
    d_ff = act.shape[1]
    k_half = d_ff // 2
    p0 = _mm([dict(a=act, a_blk=0, K=k_half, w=w_down, w_row_blk=0, w_col_blk=0)], N=D, tm=tm_down, tn=tn,
             out_dtype=F32, name="ffn_down_lo", addend=x1)
    x2 = _mm([dict(a=act, a_blk=1, K=k_half, w=w_down, w_row_blk=1, w_col_blk=0)], N=D, tm=tm_down, tn=tn,
             out_dtype=F32, name="ffn_down_hi", addend=p0)
    return x2


def kernel(x, positions, norm_mix_w, w_in, diff_lambda_q1, diff_lambda_k1, diff_lambda_q2, diff_lambda_k2, diff_subln_w, mla_q_norm_w, mla_w_uq, mla_kv_norm_w, mla_w_ukv, w_o_diff, w_o_mla, w_out, norm_ffn_w, ffn_w_up, ffn_conv_w, ffn_conv_b, ffn_w_down, final_norm_w):
    B, S, D = x.shape
    assert B == 1
    x2d = x.reshape(S, D)
    pos_col = positions.reshape(S, 1)
    for l in range(w_in.shape[0]):
        x2d = _block_forward(
            x2d, pos_col, l, norm_mix_w[l], w_in[l], diff_lambda_q1[l], diff_lambda_k1[l], diff_lambda_q2[l],
            diff_lambda_k2[l], diff_subln_w[l], mla_q_norm_w[l], mla_w_uq[l], mla_kv_norm_w[l], mla_w_ukv[l],
            w_o_diff[l], w_o_mla[l], w_out[l], norm_ffn_w[l], ffn_w_up[l], ffn_conv_w[l], ffn_conv_b[l],
            ffn_w_down[l])
    out = _rmsnorm(x2d, final_norm_w, F32)
    return out.reshape(B, S, D)
```

```python
import functools
import math

import jax
import jax.numpy as jnp
from jax import lax
from jax.experimental import pallas as pl
from jax.experimental.pallas import tpu as pltpu

BF16 = jnp.bfloat16
F32 = jnp.float32

LANES = 128
V7X_VMEM_LIMIT_BYTES = 56 << 20
V7X_VMEM_LIMIT_MAX_BYTES = 62 << 20

DIFF_HEADS = 16
DIFF_HEAD_DIM = 64
MLA_HEADS = 16
MLA_NOPE_DIM = 128
MLA_ROPE_DIM = 64
MLA_V_DIM = 128
ROPE_THETA = 10000.0
NORM_EPS = 1e-6
DIFF_SUBLN_EPS = 1e-5
CONV_WIDTH = 3
MASK_VALUE = -1e30
LOG2_E = math.log2(math.e)


def _cparams(n_axes, vmem_limit_bytes=V7X_VMEM_LIMIT_BYTES):
    return pltpu.CompilerParams(
        dimension_semantics=("arbitrary",) * n_axes,
        vmem_limit_bytes=vmem_limit_bytes,
    )


def _rope_table_kernel(pos_ref, freq_ref, sign_ref, cos_ref, sin_ref):
    ang = pos_ref[...].astype(F32) * freq_ref[...]
    cos_ref[...] = jnp.cos(ang)
    sin_ref[...] = jnp.sin(ang) * sign_ref[...]


def _rope_tables(pos_col, tm=2048):
    S = pos_col.shape[0]
    half = MLA_ROPE_DIM // 2
    inv_freq = ROPE_THETA ** (-jnp.arange(0, MLA_ROPE_DIM, 2, dtype=F32) / MLA_ROPE_DIM)
    freq = jnp.tile(inv_freq, LANES // half).reshape(1, LANES)
    sign = jnp.tile(jnp.concatenate([-jnp.ones((half,), F32), jnp.ones((half,), F32)]),
                    LANES // (2 * half)).reshape(1, LANES)
    return pl.pallas_call(
        _rope_table_kernel,
        grid=(S // tm,),
        in_specs=[pl.BlockSpec((tm, 1), lambda i: (i, 0)),
                  pl.BlockSpec((1, LANES), lambda i: (0, 0)),
                  pl.BlockSpec((1, LANES), lambda i: (0, 0))],
        out_specs=[pl.BlockSpec((tm, LANES), lambda i: (i, 0)),
                   pl.BlockSpec((tm, LANES), lambda i: (i, 0))],
        out_shape=[jax.ShapeDtypeStruct((S, LANES), F32)] * 2,
        compiler_params=_cparams(1),
        name="rope_tables",
    )(pos_col, freq, sign)


def _rope_partner(zc):
    lane = lax.broadcasted_iota(jnp.int32, zc.shape, 1)
    first_half = (lane & 32) == 0
    return jnp.where(first_half, pltpu.roll(zc, 96, 1), pltpu.roll(zc, 32, 1))


def _rope_lanes(z, cos, sin):
    outs = []
    for c in range(z.shape[1] // LANES):
        zc = z[:, c * LANES:(c + 1) * LANES]
        outs.append(zc * cos + _rope_partner(zc) * sin)
    return outs[0] if len(outs) == 1 else jnp.concatenate(outs, axis=1)


def _rmsnorm_kernel(x_ref, w_ref, o_ref, *, eps):
    xf = x_ref[...].astype(F32)
    ms = jnp.mean(xf * xf, axis=1, keepdims=True)
    o_ref[...] = (xf * lax.rsqrt(ms + eps) * w_ref[...]).astype(o_ref.dtype)


def _rmsnorm(x, w, out_dtype, eps=NORM_EPS, tm=512):
    M, D = x.shape
    return pl.pallas_call(
        functools.partial(_rmsnorm_kernel, eps=eps),
        grid=(M // tm,),
        in_specs=[pl.BlockSpec((tm, D), lambda i: (i, 0)),
                  pl.BlockSpec((1, D), lambda i: (0, 0))],
        out_specs=pl.BlockSpec((tm, D), lambda i: (i, 0)),
        out_shape=jax.ShapeDtypeStruct((M, D), out_dtype),
        compiler_params=_cparams(1),
        name="rmsnorm",
    )(x, w.reshape(1, D).astype(F32))


def _tile_walk(ni, n_steps):
    def cur(s):
        c = jnp.minimum(s, n_steps - 1)
        return c % ni, c // ni

    def prev(s):
        p = jnp.maximum(s - 1, 0)
        return p % ni, p // ni

    return cur, prev


MM_SUB_ROWS = 256
MXU_DIM = 256


def _mm_kernel(*refs, n_pairs, has_gate, has_norm, has_addend, has_rope, epilogue, epi_arg, eps, ni, n_steps,
               tm, sub, w_transposed):
    refs = list(refs)
    pair_refs = []
    for _ in range(n_pairs):
        a_ref = refs.pop(0)
        w_ref = refs.pop(0)
        g_ref = refs.pop(0) if has_gate else None
        pair_refs.append((a_ref, w_ref, g_ref))
    nw_ref = refs.pop(0) if has_norm else None
    add_ref = refs.pop(0) if has_addend else None
    cos_ref = refs.pop(0) if has_rope else None
    sin_ref = refs.pop(0) if has_rope else None
    o_ref = refs.pop(0)
    wbf_refs = refs[:n_pairs]
    raw_refs = refs[n_pairs:]

    s = pl.program_id(0)
    cur, prev = _tile_walk(ni, n_steps)
    i_cur, _ = cur(s)
    _, j = prev(s)

    @pl.when(s == 0)
    def _():
        for raw in raw_refs:
            raw[...] = jnp.zeros(raw.shape, F32)

    @pl.when(i_cur == 0)
    def _():
        for (_, w_ref, _), wbf in zip(pair_refs, wbf_refs):
            if w_transposed:
                eye = _eye_bf16(MXU_DIM)
                for kc in range(wbf.shape[0] // MXU_DIM):
                    ks = slice(kc * MXU_DIM, (kc + 1) * MXU_DIM)
                    wbf[ks, :] = _nt_dot(eye, w_ref[:, ks].astype(BF16)).astype(BF16)
            else:
                wbf[...] = w_ref[...].astype(BF16)

    def finish_previous(rows, roped):
        if has_gate:
            acc = None
            for (_, _, g_ref), raw in zip(pair_refs, raw_refs):
                d = raw[rows, :] * g_ref[rows, :].astype(F32)
                acc = d if acc is None else acc + d
        else:
            acc = raw_refs[0][rows, :]
        if has_addend:
            acc = acc + add_ref[rows, :]
        if epilogue == "sigmoid":
            out = 1.0 / (1.0 + jnp.exp(-acc))
        elif epilogue == "rope_lt":
            _, n_scaled, scale = epi_arg
            out = acc
            if roped:
                out = _rope_lanes(acc, cos_ref[rows, :], sin_ref[rows, :]) * jnp.where(j < n_scaled, scale, 1.0)
        elif epilogue == "scale_rope_ge":
            out = acc * epi_arg[0]
            if roped:
                out = _rope_lanes(out, cos_ref[rows, :], sin_ref[rows, :])
        elif epilogue == "krope_dup":
            lane = lax.broadcasted_iota(jnp.int32, acc.shape, 1)
            kr = jnp.where(lane < MLA_ROPE_DIM, acc, 0.0)
            r = kr * cos_ref[rows, :] + _rope_partner(kr) * sin_ref[rows, :]
            out = r + pltpu.roll(r, MLA_ROPE_DIM, 1)
        else:
            out = acc
        o_ref[rows, :] = out.astype(o_ref.dtype)

    def multiply_current(rows):
        acc = None
        for p, ((a_ref, _, _), wbf) in enumerate(zip(pair_refs, wbf_refs)):
            a = a_ref[rows, :]
            if has_norm:
                af = a.astype(F32)
                ms = jnp.mean(af * af, axis=1, keepdims=True)
                a = (af * lax.rsqrt(ms + eps) * nw_ref[...]).astype(BF16)
            d = jnp.dot(a, wbf[...], preferred_element_type=F32)
            if has_gate:
                raw_refs[p][rows, :] = d
            else:
                acc = d if acc is None else acc + d
        if not has_gate:
            raw_refs[0][rows, :] = acc

    def step(roped):
        for r in range(tm // sub):
            rows = slice(r * sub, (r + 1) * sub)
            finish_previous(rows, roped)
            multiply_current(rows)

    if epilogue == "rope_lt":
        pl.when(j < epi_arg[0])(lambda: step(True))
        pl.when(j >= epi_arg[0])(lambda: step(False))
    elif epilogue == "scale_rope_ge":
        pl.when(j >= epi_arg[1])(lambda: step(True))
        pl.when(j < epi_arg[1])(lambda: step(False))
    else:
        step(False)


def _mm(pairs, *, N, tm, tn, out_dtype, name, epilogue="none", epi_arg=None,
        norm_w=None, addend=None, rope=None, eps=NORM_EPS, w_transposed=False):
    M = pairs[0]["a"].shape[0]
    ni, nj = M // tm, N // tn
    n_steps = ni * nj
    cur, prev = _tile_walk(ni, n_steps)
    has_gate = pairs[0].get("gate") is not None
    args, in_specs, wbf_scratch = [], [], []

    def at_cur(fn):
        return lambda s: fn(*cur(s))

    def at_prev(fn):
        return lambda s: fn(*prev(s))

    for p in pairs:
        K = p["K"]
        args.append(p["a"])
        in_specs.append(pl.BlockSpec((tm, K), at_cur(functools.partial(lambda i, j, b: (i, b), b=p["a_blk"]))))
        args.append(p["w"])
        if not w_transposed:
            in_specs.append(pl.BlockSpec((K, tn), at_cur(functools.partial(
                lambda i, j, r, c: (r, c + j), r=p["w_row_blk"], c=p["w_col_blk"]))))
        elif "w_elem_off" in p:
            in_specs.append(pl.BlockSpec((pl.Element(tn), pl.Element(K)), at_cur(functools.partial(
                lambda i, j, r, off, k: (pl.multiple_of(off + tn * j, math.gcd(off, tn)), r * k),
                r=p["w_row_blk"], off=p["w_elem_off"], k=K))))
        else:
            in_specs.append(pl.BlockSpec((tn, K), at_cur(functools.partial(
                lambda i, j, r, c: (c + j, r), r=p["w_row_blk"], c=p["w_col_blk"]))))
        if has_gate:
            g, g_off = p["gate"]
            args.append(g)
            in_specs.append(pl.BlockSpec((tm, tn), at_prev(functools.partial(lambda i, j, c: (i, c + j), c=g_off))))
        wbf_scratch.append(pltpu.VMEM((K, tn), BF16))
    if norm_w is not None:
        args.append(norm_w.reshape(1, -1).astype(F32))
        in_specs.append(pl.BlockSpec((1, norm_w.shape[-1]), lambda s: (0, 0)))
    if addend is not None:
        args.append(addend)
        in_specs.append(pl.BlockSpec((tm, tn), at_prev(lambda i, j: (i, j))))
    if rope is not None:
        for t in rope:
            args.append(t)
            in_specs.append(pl.BlockSpec((tm, LANES), at_prev(lambda i, j: (i, 0))))
    raw_scratch = [pltpu.VMEM((tm, tn), F32)] * (len(pairs) if has_gate else 1)
    kern = functools.partial(
        _mm_kernel, n_pairs=len(pairs), has_gate=has_gate, has_norm=norm_w is not None,
        has_addend=addend is not None, has_rope=rope is not None, epilogue=epilogue, epi_arg=epi_arg, eps=eps,
        ni=ni, n_steps=n_steps, tm=tm, sub=min(MM_SUB_ROWS, tm), w_transposed=w_transposed)
    return pl.pallas_call(
        kern,
        grid=(n_steps + 1,),
        in_specs=in_specs,
        out_specs=pl.BlockSpec((tm, tn), at_prev(lambda i, j: (i, j))),
        out_shape=jax.ShapeDtypeStruct((M, N), out_dtype),
        scratch_shapes=wbf_scratch + raw_scratch,
        compiler_params=_cparams(1),
        name=name,
    )(*args)


CONV_HALO = 8


def _ffn_up_kernel(a_ref, wg_ref, wv_ref, cwg_ref, cwv_ref, cbg_ref, cbv_ref, o_ref,
                   wgbf, wvbf, ug_buf, uv_buf, *, tm, ni, n_steps):
    s = pl.program_id(0)
    cur, _ = _tile_walk(ni, n_steps)
    i_cur, _ = cur(s)

    @pl.when(s == 0)
    def _():
        ug_buf[...] = jnp.zeros(ug_buf.shape, F32)
        uv_buf[...] = jnp.zeros(uv_buf.shape, F32)

    @pl.when(i_cur == 0)
    def _():
        wgbf[...] = wg_ref[...].astype(BF16)
        wvbf[...] = wv_ref[...].astype(BF16)

    sub = min(MM_SUB_ROWS, tm)

    def conv(buf, r0, cw_ref, cb_ref):
        lo = CONV_HALO + r0
        out = cb_ref[...] + cw_ref[0:1, :] * buf[lo - 2:lo - 2 + sub, :]
        out = out + cw_ref[1:2, :] * buf[lo - 1:lo - 1 + sub, :]
        return out + cw_ref[2:3, :] * buf[lo:lo + sub, :]

    keep = jnp.where(i_cur == 0, 0.0, 1.0)
    halo_g = ug_buf[tm:tm + CONV_HALO, :] * keep
    halo_v = uv_buf[tm:tm + CONV_HALO, :] * keep

    for r in reversed(range(tm // sub)):
        r0 = r * sub
        g = conv(ug_buf, r0, cwg_ref, cbg_ref)
        v = conv(uv_buf, r0, cwv_ref, cbv_ref)
        o_ref[r0:r0 + sub, :] = (g / (1.0 + jnp.exp(-g)) * v).astype(o_ref.dtype)
        a = a_ref[r0:r0 + sub, :]
        ug_buf[CONV_HALO + r0:CONV_HALO + r0 + sub, :] = jnp.dot(a, wgbf[...], preferred_element_type=F32)
        uv_buf[CONV_HALO + r0:CONV_HALO + r0 + sub, :] = jnp.dot(a, wvbf[...], preferred_element_type=F32)

    ug_buf[0:CONV_HALO, :] = halo_g
    uv_buf[0:CONV_HALO, :] = halo_v


def _ffn_up(h, w_up, conv_w, conv_b, *, tm, tn):
    M, K = h.shape
    d_ff = w_up.shape[1] // 2
    ni, nj = M // tm, d_ff // tn
    n_steps = ni * nj
    cur, prev = _tile_walk(ni, n_steps)
    cb = conv_b.reshape(1, -1)

    def at_cur(fn):
        return lambda s: fn(*cur(s))

    def at_prev(fn):
        return lambda s: fn(*prev(s))

    return pl.pallas_call(
        functools.partial(_ffn_up_kernel, tm=tm, ni=ni, n_steps=n_steps),
        grid=(n_steps + 1,),
        in_specs=[
            pl.BlockSpec((tm, K), at_cur(lambda i, j: (i, 0))),
            pl.BlockSpec((K, tn), at_cur(lambda i, j: (0, j))),
            pl.BlockSpec((K, tn), at_cur(lambda i, j: (0, nj + j))),
            pl.BlockSpec((CONV_WIDTH, tn), at_prev(lambda i, j: (0, j))),
            pl.BlockSpec((CONV_WIDTH, tn), at_prev(lambda i, j: (0, nj + j))),
            pl.BlockSpec((1, tn), at_prev(lambda i, j: (0, j))),
            pl.BlockSpec((1, tn), at_prev(lambda i, j: (0, nj + j))),
        ],
        out_specs=pl.BlockSpec((tm, tn), at_prev(lambda i, j: (i, j))),
        out_shape=jax.ShapeDtypeStruct((M, d_ff), BF16),
        scratch_shapes=[pltpu.VMEM((K, tn), BF16), pltpu.VMEM((K, tn), BF16),
                        pltpu.VMEM((CONV_HALO + tm, tn), F32), pltpu.VMEM((CONV_HALO + tm, tn), F32)],
        compiler_params=_cparams(1),
        name="ffn_up_conv_gate",
    )(h, w_up, w_up, conv_w, conv_w, cb, cb)


DIFF_CHUNK = 256
MLA_CHUNK = 256
DIFF_STAGE_ORDER = "q|sp"
MLA_STAGE_ORDER = "q|sp"


def _nt_dot(a, b):
    return lax.dot_general(a, b, (((1,), (1,)), ((), ())), preferred_element_type=F32)


def _eye_bf16(n):
    r = lax.broadcasted_iota(jnp.int32, (n, n), 0)
    c = lax.broadcasted_iota(jnp.int32, (n, n), 1)
    return jnp.where(r == c, 1.0, 0.0).astype(BF16)


def _transpose_bf16(x, eye):
    return _nt_dot(eye, x).astype(BF16)


SUBLANES = 8
REDUCE_WAYS = 8


def _reduce_rows(x, op, final):
    n = x.shape[0]
    groups = [x[r * SUBLANES:(r + 1) * SUBLANES] for r in range(n // SUBLANES)]
    ways = min(REDUCE_WAYS, len(groups))
    parts = groups[:ways]
    for g, blk in enumerate(groups[ways:]):
        parts[g % ways] = op(parts[g % ways], blk)
    while len(parts) > 1:
        parts = [op(parts[i], parts[i + 1]) if i + 1 < len(parts) else parts[i] for i in range(0, len(parts), 2)]
    return final(parts[0], axis=0, keepdims=True)


def _softmax_chunk(load_s, m_ref, c, mask_q0):
    def scores():
        sT = load_s()
        if mask_q0 is not None:
            key = lax.broadcasted_iota(jnp.int32, sT.shape, 0)
            qq = lax.broadcasted_iota(jnp.int32, sT.shape, 1) + mask_q0
            sT = jnp.where(qq >= key, sT, MASK_VALUE)
        return sT

    m_prev = m_ref[c]
    m_new = jnp.maximum(m_prev, _reduce_rows(scores(), jnp.maximum, jnp.max))
    p = jnp.exp2(scores() - m_new)
    alpha = jnp.exp2(m_prev - m_new)
    m_ref[c] = m_new
    return p.astype(BF16), alpha


ONES_ROWS = 16


def _causal_attn_loop(qi, kv_refs, qT_ref, s_ref, p_ref, a_ref, m_ref, acc_ref, *, T, n_maps, order):
    CW = qT_ref.shape[2]
    per_map = T // CW
    chunks = [(c, (c % per_map) * CW) for c in range(n_maps * per_map)]
    m_ref[...] = jnp.full(m_ref.shape, MASK_VALUE, F32)
    acc_ref[...] = jnp.zeros(acc_ref.shape, F32)

    maps_per_kv = n_maps // len(kv_refs)

    def kv_of(c):
        return kv_refs[c // (per_map * maps_per_kv)]

    def qk(blk, slot, c):
        k = kv_of(c)[0](pl.multiple_of(blk * T, T))
        s_ref[slot, c] = jnp.dot(k, qT_ref[c], preferred_element_type=F32)

    def softmax(slot, c, q0, diagonal):
        n = q0 + CW if diagonal else T
        p, alpha = _softmax_chunk(lambda: s_ref[slot, c, 0:n, :], m_ref, c, q0 if diagonal else None)
        p_ref[slot, c, 0:n, :] = p
        a_ref[slot, c] = alpha

    def pv(blk, slot, c, q0, diagonal):
        n = q0 + CW if diagonal else T
        acc_ref[c] = acc_ref[c] * a_ref[slot, c] + jnp.dot(
            kv_of(c)[1][blk, :, 0:n], p_ref[slot, c, 0:n, :], preferred_element_type=F32)

    def step(t, slot):
        stage = {"q": lambda c, q0: qk(t, slot, c),
                 "s": lambda c, q0: softmax(1 - slot, c, q0, False),
                 "p": lambda c, q0: pv(t - 2, slot, c, q0, False)}
        for group in order.split("|"):
            for c, q0 in chunks:
                for name in group:
                    stage[name](c, q0)

    def drain(slot):
        for c, q0 in chunks:
            softmax(slot, c, q0, True)
            pv(qi, slot, c, q0, True)

    for c, _ in chunks:
        qk(0, 0, c)

    @pl.when(qi == 0)
    def _():
        drain(0)

    @pl.when(qi >= 1)
    def _():
        for c, q0 in chunks:
            qk(1, 1, c)
            softmax(0, c, q0, False)

    def body(u, carry):
        t = 2 + 2 * u
        step(t, 0)
        step(t + 1, 1)
        return carry

    lax.fori_loop(0, lax.shift_right_arithmetic(qi - 1, 1), body, 0)

    @pl.when(jnp.logical_and(qi >= 2, qi % 2 == 0))
    def _():
        step(qi, 0)
        for c, q0 in chunks:
            pv(qi - 1, 1, c, q0, False)
        drain(0)

    @pl.when(qi % 2 == 1)
    def _():
        for c, q0 in chunks:
            pv(qi - 1, 0, c, q0, False)
        drain(1)


DIFF_HEADS_PER_STEP = 1


def _diff_attn_kernel(q_ref, k_ref, v_ref, lq1_ref, lk1_ref, lq2_ref, lk2_ref, sw_ref, o_ref,
                      vT_ref, qT_ref, s_ref, p_ref, a_ref, m_ref, acc_ref, o_stage, *, T, lam_init):
    eye = _eye_bf16(LANES)
    hd = 2 * DIFF_HEAD_DIM
    dv = hd
    CW = qT_ref.shape[2]
    per_map = T // CW
    E = DIFF_HEADS_PER_STEP
    for e in range(E):
        for jb in range(vT_ref.shape[1]):
            vT_ref[e, jb, 0:dv, :] = _transpose_bf16(v_ref[jb * T:(jb + 1) * T, e * hd:(e + 1) * hd], eye)
            vT_ref[e, jb, dv:dv + ONES_ROWS, :] = jnp.ones((ONES_ROWS, T), BF16)
    lam = (jnp.exp(jnp.sum(lq1_ref[...] * lk1_ref[...], axis=1, keepdims=True))
           - jnp.exp(jnp.sum(lq2_ref[...] * lk2_ref[...], axis=1, keepdims=True)) + lam_init)
    kv_refs = [(functools.partial(lambda row, e: k_ref[pl.ds(row, T), :][:, e * hd:(e + 1) * hd], e=e),
                vT_ref.at[e]) for e in range(E)]

    def q_block(qi, carry):
        row0 = pl.multiple_of(qi * T, T)
        q_all = q_ref[pl.ds(row0, T), :]
        for e in range(E):
            q = q_all[:, e * hd:(e + 1) * hd]
            lane = lax.broadcasted_iota(jnp.int32, q.shape, 1)
            zero = jnp.zeros_like(q)
            for half, keep in enumerate((lane < DIFF_HEAD_DIM, lane >= DIFF_HEAD_DIM)):
                qh = jnp.where(keep, q, zero)
                for c in range(per_map):
                    qT_ref[(2 * e + half) * per_map + c] = _transpose_bf16(qh[c * CW:(c + 1) * CW, :], eye)

        _causal_attn_loop(qi, kv_refs, qT_ref, s_ref, p_ref, a_ref, m_ref, acc_ref, T=T, n_maps=2 * E,
                          order=DIFF_STAGE_ORDER)

        for e in range(E):
            for c in range(per_map):
                c1, c2 = 2 * e * per_map + c, (2 * e + 1) * per_map + c
                o1 = acc_ref[c1, 0:dv, :] / acc_ref[c1, dv:dv + 1, :]
                o2 = acc_ref[c2, 0:dv, :] / acc_ref[c2, dv:dv + 1, :]
                odT = o1 - lam * o2
                ms = jnp.mean(odT * odT, axis=0, keepdims=True)
                outT = (odT * lax.rsqrt(ms + DIFF_SUBLN_EPS) * sw_ref[...] * (1.0 - lam_init)).astype(BF16)
                for r in range(CW // LANES):
                    r0 = c * CW + r * LANES
                    o_stage[r0:r0 + LANES, e * hd:(e + 1) * hd] = _transpose_bf16(
                        outT[:, r * LANES:(r + 1) * LANES], eye).astype(o_stage.dtype)
        o_ref[pl.ds(row0, T), :] = o_stage[...]
        return carry

    lax.fori_loop(0, q_ref.shape[0] // T, q_block, 0)


def _diff_attn(zqkv, lq1, lk1, lq2, lk2, subln_w, *, lam_init, T):
    S = zqkv.shape[0]
    H = DIFF_HEADS
    E = DIFF_HEADS_PER_STEP
    hd = 2 * DIFF_HEAD_DIM
    cw = min(DIFF_CHUNK, T)
    nc = E * 2 * (T // cw)
    vec = lambda a: a.reshape(1, -1).astype(F32)
    small = lambda n: pl.BlockSpec((1, n), lambda g: (0, 0))
    return pl.pallas_call(
        functools.partial(_diff_attn_kernel, T=T, lam_init=lam_init),
        grid=(H // E,),
        in_specs=[
            pl.BlockSpec((S, E * hd), lambda g: (0, g)),
            pl.BlockSpec((S, E * hd), lambda g: (0, H // E + g)),
            pl.BlockSpec((S, E * hd), lambda g: (0, 2 * (H // E) + g)),
            small(DIFF_HEAD_DIM), small(DIFF_HEAD_DIM), small(DIFF_HEAD_DIM), small(DIFF_HEAD_DIM),
            pl.BlockSpec((hd, 1), lambda g: (0, 0)),
        ],
        out_specs=pl.BlockSpec((S, E * hd), lambda g: (0, g)),
        out_shape=jax.ShapeDtypeStruct((S, H * hd), BF16),
        scratch_shapes=[pltpu.VMEM((E, S // T, hd + ONES_ROWS, T), BF16), pltpu.VMEM((nc, hd, cw), BF16),
                        pltpu.VMEM((2, nc, T, cw), F32), pltpu.VMEM((2, nc, T, cw), BF16),
                        pltpu.VMEM((2, nc, 1, cw), F32), pltpu.VMEM((nc, 1, cw), F32),
                        pltpu.VMEM((nc, hd + ONES_ROWS, cw), F32), pltpu.VMEM((T, E * hd), BF16)],
        compiler_params=_cparams(1),
        name="diff_attention",
    )(zqkv, zqkv, zqkv, vec(lq1), vec(lk1), vec(lq2), vec(lk2), subln_w.reshape(-1, 1).astype(F32))


MLA_HEADS_PER_STEP = 2


def _mla_attn_kernel(qn_ref, qr_ref, kv_ref, kr_ref, o_ref, kcat, vT_ref, qT_ref, s_ref, p_ref, a_ref,
                     m_ref, acc_ref, o_stage, *, T):
    eye = _eye_bf16(LANES)
    dv = MLA_V_DIM
    CW = qT_ref.shape[2]
    per_map = T // CW
    pair_w = MLA_NOPE_DIM + MLA_V_DIM

    for e in range(MLA_HEADS_PER_STEP):
        kcat[e, :, 0:MLA_NOPE_DIM] = kv_ref[:, e * pair_w:e * pair_w + MLA_NOPE_DIM]
        kcat[e, :, MLA_NOPE_DIM:] = kr_ref[...]
        for jb in range(vT_ref.shape[1]):
            v_blk = kv_ref[jb * T:(jb + 1) * T, e * pair_w + MLA_NOPE_DIM:(e + 1) * pair_w]
            vT_ref[e, jb, 0:dv, :] = _transpose_bf16(v_blk, eye)
            vT_ref[e, jb, dv:dv + ONES_ROWS, :] = jnp.ones((ONES_ROWS, T), BF16)

    def q_block(qi, carry):
        row0 = pl.multiple_of(qi * T, T)
        qr = qr_ref[pl.ds(row0, T), :]
        lane = lax.broadcasted_iota(jnp.int32, qr.shape, 1)
        qn_all = qn_ref[pl.ds(row0, T), :]
        for e in range(MLA_HEADS_PER_STEP):
            mine = jnp.logical_and(lane >= e * MLA_ROPE_DIM, lane < (e + 1) * MLA_ROPE_DIM)
            qn = qn_all[:, e * MLA_NOPE_DIM:(e + 1) * MLA_NOPE_DIM]
            qrm = jnp.where(mine, qr, jnp.zeros_like(qr))
            for c in range(per_map):
                qT_ref[e * per_map + c, 0:MLA_NOPE_DIM, :] = _transpose_bf16(qn[c * CW:(c + 1) * CW, :], eye)
                qT_ref[e * per_map + c, MLA_NOPE_DIM:, :] = _transpose_bf16(qrm[c * CW:(c + 1) * CW, :], eye)

        kv_refs = [(functools.partial(lambda row, e: kcat[e, pl.ds(row, T), :], e=e), vT_ref.at[e])
                   for e in range(MLA_HEADS_PER_STEP)]
        _causal_attn_loop(qi, kv_refs, qT_ref, s_ref, p_ref, a_ref, m_ref, acc_ref, T=T,
                          n_maps=MLA_HEADS_PER_STEP, order=MLA_STAGE_ORDER)

        for e in range(MLA_HEADS_PER_STEP):
            for c in range(per_map):
                cc = e * per_map + c
                oT = (acc_ref[cc, 0:dv, :] / acc_ref[cc, dv:dv + 1, :]).astype(BF16)
                for r in range(CW // LANES):
                    r0 = c * CW + r * LANES
                    o_stage[r0:r0 + LANES, e * dv:(e + 1) * dv] = _transpose_bf16(
                        oT[:, r * LANES:(r + 1) * LANES], eye).astype(o_stage.dtype)
        o_ref[pl.ds(row0, T), :] = o_stage[...]
        return carry

    lax.fori_loop(0, qn_ref.shape[0] // T, q_block, 0)


def _mla_attn(qm, kv, kr_dup, *, T):
    S = qm.shape[0]
    H = MLA_HEADS
    E = MLA_HEADS_PER_STEP
    cw = min(MLA_CHUNK, T)
    nc = E * (T // cw)
    return pl.pallas_call(
        functools.partial(_mla_attn_kernel, T=T),
        grid=(H // E,),
        in_specs=[
            pl.BlockSpec((S, E * MLA_NOPE_DIM), lambda g: (0, g)),
            pl.BlockSpec((S, LANES), lambda g: (0, H + g)),
            pl.BlockSpec((S, E * (MLA_NOPE_DIM + MLA_V_DIM)), lambda g: (0, g)),
            pl.BlockSpec((S, LANES), lambda g: (0, 0)),
        ],
        out_specs=pl.BlockSpec((S, E * MLA_V_DIM), lambda g: (0, g)),
        out_shape=jax.ShapeDtypeStruct((S, H * MLA_V_DIM), BF16),
        scratch_shapes=[pltpu.VMEM((E, S, MLA_NOPE_DIM + LANES), BF16),
                        pltpu.VMEM((E, S // T, MLA_V_DIM + ONES_ROWS, T), BF16),
                        pltpu.VMEM((nc, MLA_NOPE_DIM + LANES, cw), BF16),
                        pltpu.VMEM((2, nc, T, cw), F32), pltpu.VMEM((2, nc, T, cw), BF16),
                        pltpu.VMEM((2, nc, 1, cw), F32), pltpu.VMEM((nc, 1, cw), F32),
                        pltpu.VMEM((nc, MLA_V_DIM + ONES_ROWS, cw), F32),
                        pltpu.VMEM((T, E * MLA_V_DIM), BF16)],
        compiler_params=_cparams(1, V7X_VMEM_LIMIT_MAX_BYTES),
        name="mla_attention",
    )(qm, qm, kv, kr_dup)


def _block_forward(x2d, pos_col, l, norm_mix_w, w_in, lq1, lk1, lq2, lk2, subln_w, q_norm_w, w_uq,
                   kv_norm_w, w_ukv, w_o_diff, w_o_mla, w_out, norm_ffn_w, w_up, conv_w, conv_b, w_down,
                   *, tm=1024, tn=512, t_attn=512, tn_ffn=256, tm_down=512, tm_wide=512, tn_wide=1024,
                   tm_lat=2048, tn_lat=1024):
    S, D = x2d.shape
    H = DIFF_HEADS
    qkv_w = 3 * H * 2 * DIFF_HEAD_DIM
    q_rank = w_uq.shape[0]
    kv_rank = w_ukv.shape[0]
    lat_w = q_rank + kv_rank
    main_w = qkv_w + lat_w
    gate_start = main_w + MLA_ROPE_DIM
    lam_init = 0.8 - 0.6 * math.exp(-0.3 * l)

    cos, sin = _rope_tables(pos_col)
    h = _rmsnorm(x2d, norm_mix_w, BF16)

    w_in_t = jnp.swapaxes(w_in, 0, 1)
    z = _mm([dict(a=h, a_blk=0, K=D, w=w_in_t, w_row_blk=0, w_col_blk=0)], N=main_w, tm=tm, tn=tn,
            out_dtype=BF16, name="in_proj_main", epilogue="rope_lt",
            epi_arg=((2 * H * 2 * DIFF_HEAD_DIM) // tn, (H * 2 * DIFF_HEAD_DIM) // tn,
                     DIFF_HEAD_DIM ** -0.5 * LOG2_E), rope=(cos, sin), w_transposed=True)
    kr_dup = _mm([dict(a=h, a_blk=0, K=D, w=w_in_t, w_row_blk=0, w_col_blk=main_w // LANES)], N=LANES,
                 tm=tm, tn=LANES, out_dtype=BF16, name="in_proj_krope", epilogue="krope_dup",
                 rope=(cos, sin), w_transposed=True)
    gates = _mm([dict(a=h, a_blk=0, K=D, w=w_in_t, w_row_blk=0, w_elem_off=gate_start)], N=2 * D,
                tm=tm_wide, tn=tn_wide, out_dtype=BF16, name="in_proj_gates", epilogue="sigmoid",
                w_transposed=True)

    o_d = _diff_attn(z, lq1, lk1, lq2, lk2, subln_w, lam_init=lam_init, T=t_attn)

    qk_dim = MLA_NOPE_DIM + MLA_ROPE_DIM
    w_uq3 = w_uq.reshape(q_rank, MLA_HEADS, qk_dim)
    w_uq_perm = jnp.concatenate([w_uq3[:, :, :MLA_NOPE_DIM].reshape(q_rank, -1),
                                 w_uq3[:, :, MLA_NOPE_DIM:].reshape(q_rank, -1)], axis=1)
    qm = _mm([dict(a=z, a_blk=qkv_w // q_rank, K=q_rank, w=w_uq_perm, w_row_blk=0, w_col_blk=0)],
             N=MLA_HEADS * qk_dim, tm=tm_lat, tn=tn_lat, out_dtype=BF16, name="mla_q_up", norm_w=q_norm_w,
             epilogue="scale_rope_ge",
             epi_arg=(qk_dim ** -0.5 * LOG2_E, (MLA_HEADS * MLA_NOPE_DIM) // tn_lat), rope=(cos, sin))
    kv = _mm([dict(a=z, a_blk=(qkv_w + q_rank) // kv_rank, K=kv_rank, w=w_ukv, w_row_blk=0, w_col_blk=0)],
             N=w_ukv.shape[1], tm=tm_lat, tn=tn_lat, out_dtype=BF16, name="mla_kv_up", norm_w=kv_norm_w)
    o_m = _mla_attn(qm, kv, kr_dup, T=t_attn)

    y = _mm([dict(a=o_d, a_blk=0, K=o_d.shape[1], w=w_o_diff, w_row_blk=0, w_col_blk=0, gate=(gates, 0)),
             dict(a=o_m, a_blk=0, K=o_m.shape[1], w=w_o_mla, w_row_blk=0, w_col_blk=0, gate=(gates, D // tn))],
            N=D, tm=tm, tn=tn, out_dtype=BF16, name="branch_merge")
    x1 = _mm([dict(a=y, a_blk=0, K=D, w=w_out, w_row_blk=0, w_col_blk=0)], N=D, tm=tm, tn=tn,
             out_dtype=F32, name="out_proj", addend=x2d)

    h2 = _rmsnorm(x1, norm_ffn_w, BF16)
    act = _ffn_up(h2, w_up, conv_w, conv_b, tm=tm, tn=tn_ffn)
    d_ff = act.shape[1]
    k_half = d_ff // 2
    p0 = _mm([dict(a=act, a_blk=0, K=k_half, w=w_down, w_row_blk=0, w_col_blk=0)], N=D, tm=tm_down, tn=tn,
             out_dtype=F32, name="ffn_down_lo", addend=x1)
    x2 = _mm([dict(a=act, a_blk=1, K=k_half, w=w_down, w_row_blk=1, w_col_blk=0)], N=D, tm=tm_down, tn=tn,
             out_dtype=F32, name="ffn_down_hi", addend=p0)
    return x2


def kernel(x, positions, norm_mix_w, w_in, diff_lambda_q1, diff_lambda_k1, diff_lambda_q2, diff_lambda_k2, diff_subln_w, mla_q_norm_w, mla_w_uq, mla_kv_norm_w, mla_w_ukv, w_o_diff, w_o_mla, w_out, norm_ffn_w, ffn_w_up, ffn_conv_w, ffn_conv_b, ffn_w_down, final_norm_w):
    B, S, D = x.shape
    assert B == 1
    x2d = x.reshape(S, D)
    pos_col = positions.reshape(S, 1)
    for l in range(w_in.shape[0]):
        x2d = _block_forward(
            x2d, pos_col, l, norm_mix_w[l], w_in[l], diff_lambda_q1[l], diff_lambda_k1[l], diff_lambda_q2[l],
            diff_lambda_k2[l], diff_subln_w[l], mla_q_norm_w[l], mla_w_uq[l], mla_kv_norm_w[l], mla_w_ukv[l],
            w_o_diff[l], w_o_mla[l], w_out[l], norm_ffn_w[l], ffn_w_up[l], ffn_conv_w[l], ffn_conv_b[l],
            ffn_w_down[l])
    out = _rmsnorm(x2d, final_norm_w, F32)
    return out.reshape(B, S, D)
```

```python
import functools
import math

import jax
import jax.numpy as jnp
from jax import lax
from jax.experimental import pallas as pl
from jax.experimental.pallas import tpu as pltpu

BF16 = jnp.bfloat16
F32 = jnp.float32

LANES = 128
V7X_VMEM_LIMIT_BYTES = 56 << 20
V7X_VMEM_LIMIT_MAX_BYTES = 62 << 20

DIFF_HEADS = 16
DIFF_HEAD_DIM = 64
MLA_HEADS = 16
MLA_NOPE_DIM = 128
MLA_ROPE_DIM = 64
MLA_V_DIM = 128
ROPE_THETA = 10000.0
NORM_EPS = 1e-6
DIFF_SUBLN_EPS = 1e-5
CONV_WIDTH = 3
MASK_VALUE = -1e30
LOG2_E = math.log2(math.e)


def _cparams(n_axes, vmem_limit_bytes=V7X_VMEM_LIMIT_BYTES):
    return pltpu.CompilerParams(
        dimension_semantics=("arbitrary",) * n_axes,
        vmem_limit_bytes=vmem_limit_bytes,
    )


def _rope_table_kernel(pos_ref, freq_ref, sign_ref, cos_ref, sin_ref):
    ang = pos_ref[...].astype(F32) * freq_ref[...]
    cos_ref[...] = jnp.cos(ang)
    sin_ref[...] = jnp.sin(ang) * sign_ref[...]


def _rope_tables(pos_col, tm=2048):
    S = pos_col.shape[0]
    half = MLA_ROPE_DIM // 2
    inv_freq = ROPE_THETA ** (-jnp.arange(0, MLA_ROPE_DIM, 2, dtype=F32) / MLA_ROPE_DIM)
    freq = jnp.tile(inv_freq, LANES // half).reshape(1, LANES)
    sign = jnp.tile(jnp.concatenate([-jnp.ones((half,), F32), jnp.ones((half,), F32)]),
                    LANES // (2 * half)).reshape(1, LANES)
    return pl.pallas_call(
        _rope_table_kernel,
        grid=(S // tm,),
        in_specs=[pl.BlockSpec((tm, 1), lambda i: (i, 0)),
                  pl.BlockSpec((1, LANES), lambda i: (0, 0)),
                  pl.BlockSpec((1, LANES), lambda i: (0, 0))],
        out_specs=[pl.BlockSpec((tm, LANES), lambda i: (i, 0)),
                   pl.BlockSpec((tm, LANES), lambda i: (i, 0))],
        out_shape=[jax.ShapeDtypeStruct((S, LANES), F32)] * 2,
        compiler_params=_cparams(1),
        name="rope_tables",
    )(pos_col, freq, sign)


def _rope_partner(zc):
    lane = lax.broadcasted_iota(jnp.int32, zc.shape, 1)
    first_half = (lane & 32) == 0
    return jnp.where(first_half, pltpu.roll(zc, 96, 1), pltpu.roll(zc, 32, 1))


def _rope_lanes(z, cos, sin):
    outs = []
    for c in range(z.shape[1] // LANES):
        zc = z[:, c * LANES:(c + 1) * LANES]
        outs.append(zc * cos + _rope_partner(zc) * sin)
    return outs[0] if len(outs) == 1 else jnp.concatenate(outs, axis=1)


def _rmsnorm_kernel(x_ref, w_ref, o_ref, *, eps):
    xf = x_ref[...].astype(F32)
    ms = jnp.mean(xf * xf, axis=1, keepdims=True)
    o_ref[...] = (xf * lax.rsqrt(ms + eps) * w_ref[...]).astype(o_ref.dtype)


def _rmsnorm(x, w, out_dtype, eps=NORM_EPS, tm=512):
    M, D = x.shape
    return pl.pallas_call(
        functools.partial(_rmsnorm_kernel, eps=eps),
        grid=(M // tm,),
        in_specs=[pl.BlockSpec((tm, D), lambda i: (i, 0)),
                  pl.BlockSpec((1, D), lambda i: (0, 0))],
        out_specs=pl.BlockSpec((tm, D), lambda i: (i, 0)),
        out_shape=jax.ShapeDtypeStruct((M, D), out_dtype),
        compiler_params=_cparams(1),
        name="rmsnorm",
    )(x, w.reshape(1, D).astype(F32))


def _tile_walk(ni, n_steps):
    def cur(s):
        c = jnp.minimum(s, n_steps - 1)
        return c % ni, c // ni

    def prev(s):
        p = jnp.maximum(s - 1, 0)
        return p % ni, p // ni

    return cur, prev


MM_SUB_ROWS = 256
MXU_DIM = 256


def _mm_kernel(*refs, n_pairs, has_gate, has_norm, has_addend, has_rope, epilogue, epi_arg, eps, ni, n_steps,
               tm, sub, w_transposed):
    refs = list(refs)
    pair_refs = []
    for _ in range(n_pairs):
        a_ref = refs.pop(0)
        w_ref = refs.pop(0)
        g_ref = refs.pop(0) if has_gate else None
        pair_refs.append((a_ref, w_ref, g_ref))
    nw_ref = refs.pop(0) if has_norm else None
    add_ref = refs.pop(0) if has_addend else None
    cos_ref = refs.pop(0) if has_rope else None
    sin_ref = refs.pop(0) if has_rope else None
    o_ref = refs.pop(0)
    wbf_refs = refs[:n_pairs]
    raw_refs = refs[n_pairs:]

    s = pl.program_id(0)
    cur, prev = _tile_walk(ni, n_steps)
    i_cur, _ = cur(s)
    _, j = prev(s)

    @pl.when(s == 0)
    def _():
        for raw in raw_refs:
            raw[...] = jnp.zeros(raw.shape, F32)

    @pl.when(i_cur == 0)
    def _():
        for (_, w_ref, _), wbf in zip(pair_refs, wbf_refs):
            if w_transposed:
                eye = _eye_bf16(MXU_DIM)
                for kc in range(wbf.shape[0] // MXU_DIM):
                    ks = slice(kc * MXU_DIM, (kc + 1) * MXU_DIM)
                    wbf[ks, :] = _nt_dot(eye, w_ref[:, ks].astype(BF16)).astype(BF16)
            else:
                wbf[...] = w_ref[...].astype(BF16)

    def finish_previous(rows, roped):
        if has_gate:
            acc = None
            for (_, _, g_ref), raw in zip(pair_refs, raw_refs):
                d = raw[rows, :] * g_ref[rows, :].astype(F32)
                acc = d if acc is None else acc + d
        else:
            acc = raw_refs[0][rows, :]
        if has_addend:
            acc = acc + add_ref[rows, :]
        if epilogue == "sigmoid":
            out = 1.0 / (1.0 + jnp.exp(-acc))
        elif epilogue == "rope_lt":
            _, n_scaled, scale = epi_arg
            out = acc
            if roped:
                out = _rope_lanes(acc, cos_ref[rows, :], sin_ref[rows, :]) * jnp.where(j < n_scaled, scale, 1.0)
        elif epilogue == "scale_rope_ge":
            out = acc * epi_arg[0]
            if roped:
                out = _rope_lanes(out, cos_ref[rows, :], sin_ref[rows, :])
        elif epilogue == "krope_dup":
            lane = lax.broadcasted_iota(jnp.int32, acc.shape, 1)
            kr = jnp.where(lane < MLA_ROPE_DIM, acc, 0.0)
            r = kr * cos_ref[rows, :] + _rope_partner(kr) * sin_ref[rows, :]
            out = r + pltpu.roll(r, MLA_ROPE_DIM, 1)
        else:
            out = acc
        o_ref[rows, :] = out.astype(o_ref.dtype)

    def multiply_current(rows):
        acc = None
        for p, ((a_ref, _, _), wbf) in enumerate(zip(pair_refs, wbf_refs)):
            a = a_ref[rows, :]
            if has_norm:
                af = a.astype(F32)
                ms = jnp.mean(af * af, axis=1, keepdims=True)
                a = (af * lax.rsqrt(ms + eps) * nw_ref[...]).astype(BF16)
            d = jnp.dot(a, wbf[...], preferred_element_type=F32)
            if has_gate:
                raw_refs[p][rows, :] = d
            else:
                acc = d if acc is None else acc + d
        if not has_gate:
            raw_refs[0][rows, :] = acc

    def step(roped):
        for r in range(tm // sub):
            rows = slice(r * sub, (r + 1) * sub)
            finish_previous(rows, roped)
            multiply_current(rows)

    if epilogue == "rope_lt":
        pl.when(j < epi_arg[0])(lambda: step(True))
        pl.when(j >= epi_arg[0])(lambda: step(False))
    elif epilogue == "scale_rope_ge":
        pl.when(j >= epi_arg[1])(lambda: step(True))
        pl.when(j < epi_arg[1])(lambda: step(False))
    else:
        step(False)


def _mm(pairs, *, N, tm, tn, out_dtype, name, epilogue="none", epi_arg=None,
        norm_w=None, addend=None, rope=None, eps=NORM_EPS, w_transposed=False):
    M = pairs[0]["a"].shape[0]
    ni, nj = M // tm, N // tn
    n_steps = ni * nj
    cur, prev = _tile_walk(ni, n_steps)
    has_gate = pairs[0].get("gate") is not None
    args, in_specs, wbf_scratch = [], [], []

    def at_cur(fn):
        return lambda s: fn(*cur(s))

    def at_prev(fn):
        return lambda s: fn(*prev(s))

    for p in pairs:
        K = p["K"]
        args.append(p["a"])
        in_specs.append(pl.BlockSpec((tm, K), at_cur(functools.partial(lambda i, j, b: (i, b), b=p["a_blk"]))))
        args.append(p["w"])
        if not w_transposed:
            in_specs.append(pl.BlockSpec((K, tn), at_cur(functools.partial(
                lambda i, j, r, c: (r, c + j), r=p["w_row_blk"], c=p["w_col_blk"]))))
        elif "w_elem_off" in p:
            in_specs.append(pl.BlockSpec((pl.Element(tn), pl.Element(K)), at_cur(functools.partial(
                lambda i, j, r, off, k: (pl.multiple_of(off + tn * j, math.gcd(off, tn)), r * k),
                r=p["w_row_blk"], off=p["w_elem_off"], k=K))))
        else:
            in_specs.append(pl.BlockSpec((tn, K), at_cur(functools.partial(
                lambda i, j, r, c: (c + j, r), r=p["w_row_blk"], c=p["w_col_blk"]))))
        if has_gate:
            g, g_off = p["gate"]
            args.append(g)
            in_specs.append(pl.BlockSpec((tm, tn), at_prev(functools.partial(lambda i, j, c: (i, c + j), c=g_off))))
        wbf_scratch.append(pltpu.VMEM((K, tn), BF16))
    if norm_w is not None:
        args.append(norm_w.reshape(1, -1).astype(F32))
        in_specs.append(pl.BlockSpec((1, norm_w.shape[-1]), lambda s: (0, 0)))
    if addend is not None:
        args.append(addend)
        in_specs.append(pl.BlockSpec((tm, tn), at_prev(lambda i, j: (i, j))))
    if rope is not None:
        for t in rope:
            args.append(t)
            in_specs.append(pl.BlockSpec((tm, LANES), at_prev(lambda i, j: (i, 0))))
    raw_scratch = [pltpu.VMEM((tm, tn), F32)] * (len(pairs) if has_gate else 1)
    kern = functools.partial(
        _mm_kernel, n_pairs=len(pairs), has_gate=has_gate, has_norm=norm_w is not None,
        has_addend=addend is not None, has_rope=rope is not None, epilogue=epilogue, epi_arg=epi_arg, eps=eps,
        ni=ni, n_steps=n_steps, tm=tm, sub=min(MM_SUB_ROWS, tm), w_transposed=w_transposed)
    return pl.pallas_call(
        kern,
        grid=(n_steps + 1,),
        in_specs=in_specs,
        out_specs=pl.BlockSpec((tm, tn), at_prev(lambda i, j: (i, j))),
        out_shape=jax.ShapeDtypeStruct((M, N), out_dtype),
        scratch_shapes=wbf_scratch + raw_scratch,
        compiler_params=_cparams(1),
        name=name,
    )(*args)


CONV_HALO = 8


def _ffn_up_kernel(a_ref, wg_ref, wv_ref, cwg_ref, cwv_ref, cbg_ref, cbv_ref, o_ref,
                   wgbf, wvbf, ug_buf, uv_buf, *, tm, ni, n_steps):
    s = pl.program_id(0)
    cur, _ = _tile_walk(ni, n_steps)
    i_cur, _ = cur(s)

    @pl.when(s == 0)
    def _():
        ug_buf[...] = jnp.zeros(ug_buf.shape, F32)
        uv_buf[...] = jnp.zeros(uv_buf.shape, F32)

    @pl.when(i_cur == 0)
    def _():
        wgbf[...] = wg_ref[...].astype(BF16)
        wvbf[...] = wv_ref[...].astype(BF16)

    sub = min(MM_SUB_ROWS, tm)

    def conv(buf, r0, cw_ref, cb_ref):
        lo = CONV_HALO + r0
        out = cb_ref[...] + cw_ref[0:1, :] * buf[lo - 2:lo - 2 + sub, :]
        out = out + cw_ref[1:2, :] * buf[lo - 1:lo - 1 + sub, :]
        return out + cw_ref[2:3, :] * buf[lo:lo + sub, :]

    keep = jnp.where(i_cur == 0, 0.0, 1.0)
    halo_g = ug_buf[tm:tm + CONV_HALO, :] * keep
    halo_v = uv_buf[tm:tm + CONV_HALO, :] * keep

    for r in reversed(range(tm // sub)):
        r0 = r * sub
        g = conv(ug_buf, r0, cwg_ref, cbg_ref)
        v = conv(uv_buf, r0, cwv_ref, cbv_ref)
        o_ref[r0:r0 + sub, :] = (g / (1.0 + jnp.exp(-g)) * v).astype(o_ref.dtype)
        a = a_ref[r0:r0 + sub, :]
        ug_buf[CONV_HALO + r0:CONV_HALO + r0 + sub, :] = jnp.dot(a, wgbf[...], preferred_element_type=F32)
        uv_buf[CONV_HALO + r0:CONV_HALO + r0 + sub, :] = jnp.dot(a, wvbf[...], preferred_element_type=F32)

    ug_buf[0:CONV_HALO, :] = halo_g
    uv_buf[0:CONV_HALO, :] = halo_v


def _ffn_up(h, w_up, conv_w, conv_b, *, tm, tn):
    M, K = h.shape
    d_ff = w_up.shape[1] // 2
    ni, nj = M // tm, d_ff // tn
    n_steps = ni * nj
    cur, prev = _tile_walk(ni, n_steps)
    cb = conv_b.reshape(1, -1)

    def at_cur(fn):
        return lambda s: fn(*cur(s))

    def at_prev(fn):
        return lambda s: fn(*prev(s))

    return pl.pallas_call(
        functools.partial(_ffn_up_kernel, tm=tm, ni=ni, n_steps=n_steps),
        grid=(n_steps + 1,),
        in_specs=[
            pl.BlockSpec((tm, K), at_cur(lambda i, j: (i, 0))),
            pl.BlockSpec((K, tn), at_cur(lambda i, j: (0, j))),
            pl.BlockSpec((K, tn), at_cur(lambda i, j: (0, nj + j))),
            pl.BlockSpec((CONV_WIDTH, tn), at_prev(lambda i, j: (0, j))),
            pl.BlockSpec((CONV_WIDTH, tn), at_prev(lambda i, j: (0, nj + j))),
            pl.BlockSpec((1, tn), at_prev(lambda i, j: (0, j))),
            pl.BlockSpec((1, tn), at_prev(lambda i, j: (0, nj + j))),
        ],
        out_specs=pl.BlockSpec((tm, tn), at_prev(lambda i, j: (i, j))),
        out_shape=jax.ShapeDtypeStruct((M, d_ff), BF16),
        scratch_shapes=[pltpu.VMEM((K, tn), BF16), pltpu.VMEM((K, tn), BF16),
                        pltpu.VMEM((CONV_HALO + tm, tn), F32), pltpu.VMEM((CONV_HALO + tm, tn), F32)],
        compiler_params=_cparams(1),
        name="ffn_up_conv_gate",
    )(h, w_up, w_up, conv_w, conv_w, cb, cb)


DIFF_CHUNK = 256
MLA_CHUNK = 256
DIFF_STAGE_ORDER = "q|sp"
MLA_STAGE_ORDER = "q|sp"


def _nt_dot(a, b):
    return lax.dot_general(a, b, (((1,), (1,)), ((), ())), preferred_element_type=F32)


def _eye_bf16(n):
    r = lax.broadcasted_iota(jnp.int32, (n, n), 0)
    c = lax.broadcasted_iota(jnp.int32, (n, n), 1)
    return jnp.where(r == c, 1.0, 0.0).astype(BF16)


def _transpose_bf16(x, eye):
    return _nt_dot(eye, x).astype(BF16)


SUBLANES = 8
REDUCE_WAYS = 8


def _reduce_rows(x, op, final):
    n = x.shape[0]
    groups = [x[r * SUBLANES:(r + 1) * SUBLANES] for r in range(n // SUBLANES)]
    ways = min(REDUCE_WAYS, len(groups))
    parts = groups[:ways]
    for g, blk in enumerate(groups[ways:]):
        parts[g % ways] = op(parts[g % ways], blk)
    while len(parts) > 1:
        parts = [op(parts[i], parts[i + 1]) if i + 1 < len(parts) else parts[i] for i in range(0, len(parts), 2)]
    return final(parts[0], axis=0, keepdims=True)


def _softmax_chunk(load_s, m_ref, c, mask_q0):
    def scores():
        sT = load_s()
        if mask_q0 is not None:
            key = lax.broadcasted_iota(jnp.int32, sT.shape, 0)
            qq = lax.broadcasted_iota(jnp.int32, sT.shape, 1) + mask_q0
            sT = jnp.where(qq >= key, sT, MASK_VALUE)
        return sT

    m_prev = m_ref[c]
    m_new = jnp.maximum(m_prev, _reduce_rows(scores(), jnp.maximum, jnp.max))
    p = jnp.exp2(scores() - m_new)
    alpha = jnp.exp2(m_prev - m_new)
    m_ref[c] = m_new
    return p.astype(BF16), alpha


ONES_ROWS = 16


def _causal_attn_loop(qi, kv_refs, qT_ref, s_ref, p_ref, a_ref, m_ref, acc_ref, *, T, n_maps, order):
    CW = qT_ref.shape[2]
    per_map = T // CW
    chunks = [(c, (c % per_map) * CW) for c in range(n_maps * per_map)]
    m_ref[...] = jnp.full(m_ref.shape, MASK_VALUE, F32)
    acc_ref[...] = jnp.zeros(acc_ref.shape, F32)

    maps_per_kv = n_maps // len(kv_refs)

    def kv_of(c):
        return kv_refs[c // (per_map * maps_per_kv)]

    def qk(blk, slot, c):
        k = kv_of(c)[0](pl.multiple_of(blk * T, T))
        s_ref[slot, c] = jnp.dot(k, qT_ref[c], preferred_element_type=F32)

    def softmax(slot, c, q0, diagonal):
        n = q0 + CW if diagonal else T
        p, alpha = _softmax_chunk(lambda: s_ref[slot, c, 0:n, :], m_ref, c, q0 if diagonal else None)
        p_ref[slot, c, 0:n, :] = p
        a_ref[slot, c] = alpha

    def pv(blk, slot, c, q0, diagonal):
        n = q0 + CW if diagonal else T
        acc_ref[c] = acc_ref[c] * a_ref[slot, c] + jnp.dot(
            kv_of(c)[1][blk, :, 0:n], p_ref[slot, c, 0:n, :], preferred_element_type=F32)

    def step(t, slot):
        stage = {"q": lambda c, q0: qk(t, slot, c),
                 "s": lambda c, q0: softmax(1 - slot, c, q0, False),
                 "p": lambda c, q0: pv(t - 2, slot, c, q0, False)}
        for group in order.split("|"):
            for c, q0 in chunks:
                for name in group:
                    stage[name](c, q0)

    def drain(slot):
        for c, q0 in chunks:
            softmax(slot, c, q0, True)
            pv(qi, slot, c, q0, True)

    for c, _ in chunks:
        qk(0, 0, c)

    @pl.when(qi == 0)
    def _():
        drain(0)

    @pl.when(qi >= 1)
    def _():
        for c, q0 in chunks:
            qk(1, 1, c)
            softmax(0, c, q0, False)

    def body(u, carry):
        t = 2 + 2 * u
        step(t, 0)
        step(t + 1, 1)
        return carry

    lax.fori_loop(0, lax.shift_right_arithmetic(qi - 1, 1), body, 0)

    @pl.when(jnp.logical_and(qi >= 2, qi % 2 == 0))
    def _():
        step(qi, 0)
        for c, q0 in chunks:
            pv(qi - 1, 1, c, q0, False)
        drain(0)

    @pl.when(qi % 2 == 1)
    def _():
        for c, q0 in chunks:
            pv(qi - 1, 0, c, q0, False)
        drain(1)


DIFF_HEADS_PER_STEP = 1


def _diff_attn_kernel(q_ref, k_ref, v_ref, lq1_ref, lk1_ref, lq2_ref, lk2_ref, sw_ref, o_ref,
                      vT_ref, qT_ref, s_ref, p_ref, a_ref, m_ref, acc_ref, o_stage, *, T, lam_init):
    eye = _eye_bf16(LANES)
    hd = 2 * DIFF_HEAD_DIM
    dv = hd
    CW = qT_ref.shape[2]
    per_map = T // CW
    E = DIFF_HEADS_PER_STEP
    for e in range(E):
        for jb in range(vT_ref.shape[1]):
            vT_ref[e, jb, 0:dv, :] = _transpose_bf16(v_ref[jb * T:(jb + 1) * T, e * hd:(e + 1) * hd], eye)
            vT_ref[e, jb, dv:dv + ONES_ROWS, :] = jnp.ones((ONES_ROWS, T), BF16)
    lam = (jnp.exp(jnp.sum(lq1_ref[...] * lk1_ref[...], axis=1, keepdims=True))
           - jnp.exp(jnp.sum(lq2_ref[...] * lk2_ref[...], axis=1, keepdims=True)) + lam_init)
    kv_refs = [(functools.partial(lambda row, e: k_ref[pl.ds(row, T), :][:, e * hd:(e + 1) * hd], e=e),
                vT_ref.at[e]) for e in range(E)]

    def q_block(qi, carry):
        row0 = pl.multiple_of(qi * T, T)
        q_all = q_ref[pl.ds(row0, T), :]
        for e in range(E):
            q = q_all[:, e * hd:(e + 1) * hd]
            lane = lax.broadcasted_iota(jnp.int32, q.shape, 1)
            zero = jnp.zeros_like(q)
            for half, keep in enumerate((lane < DIFF_HEAD_DIM, lane >= DIFF_HEAD_DIM)):
                qh = jnp.where(keep, q, zero)
                for c in range(per_map):
                    qT_ref[(2 * e + half) * per_map + c] = _transpose_bf16(qh[c * CW:(c + 1) * CW, :], eye)

        _causal_attn_loop(qi, kv_refs, qT_ref, s_ref, p_ref, a_ref, m_ref, acc_ref, T=T, n_maps=2 * E,
                          order=DIFF_STAGE_ORDER)

        for e in range(E):
            for c in range(per_map):
                c1, c2 = 2 * e * per_map + c, (2 * e + 1) * per_map + c
                o1 = acc_ref[c1, 0:dv, :] / acc_ref[c1, dv:dv + 1, :]
                o2 = acc_ref[c2, 0:dv, :] / acc_ref[c2, dv:dv + 1, :]
                odT = o1 - lam * o2
                ms = jnp.mean(odT * odT, axis=0, keepdims=True)
                outT = (odT * lax.rsqrt(ms + DIFF_SUBLN_EPS) * sw_ref[...] * (1.0 - lam_init)).astype(BF16)
                for r in range(CW // LANES):
                    r0 = c * CW + r * LANES
                    o_stage[r0:r0 + LANES, e * hd:(e + 1) * hd] = _transpose_bf16(
                        outT[:, r * LANES:(r + 1) * LANES], eye).astype(o_stage.dtype)
        o_ref[pl.ds(row0, T), :] = o_stage[...]
        return carry

    lax.fori_loop(0, q_ref.shape[0] // T, q_block, 0)


def _diff_attn(zqkv, lq1, lk1, lq2, lk2, subln_w, *, lam_init, T):
    S = zqkv.shape[0]
    H = DIFF_HEADS
    E = DIFF_HEADS_PER_STEP
    hd = 2 * DIFF_HEAD_DIM
    cw = min(DIFF_CHUNK, T)
    nc = E * 2 * (T // cw)
    vec = lambda a: a.reshape(1, -1).astype(F32)
    small = lambda n: pl.BlockSpec((1, n), lambda g: (0, 0))
    return pl.pallas_call(
        functools.partial(_diff_attn_kernel, T=T, lam_init=lam_init),
        grid=(H // E,),
        in_specs=[
            pl.BlockSpec((S, E * hd), lambda g: (0, g)),
            pl.BlockSpec((S, E * hd), lambda g: (0, H // E + g)),
            pl.BlockSpec((S, E * hd), lambda g: (0, 2 * (H // E) + g)),
            small(DIFF_HEAD_DIM), small(DIFF_HEAD_DIM), small(DIFF_HEAD_DIM), small(DIFF_HEAD_DIM),
            pl.BlockSpec((hd, 1), lambda g: (0, 0)),
        ],
        out_specs=pl.BlockSpec((S, E * hd), lambda g: (0, g)),
        out_shape=jax.ShapeDtypeStruct((S, H * hd), BF16),
        scratch_shapes=[pltpu.VMEM((E, S // T, hd + ONES_ROWS, T), BF16), pltpu.VMEM((nc, hd, cw), BF16),
                        pltpu.VMEM((2, nc, T, cw), F32), pltpu.VMEM((2, nc, T, cw), BF16),
                        pltpu.VMEM((2, nc, 1, cw), F32), pltpu.VMEM((nc, 1, cw), F32),
                        pltpu.VMEM((nc, hd + ONES_ROWS, cw), F32), pltpu.VMEM((T, E * hd), BF16)],
        compiler_params=_cparams(1),
        name="diff_attention",
    )(zqkv, zqkv, zqkv, vec(lq1), vec(lk1), vec(lq2), vec(lk2), subln_w.reshape(-1, 1).astype(F32))


MLA_HEADS_PER_STEP = 2


def _mla_attn_kernel(qn_ref, qr_ref, kv_ref, kr_ref, o_ref, kcat, vT_ref, qT_ref, s_ref, p_ref, a_ref,
                     m_ref, acc_ref, o_stage, *, T):
    eye = _eye_bf16(LANES)
    dv = MLA_V_DIM
    CW = qT_ref.shape[2]
    per_map = T // CW
    pair_w = MLA_NOPE_DIM + MLA_V_DIM

    for e in range(MLA_HEADS_PER_STEP):
        kcat[e, :, 0:MLA_NOPE_DIM] = kv_ref[:, e * pair_w:e * pair_w + MLA_NOPE_DIM]
        kcat[e, :, MLA_NOPE_DIM:] = kr_ref[...]
        for jb in range(vT_ref.shape[1]):
            v_blk = kv_ref[jb * T:(jb + 1) * T, e * pair_w + MLA_NOPE_DIM:(e + 1) * pair_w]
            vT_ref[e, jb, 0:dv, :] = _transpose_bf16(v_blk, eye)
            vT_ref[e, jb, dv:dv + ONES_ROWS, :] = jnp.ones((ONES_ROWS, T), BF16)

    def q_block(qi, carry):
        row0 = pl.multiple_of(qi * T, T)
        qr = qr_ref[pl.ds(row0, T), :]
        lane = lax.broadcasted_iota(jnp.int32, qr.shape, 1)
        qn_all = qn_ref[pl.ds(row0, T), :]
        for e in range(MLA_HEADS_PER_STEP):
            mine = jnp.logical_and(lane >= e * MLA_ROPE_DIM, lane < (e + 1) * MLA_ROPE_DIM)
            qn = qn_all[:, e * MLA_NOPE_DIM:(e + 1) * MLA_NOPE_DIM]
            qrm = jnp.where(mine, qr, jnp.zeros_like(qr))
            for c in range(per_map):
                qT_ref[e * per_map + c, 0:MLA_NOPE_DIM, :] = _transpose_bf16(qn[c * CW:(c + 1) * CW, :], eye)
                qT_ref[e * per_map + c, MLA_NOPE_DIM:, :] = _transpose_bf16(qrm[c * CW:(c + 1) * CW, :], eye)

        kv_refs = [(functools.partial(lambda row, e: kcat[e, pl.ds(row, T), :], e=e), vT_ref.at[e])
                   for e in range(MLA_HEADS_PER_STEP)]
        _causal_attn_loop(qi, kv_refs, qT_ref, s_ref, p_ref, a_ref, m_ref, acc_ref, T=T,
                          n_maps=MLA_HEADS_PER_STEP, order=MLA_STAGE_ORDER)

        for e in range(MLA_HEADS_PER_STEP):
            for c in range(per_map):
                cc = e * per_map + c
                oT = (acc_ref[cc, 0:dv, :] / acc_ref[cc, dv:dv + 1, :]).astype(BF16)
                for r in range(CW // LANES):
                    r0 = c * CW + r * LANES
                    o_stage[r0:r0 + LANES, e * dv:(e + 1) * dv] = _transpose_bf16(
                        oT[:, r * LANES:(r + 1) * LANES], eye).astype(o_stage.dtype)
        o_ref[pl.ds(row0, T), :] = o_stage[...]
        return carry

    lax.fori_loop(0, qn_ref.shape[0] // T, q_block, 0)


def _mla_attn(qm, kv, kr_dup, *, T):
    S = qm.shape[0]
    H = MLA_HEADS
    E = MLA_HEADS_PER_STEP
    cw = min(MLA_CHUNK, T)
    nc = E * (T // cw)
    return pl.pallas_call(
        functools.partial(_mla_attn_kernel, T=T),
        grid=(H // E,),
        in_specs=[
            pl.BlockSpec((S, E * MLA_NOPE_DIM), lambda g: (0, g)),
            pl.BlockSpec((S, LANES), lambda g: (0, H + g)),
            pl.BlockSpec((S, E * (MLA_NOPE_DIM + MLA_V_DIM)), lambda g: (0, g)),
            pl.BlockSpec((S, LANES), lambda g: (0, 0)),
        ],
        out_specs=pl.BlockSpec((S, E * MLA_V_DIM), lambda g: (0, g)),
        out_shape=jax.ShapeDtypeStruct((S, H * MLA_V_DIM), BF16),
        scratch_shapes=[pltpu.VMEM((E, S, MLA_NOPE_DIM + LANES), BF16),
                        pltpu.VMEM((E, S // T, MLA_V_DIM + ONES_ROWS, T), BF16),
                        pltpu.VMEM((nc, MLA_NOPE_DIM + LANES, cw), BF16),
                        pltpu.VMEM((2, nc, T, cw), F32), pltpu.VMEM((2, nc, T, cw), BF16),
                        pltpu.VMEM((2, nc, 1, cw), F32), pltpu.VMEM((nc, 1, cw), F32),
                        pltpu.VMEM((nc, MLA_V_DIM + ONES_ROWS, cw), F32),
                        pltpu.VMEM((T, E * MLA_V_DIM), BF16)],
        compiler_params=_cparams(1, V7X_VMEM_LIMIT_MAX_BYTES),
        name="mla_attention",
    )(qm, qm, kv, kr_dup)


def _block_forward(x2d, pos_col, l, norm_mix_w, w_in, lq1, lk1, lq2, lk2, subln_w, q_norm_w, w_uq,
                   kv_norm_w, w_ukv, w_o_diff, w_o_mla, w_out, norm_ffn_w, w_up, conv_w, conv_b, w_down,
                   *, tm=1024, tn=512, t_attn=512, tn_ffn=256, tm_down=512, tm_lat=2048, tn_lat=1024):
    S, D = x2d.shape
    H = DIFF_HEADS
    qkv_w = 3 * H * 2 * DIFF_HEAD_DIM
    q_rank = w_uq.shape[0]
    kv_rank = w_ukv.shape[0]
    lat_w = q_rank + kv_rank
    main_w = qkv_w + lat_w
    gate_start = main_w + MLA_ROPE_DIM
    lam_init = 0.8 - 0.6 * math.exp(-0.3 * l)

    cos, sin = _rope_tables(pos_col)
    h = _rmsnorm(x2d, norm_mix_w, BF16)

    w_in_t = jnp.swapaxes(w_in, 0, 1)
    z = _mm([dict(a=h, a_blk=0, K=D, w=w_in_t, w_row_blk=0, w_col_blk=0)], N=main_w, tm=tm, tn=tn,
            out_dtype=BF16, name="in_proj_main", epilogue="rope_lt",
            epi_arg=((2 * H * 2 * DIFF_HEAD_DIM) // tn, (H * 2 * DIFF_HEAD_DIM) // tn,
                     DIFF_HEAD_DIM ** -0.5 * LOG2_E), rope=(cos, sin), w_transposed=True)
    kr_dup = _mm([dict(a=h, a_blk=0, K=D, w=w_in_t, w_row_blk=0, w_col_blk=main_w // LANES)], N=LANES,
                 tm=tm, tn=LANES, out_dtype=BF16, name="in_proj_krope", epilogue="krope_dup",
                 rope=(cos, sin), w_transposed=True)
    gates = _mm([dict(a=h, a_blk=0, K=D, w=w_in_t, w_row_blk=0, w_elem_off=gate_start)], N=2 * D,
                tm=tm, tn=tn, out_dtype=BF16, name="in_proj_gates", epilogue="sigmoid",
                w_transposed=True)

    o_d = _diff_attn(z, lq1, lk1, lq2, lk2, subln_w, lam_init=lam_init, T=t_attn)

    qk_dim = MLA_NOPE_DIM + MLA_ROPE_DIM
    w_uq3 = w_uq.reshape(q_rank, MLA_HEADS, qk_dim)
    w_uq_perm = jnp.concatenate([w_uq3[:, :, :MLA_NOPE_DIM].reshape(q_rank, -1),
                                 w_uq3[:, :, MLA_NOPE_DIM:].reshape(q_rank, -1)], axis=1)
    qm = _mm([dict(a=z, a_blk=qkv_w // q_rank, K=q_rank, w=w_uq_perm, w_row_blk=0, w_col_blk=0)],
             N=MLA_HEADS * qk_dim, tm=tm_lat, tn=tn_lat, out_dtype=BF16, name="mla_q_up", norm_w=q_norm_w,
             epilogue="scale_rope_ge",
             epi_arg=(qk_dim ** -0.5 * LOG2_E, (MLA_HEADS * MLA_NOPE_DIM) // tn_lat), rope=(cos, sin))
    kv = _mm([dict(a=z, a_blk=(qkv_w + q_rank) // kv_rank, K=kv_rank, w=w_ukv, w_row_blk=0, w_col_blk=0)],
             N=w_ukv.shape[1], tm=tm_lat, tn=tn_lat, out_dtype=BF16, name="mla_kv_up", norm_w=kv_norm_w)
    o_m = _mla_attn(qm, kv, kr_dup, T=t_attn)

    y = _mm([dict(a=o_d, a_blk=0, K=o_d.shape[1], w=w_o_diff, w_row_blk=0, w_col_blk=0, gate=(gates, 0)),
             dict(a=o_m, a_blk=0, K=o_m.shape[1], w=w_o_mla, w_row_blk=0, w_col_blk=0, gate=(gates, D // tn))],
            N=D, tm=tm, tn=tn, out_dtype=BF16, name="branch_merge")
    x1 = _mm([dict(a=y, a_blk=0, K=D, w=w_out, w_row_blk=0, w_col_blk=0)], N=D, tm=tm, tn=tn,
             out_dtype=F32, name="out_proj", addend=x2d)

    h2 = _rmsnorm(x1, norm_ffn_w, BF16)
    act = _ffn_up(h2, w_up, conv_w, conv_b, tm=tm, tn=tn_ffn)
    d_ff = act.shape[1]
    k_half = d_ff // 2
    p0 = _mm([dict(a=act, a_blk=0, K=k_half, w=w_down, w_row_blk=0, w_col_blk=0)], N=D, tm=tm_down, tn=tn,
             out_dtype=F32, name="ffn_down_lo", addend=x1)
    x2 = _mm([dict(a=act, a_blk=1, K=k_half, w=w_down, w_row_blk=1, w_col_blk=0)], N=D, tm=tm_down, tn=tn,
             out_dtype=F32, name="ffn_down_hi", addend=p0)
    return x2


def kernel(x, positions, norm_mix_w, w_in, diff_lambda_q1, diff_lambda_k1, diff_lambda_q2, diff_lambda_k2, diff_subln_w, mla_q_norm_w, mla_w_uq, mla_kv_norm_w, mla_w_ukv, w_o_diff, w_o_mla, w_out, norm_ffn_w, ffn_w_up, ffn_conv_w, ffn_conv_b, ffn_w_down, final_norm_w):
    B, S, D = x.shape
    assert B == 1
    x2d = x.reshape(S, D)
    pos_col = positions.reshape(S, 1)
    for l in range(w_in.shape[0]):
        x2d = _block_forward(
            x2d, pos_col, l, norm_mix_w[l], w_in[l], diff_lambda_q1[l], diff_lambda_k1[l], diff_lambda_q2[l],
            diff_lambda_k2[l], diff_subln_w[l], mla_q_norm_w[l], mla_w_uq[l], mla_kv_norm_w[l], mla_w_ukv[l],
            w_o_diff[l], w_o_mla[l], w_out[l], norm_ffn_w[l], ffn_w_up[l], ffn_conv_w[l], ffn_conv_b[l],
            ffn_w_down[l])
    out = _rmsnorm(x2d, final_norm_w, F32)
    return out.reshape(B, S, D)
```

```python
import functools
import math

import jax
import jax.numpy as jnp
from jax import lax
from jax.experimental import pallas as pl
from jax.experimental.pallas import tpu as pltpu

BF16 = jnp.bfloat16
F32 = jnp.float32

LANES = 128
V7X_VMEM_LIMIT_BYTES = 56 << 20
V7X_VMEM_LIMIT_MAX_BYTES = 62 << 20

DIFF_HEADS = 16
DIFF_HEAD_DIM = 64
MLA_HEADS = 16
MLA_NOPE_DIM = 128
MLA_ROPE_DIM = 64
MLA_V_DIM = 128
ROPE_THETA = 10000.0
NORM_EPS = 1e-6
DIFF_SUBLN_EPS = 1e-5
CONV_WIDTH = 3
MASK_VALUE = -1e30
LOG2_E = math.log2(math.e)


def _cparams(n_axes, vmem_limit_bytes=V7X_VMEM_LIMIT_BYTES):
    return pltpu.CompilerParams(
        dimension_semantics=("arbitrary",) * n_axes,
        vmem_limit_bytes=vmem_limit_bytes,
    )


def _rope_table_kernel(pos_ref, freq_ref, sign_ref, cos_ref, sin_ref):
    ang = pos_ref[...].astype(F32) * freq_ref[...]
    cos_ref[...] = jnp.cos(ang)
    sin_ref[...] = jnp.sin(ang) * sign_ref[...]


def _rope_tables(pos_col, tm=2048):
    S = pos_col.shape[0]
    half = MLA_ROPE_DIM // 2
    inv_freq = ROPE_THETA ** (-jnp.arange(0, MLA_ROPE_DIM, 2, dtype=F32) / MLA_ROPE_DIM)
    freq = jnp.tile(inv_freq, LANES // half).reshape(1, LANES)
    sign = jnp.tile(jnp.concatenate([-jnp.ones((half,), F32), jnp.ones((half,), F32)]),
                    LANES // (2 * half)).reshape(1, LANES)
    return pl.pallas_call(
        _rope_table_kernel,
        grid=(S // tm,),
        in_specs=[pl.BlockSpec((tm, 1), lambda i: (i, 0)),
                  pl.BlockSpec((1, LANES), lambda i: (0, 0)),
                  pl.BlockSpec((1, LANES), lambda i: (0, 0))],
        out_specs=[pl.BlockSpec((tm, LANES), lambda i: (i, 0)),
                   pl.BlockSpec((tm, LANES), lambda i: (i, 0))],
        out_shape=[jax.ShapeDtypeStruct((S, LANES), F32)] * 2,
        compiler_params=_cparams(1),
        name="rope_tables",
    )(pos_col, freq, sign)


def _rope_partner(zc):
    lane = lax.broadcasted_iota(jnp.int32, zc.shape, 1)
    first_half = (lane & 32) == 0
    return jnp.where(first_half, pltpu.roll(zc, 96, 1), pltpu.roll(zc, 32, 1))


def _rope_lanes(z, cos, sin):
    outs = []
    for c in range(z.shape[1] // LANES):
        zc = z[:, c * LANES:(c + 1) * LANES]
        outs.append(zc * cos + _rope_partner(zc) * sin)
    return outs[0] if len(outs) == 1 else jnp.concatenate(outs, axis=1)


def _rmsnorm_kernel(x_ref, w_ref, o_ref, *, eps):
    xf = x_ref[...].astype(F32)
    ms = jnp.mean(xf * xf, axis=1, keepdims=True)
    o_ref[...] = (xf * lax.rsqrt(ms + eps) * w_ref[...]).astype(o_ref.dtype)


def _rmsnorm(x, w, out_dtype, eps=NORM_EPS, tm=512):
    M, D = x.shape
    return pl.pallas_call(
        functools.partial(_rmsnorm_kernel, eps=eps),
        grid=(M // tm,),
        in_specs=[pl.BlockSpec((tm, D), lambda i: (i, 0)),
                  pl.BlockSpec((1, D), lambda i: (0, 0))],
        out_specs=pl.BlockSpec((tm, D), lambda i: (i, 0)),
        out_shape=jax.ShapeDtypeStruct((M, D), out_dtype),
        compiler_params=_cparams(1),
        name="rmsnorm",
    )(x, w.reshape(1, D).astype(F32))


def _tile_walk(ni, n_steps):
    def cur(s):
        c = jnp.minimum(s, n_steps - 1)
        return c % ni, c // ni

    def prev(s):
        p = jnp.maximum(s - 1, 0)
        return p % ni, p // ni

    return cur, prev


MM_SUB_ROWS = 256
MXU_DIM = 256


def _mm_kernel(*refs, n_pairs, has_gate, has_norm, has_addend, has_rope, epilogue, epi_arg, eps, ni, n_steps,
               tm, sub, w_transposed):
    refs = list(refs)
    pair_refs = []
    for _ in range(n_pairs):
        a_ref = refs.pop(0)
        w_ref = refs.pop(0)
        g_ref = refs.pop(0) if has_gate else None
        pair_refs.append((a_ref, w_ref, g_ref))
    nw_ref = refs.pop(0) if has_norm else None
    add_ref = refs.pop(0) if has_addend else None
    cos_ref = refs.pop(0) if has_rope else None
    sin_ref = refs.pop(0) if has_rope else None
    o_ref = refs.pop(0)
    wbf_refs = refs[:n_pairs]
    raw_refs = refs[n_pairs:]

    s = pl.program_id(0)
    cur, prev = _tile_walk(ni, n_steps)
    i_cur, _ = cur(s)
    _, j = prev(s)

    @pl.when(s == 0)
    def _():
        for raw in raw_refs:
            raw[...] = jnp.zeros(raw.shape, F32)

    @pl.when(i_cur == 0)
    def _():
        for (_, w_ref, _), wbf in zip(pair_refs, wbf_refs):
            if w_transposed:
                eye = _eye_bf16(MXU_DIM)
                for kc in range(wbf.shape[0] // MXU_DIM):
                    ks = slice(kc * MXU_DIM, (kc + 1) * MXU_DIM)
                    wbf[ks, :] = _nt_dot(eye, w_ref[:, ks].astype(BF16)).astype(BF16)
            else:
                wbf[...] = w_ref[...].astype(BF16)

    def finish_previous(rows, roped):
        if has_gate:
            acc = None
            for (_, _, g_ref), raw in zip(pair_refs, raw_refs):
                d = raw[rows, :] * g_ref[rows, :].astype(F32)
                acc = d if acc is None else acc + d
        else:
            acc = raw_refs[0][rows, :]
        if has_addend:
            acc = acc + add_ref[rows, :]
        if epilogue == "sigmoid":
            out = 1.0 / (1.0 + jnp.exp(-acc))
        elif epilogue == "rope_lt":
            _, n_scaled, scale = epi_arg
            out = acc
            if roped:
                out = _rope_lanes(acc, cos_ref[rows, :], sin_ref[rows, :]) * jnp.where(j < n_scaled, scale, 1.0)
        elif epilogue == "scale_rope_ge":
            out = acc * epi_arg[0]
            if roped:
                out = _rope_lanes(out, cos_ref[rows, :], sin_ref[rows, :])
        elif epilogue == "krope_dup":
            lane = lax.broadcasted_iota(jnp.int32, acc.shape, 1)
            kr = jnp.where(lane < MLA_ROPE_DIM, acc, 0.0)
            r = kr * cos_ref[rows, :] + _rope_partner(kr) * sin_ref[rows, :]
            out = r + pltpu.roll(r, MLA_ROPE_DIM, 1)
        else:
            out = acc
        o_ref[rows, :] = out.astype(o_ref.dtype)

    def multiply_current(rows):
        acc = None
        for p, ((a_ref, _, _), wbf) in enumerate(zip(pair_refs, wbf_refs)):
            a = a_ref[rows, :]
            if has_norm:
                af = a.astype(F32)
                ms = jnp.mean(af * af, axis=1, keepdims=True)
                a = (af * lax.rsqrt(ms + eps) * nw_ref[...]).astype(BF16)
            d = jnp.dot(a, wbf[...], preferred_element_type=F32)
            if has_gate:
                raw_refs[p][rows, :] = d
            else:
                acc = d if acc is None else acc + d
        if not has_gate:
            raw_refs[0][rows, :] = acc

    def step(roped):
        for r in range(tm // sub):
            rows = slice(r * sub, (r + 1) * sub)
            finish_previous(rows, roped)
            multiply_current(rows)

    if epilogue == "rope_lt":
        pl.when(j < epi_arg[0])(lambda: step(True))
        pl.when(j >= epi_arg[0])(lambda: step(False))
    elif epilogue == "scale_rope_ge":
        pl.when(j >= epi_arg[1])(lambda: step(True))
        pl.when(j < epi_arg[1])(lambda: step(False))
    else:
        step(False)


def _mm(pairs, *, N, tm, tn, out_dtype, name, epilogue="none", epi_arg=None,
        norm_w=None, addend=None, rope=None, eps=NORM_EPS, w_transposed=False):
    M = pairs[0]["a"].shape[0]
    ni, nj = M // tm, N // tn
    n_steps = ni * nj
    cur, prev = _tile_walk(ni, n_steps)
    has_gate = pairs[0].get("gate") is not None
    args, in_specs, wbf_scratch = [], [], []

    def at_cur(fn):
        return lambda s: fn(*cur(s))

    def at_prev(fn):
        return lambda s: fn(*prev(s))

    for p in pairs:
        K = p["K"]
        args.append(p["a"])
        in_specs.append(pl.BlockSpec((tm, K), at_cur(functools.partial(lambda i, j, b: (i, b), b=p["a_blk"]))))
        args.append(p["w"])
        if not w_transposed:
            in_specs.append(pl.BlockSpec((K, tn), at_cur(functools.partial(
                lambda i, j, r, c: (r, c + j), r=p["w_row_blk"], c=p["w_col_blk"]))))
        elif "w_elem_off" in p:
            in_specs.append(pl.BlockSpec((pl.Element(tn), pl.Element(K)), at_cur(functools.partial(
                lambda i, j, r, off, k: (pl.multiple_of(off + tn * j, math.gcd(off, tn)), r * k),
                r=p["w_row_blk"], off=p["w_elem_off"], k=K))))
        else:
            in_specs.append(pl.BlockSpec((tn, K), at_cur(functools.partial(
                lambda i, j, r, c: (c + j, r), r=p["w_row_blk"], c=p["w_col_blk"]))))
        if has_gate:
            g, g_off = p["gate"]
            args.append(g)
            in_specs.append(pl.BlockSpec((tm, tn), at_prev(functools.partial(lambda i, j, c: (i, c + j), c=g_off))))
        wbf_scratch.append(pltpu.VMEM((K, tn), BF16))
    if norm_w is not None:
        args.append(norm_w.reshape(1, -1).astype(F32))
        in_specs.append(pl.BlockSpec((1, norm_w.shape[-1]), lambda s: (0, 0)))
    if addend is not None:
        args.append(addend)
        in_specs.append(pl.BlockSpec((tm, tn), at_prev(lambda i, j: (i, j))))
    if rope is not None:
        for t in rope:
            args.append(t)
            in_specs.append(pl.BlockSpec((tm, LANES), at_prev(lambda i, j: (i, 0))))
    raw_scratch = [pltpu.VMEM((tm, tn), F32)] * (len(pairs) if has_gate else 1)
    kern = functools.partial(
        _mm_kernel, n_pairs=len(pairs), has_gate=has_gate, has_norm=norm_w is not None,
        has_addend=addend is not None, has_rope=rope is not None, epilogue=epilogue, epi_arg=epi_arg, eps=eps,
        ni=ni, n_steps=n_steps, tm=tm, sub=min(MM_SUB_ROWS, tm), w_transposed=w_transposed)
    return pl.pallas_call(
        kern,
        grid=(n_steps + 1,),
        in_specs=in_specs,
        out_specs=pl.BlockSpec((tm, tn), at_prev(lambda i, j: (i, j))),
        out_shape=jax.ShapeDtypeStruct((M, N), out_dtype),
        scratch_shapes=wbf_scratch + raw_scratch,
        compiler_params=_cparams(1),
        name=name,
    )(*args)


CONV_HALO = 8


def _ffn_up_kernel(a_ref, wg_ref, wv_ref, cwg_ref, cwv_ref, cbg_ref, cbv_ref, o_ref,
                   wgbf, wvbf, ug_buf, uv_buf, *, tm, ni, n_steps):
    s = pl.program_id(0)
    cur, _ = _tile_walk(ni, n_steps)
    i_cur, _ = cur(s)

    @pl.when(s == 0)
    def _():
        ug_buf[...] = jnp.zeros(ug_buf.shape, F32)
        uv_buf[...] = jnp.zeros(uv_buf.shape, F32)

    @pl.when(i_cur == 0)
    def _():
        wgbf[...] = wg_ref[...].astype(BF16)
        wvbf[...] = wv_ref[...].astype(BF16)

    sub = min(MM_SUB_ROWS, tm)

    def conv(buf, r0, cw_ref, cb_ref):
        lo = CONV_HALO + r0
        out = cb_ref[...] + cw_ref[0:1, :] * buf[lo - 2:lo - 2 + sub, :]
        out = out + cw_ref[1:2, :] * buf[lo - 1:lo - 1 + sub, :]
        return out + cw_ref[2:3, :] * buf[lo:lo + sub, :]

    keep = jnp.where(i_cur == 0, 0.0, 1.0)
    halo_g = ug_buf[tm:tm + CONV_HALO, :] * keep
    halo_v = uv_buf[tm:tm + CONV_HALO, :] * keep

    for r in reversed(range(tm // sub)):
        r0 = r * sub
        g = conv(ug_buf, r0, cwg_ref, cbg_ref)
        v = conv(uv_buf, r0, cwv_ref, cbv_ref)
        o_ref[r0:r0 + sub, :] = (g / (1.0 + jnp.exp(-g)) * v).astype(o_ref.dtype)
        a = a_ref[r0:r0 + sub, :]
        ug_buf[CONV_HALO + r0:CONV_HALO + r0 + sub, :] = jnp.dot(a, wgbf[...], preferred_element_type=F32)
        uv_buf[CONV_HALO + r0:CONV_HALO + r0 + sub, :] = jnp.dot(a, wvbf[...], preferred_element_type=F32)

    ug_buf[0:CONV_HALO, :] = halo_g
    uv_buf[0:CONV_HALO, :] = halo_v


def _ffn_up(h, w_up, conv_w, conv_b, *, tm, tn):
    M, K = h.shape
    d_ff = w_up.shape[1] // 2
    ni, nj = M // tm, d_ff // tn
    n_steps = ni * nj
    cur, prev = _tile_walk(ni, n_steps)
    cb = conv_b.reshape(1, -1)

    def at_cur(fn):
        return lambda s: fn(*cur(s))

    def at_prev(fn):
        return lambda s: fn(*prev(s))

    return pl.pallas_call(
        functools.partial(_ffn_up_kernel, tm=tm, ni=ni, n_steps=n_steps),
        grid=(n_steps + 1,),
        in_specs=[
            pl.BlockSpec((tm, K), at_cur(lambda i, j: (i, 0))),
            pl.BlockSpec((K, tn), at_cur(lambda i, j: (0, j))),
            pl.BlockSpec((K, tn), at_cur(lambda i, j: (0, nj + j))),
            pl.BlockSpec((CONV_WIDTH, tn), at_prev(lambda i, j: (0, j))),
            pl.BlockSpec((CONV_WIDTH, tn), at_prev(lambda i, j: (0, nj + j))),
            pl.BlockSpec((1, tn), at_prev(lambda i, j: (0, j))),
            pl.BlockSpec((1, tn), at_prev(lambda i, j: (0, nj + j))),
        ],
        out_specs=pl.BlockSpec((tm, tn), at_prev(lambda i, j: (i, j))),
        out_shape=jax.ShapeDtypeStruct((M, d_ff), BF16),
        scratch_shapes=[pltpu.VMEM((K, tn), BF16), pltpu.VMEM((K, tn), BF16),
                        pltpu.VMEM((CONV_HALO + tm, tn), F32), pltpu.VMEM((CONV_HALO + tm, tn), F32)],
        compiler_params=_cparams(1),
        name="ffn_up_conv_gate",
    )(h, w_up, w_up, conv_w, conv_w, cb, cb)


DIFF_CHUNK = 256
MLA_CHUNK = 256
DIFF_STAGE_ORDER = "q|sp"
MLA_STAGE_ORDER = "q|sp"


def _nt_dot(a, b):
    return lax.dot_general(a, b, (((1,), (1,)), ((), ())), preferred_element_type=F32)


def _eye_bf16(n):
    r = lax.broadcasted_iota(jnp.int32, (n, n), 0)
    c = lax.broadcasted_iota(jnp.int32, (n, n), 1)
    return jnp.where(r == c, 1.0, 0.0).astype(BF16)


def _transpose_bf16(x, eye):
    return _nt_dot(eye, x).astype(BF16)


SUBLANES = 8
REDUCE_WAYS = 8


def _reduce_rows(x, op, final):
    n = x.shape[0]
    groups = [x[r * SUBLANES:(r + 1) * SUBLANES] for r in range(n // SUBLANES)]
    ways = min(REDUCE_WAYS, len(groups))
    parts = groups[:ways]
    for g, blk in enumerate(groups[ways:]):
        parts[g % ways] = op(parts[g % ways], blk)
    while len(parts) > 1:
        parts = [op(parts[i], parts[i + 1]) if i + 1 < len(parts) else parts[i] for i in range(0, len(parts), 2)]
    return final(parts[0], axis=0, keepdims=True)


def _softmax_chunk(load_s, m_ref, c, mask_q0):
    def scores():
        sT = load_s()
        if mask_q0 is not None:
            key = lax.broadcasted_iota(jnp.int32, sT.shape, 0)
            qq = lax.broadcasted_iota(jnp.int32, sT.shape, 1) + mask_q0
            sT = jnp.where(qq >= key, sT, MASK_VALUE)
        return sT

    m_prev = m_ref[c]
    m_new = jnp.maximum(m_prev, _reduce_rows(scores(), jnp.maximum, jnp.max))
    p = jnp.exp2(scores() - m_new)
    alpha = jnp.exp2(m_prev - m_new)
    m_ref[c] = m_new
    return p.astype(BF16), alpha


ONES_ROWS = 16


def _causal_attn_loop(qi, kv_refs, qT_ref, s_ref, p_ref, a_ref, m_ref, acc_ref, *, T, n_maps, order):
    CW = qT_ref.shape[2]
    per_map = T // CW
    chunks = [(c, (c % per_map) * CW) for c in range(n_maps * per_map)]
    m_ref[...] = jnp.full(m_ref.shape, MASK_VALUE, F32)
    acc_ref[...] = jnp.zeros(acc_ref.shape, F32)

    maps_per_kv = n_maps // len(kv_refs)

    def kv_of(c):
        return kv_refs[c // (per_map * maps_per_kv)]

    def qk(blk, slot, c):
        k = kv_of(c)[0](pl.multiple_of(blk * T, T))
        s_ref[slot, c] = jnp.dot(k, qT_ref[c], preferred_element_type=F32)

    def softmax(slot, c, q0, diagonal):
        n = q0 + CW if diagonal else T
        p, alpha = _softmax_chunk(lambda: s_ref[slot, c, 0:n, :], m_ref, c, q0 if diagonal else None)
        p_ref[slot, c, 0:n, :] = p
        a_ref[slot, c] = alpha

    def pv(blk, slot, c, q0, diagonal):
        n = q0 + CW if diagonal else T
        acc_ref[c] = acc_ref[c] * a_ref[slot, c] + jnp.dot(
            kv_of(c)[1][blk, :, 0:n], p_ref[slot, c, 0:n, :], preferred_element_type=F32)

    def step(t, slot):
        stage = {"q": lambda c, q0: qk(t, slot, c),
                 "s": lambda c, q0: softmax(1 - slot, c, q0, False),
                 "p": lambda c, q0: pv(t - 2, slot, c, q0, False)}
        for group in order.split("|"):
            for c, q0 in chunks:
                for name in group:
                    stage[name](c, q0)

    def drain(slot):
        for c, q0 in chunks:
            softmax(slot, c, q0, True)
            pv(qi, slot, c, q0, True)

    @pl.when(qi == 0)
    def _():
        for c, _ in chunks:
            qk(0, 0, c)
        drain(0)

    @pl.when(qi >= 1)
    def _():
        for c, _ in chunks:
            qk(0, 0, c)
        for c, q0 in chunks:
            qk(1, 1, c)
            softmax(0, c, q0, False)

    def body(u, carry):
        t = 2 + 2 * u
        step(t, 0)
        step(t + 1, 1)
        return carry

    lax.fori_loop(0, lax.shift_right_arithmetic(qi - 1, 1), body, 0)

    @pl.when(jnp.logical_and(qi >= 2, qi % 2 == 0))
    def _():
        step(qi, 0)
        for c, q0 in chunks:
            pv(qi - 1, 1, c, q0, False)
        drain(0)

    @pl.when(qi % 2 == 1)
    def _():
        for c, q0 in chunks:
            pv(qi - 1, 0, c, q0, False)
        drain(1)


DIFF_HEADS_PER_STEP = 1


def _diff_attn_kernel(q_ref, k_ref, v_ref, lq1_ref, lk1_ref, lq2_ref, lk2_ref, sw_ref, o_ref,
                      vT_ref, qT_ref, s_ref, p_ref, a_ref, m_ref, acc_ref, o_stage, *, T, lam_init):
    eye = _eye_bf16(LANES)
    hd = 2 * DIFF_HEAD_DIM
    dv = hd
    CW = qT_ref.shape[2]
    per_map = T // CW
    E = DIFF_HEADS_PER_STEP
    for e in range(E):
        for jb in range(vT_ref.shape[1]):
            vT_ref[e, jb, 0:dv, :] = _transpose_bf16(v_ref[jb * T:(jb + 1) * T, e * hd:(e + 1) * hd], eye)
            vT_ref[e, jb, dv:dv + ONES_ROWS, :] = jnp.ones((ONES_ROWS, T), BF16)
    lam = (jnp.exp(jnp.sum(lq1_ref[...] * lk1_ref[...], axis=1, keepdims=True))
           - jnp.exp(jnp.sum(lq2_ref[...] * lk2_ref[...], axis=1, keepdims=True)) + lam_init)
    kv_refs = [(functools.partial(lambda row, e: k_ref[pl.ds(row, T), :][:, e * hd:(e + 1) * hd], e=e),
                vT_ref.at[e]) for e in range(E)]

    def q_block(qi, carry):
        row0 = pl.multiple_of(qi * T, T)
        q_all = q_ref[pl.ds(row0, T), :]
        for e in range(E):
            q = q_all[:, e * hd:(e + 1) * hd]
            lane = lax.broadcasted_iota(jnp.int32, q.shape, 1)
            zero = jnp.zeros_like(q)
            for half, keep in enumerate((lane < DIFF_HEAD_DIM, lane >= DIFF_HEAD_DIM)):
                qh = jnp.where(keep, q, zero)
                for c in range(per_map):
                    qT_ref[(2 * e + half) * per_map + c] = _transpose_bf16(qh[c * CW:(c + 1) * CW, :], eye)

        _causal_attn_loop(qi, kv_refs, qT_ref, s_ref, p_ref, a_ref, m_ref, acc_ref, T=T, n_maps=2 * E,
                          order=DIFF_STAGE_ORDER)

        for e in range(E):
            for c in range(per_map):
                c1, c2 = 2 * e * per_map + c, (2 * e + 1) * per_map + c
                o1 = acc_ref[c1, 0:dv, :] / acc_ref[c1, dv:dv + 1, :]
                o2 = acc_ref[c2, 0:dv, :] / acc_ref[c2, dv:dv + 1, :]
                odT = o1 - lam * o2
                ms = jnp.mean(odT * odT, axis=0, keepdims=True)
                outT = (odT * lax.rsqrt(ms + DIFF_SUBLN_EPS) * sw_ref[...] * (1.0 - lam_init)).astype(BF16)
                for r in range(CW // LANES):
                    r0 = c * CW + r * LANES
                    o_stage[r0:r0 + LANES, e * hd:(e + 1) * hd] = _transpose_bf16(
                        outT[:, r * LANES:(r + 1) * LANES], eye).astype(o_stage.dtype)
        o_ref[pl.ds(row0, T), :] = o_stage[...]
        return carry

    lax.fori_loop(0, q_ref.shape[0] // T, q_block, 0)


def _diff_attn(zqkv, lq1, lk1, lq2, lk2, subln_w, *, lam_init, T):
    S = zqkv.shape[0]
    H = DIFF_HEADS
    E = DIFF_HEADS_PER_STEP
    hd = 2 * DIFF_HEAD_DIM
    cw = min(DIFF_CHUNK, T)
    nc = E * 2 * (T // cw)
    vec = lambda a: a.reshape(1, -1).astype(F32)
    small = lambda n: pl.BlockSpec((1, n), lambda g: (0, 0))
    return pl.pallas_call(
        functools.partial(_diff_attn_kernel, T=T, lam_init=lam_init),
        grid=(H // E,),
        in_specs=[
            pl.BlockSpec((S, E * hd), lambda g: (0, g)),
            pl.BlockSpec((S, E * hd), lambda g: (0, H // E + g)),
            pl.BlockSpec((S, E * hd), lambda g: (0, 2 * (H // E) + g)),
            small(DIFF_HEAD_DIM), small(DIFF_HEAD_DIM), small(DIFF_HEAD_DIM), small(DIFF_HEAD_DIM),
            pl.BlockSpec((hd, 1), lambda g: (0, 0)),
        ],
        out_specs=pl.BlockSpec((S, E * hd), lambda g: (0, g)),
        out_shape=jax.ShapeDtypeStruct((S, H * hd), BF16),
        scratch_shapes=[pltpu.VMEM((E, S // T, hd + ONES_ROWS, T), BF16), pltpu.VMEM((nc, hd, cw), BF16),
                        pltpu.VMEM((2, nc, T, cw), F32), pltpu.VMEM((2, nc, T, cw), BF16),
                        pltpu.VMEM((2, nc, 1, cw), F32), pltpu.VMEM((nc, 1, cw), F32),
                        pltpu.VMEM((nc, hd + ONES_ROWS, cw), F32), pltpu.VMEM((T, E * hd), BF16)],
        compiler_params=_cparams(1),
        name="diff_attention",
    )(zqkv, zqkv, zqkv, vec(lq1), vec(lk1), vec(lq2), vec(lk2), subln_w.reshape(-1, 1).astype(F32))


MLA_HEADS_PER_STEP = 2


def _mla_attn_kernel(qn_ref, qr_ref, kv_ref, kr_ref, o_ref, kcat, vT_ref, qT_ref, s_ref, p_ref, a_ref,
                     m_ref, acc_ref, o_stage, *, T):
    eye = _eye_bf16(LANES)
    dv = MLA_V_DIM
    CW = qT_ref.shape[2]
    per_map = T // CW
    pair_w = MLA_NOPE_DIM + MLA_V_DIM

    for e in range(MLA_HEADS_PER_STEP):
        kcat[e, :, 0:MLA_NOPE_DIM] = kv_ref[:, e * pair_w:e * pair_w + MLA_NOPE_DIM]
        kcat[e, :, MLA_NOPE_DIM:] = kr_ref[...]
        for jb in range(vT_ref.shape[1]):
            v_blk = kv_ref[jb * T:(jb + 1) * T, e * pair_w + MLA_NOPE_DIM:(e + 1) * pair_w]
            vT_ref[e, jb, 0:dv, :] = _transpose_bf16(v_blk, eye)
            vT_ref[e, jb, dv:dv + ONES_ROWS, :] = jnp.ones((ONES_ROWS, T), BF16)

    def q_block(qi, carry):
        row0 = pl.multiple_of(qi * T, T)
        qr = qr_ref[pl.ds(row0, T), :]
        lane = lax.broadcasted_iota(jnp.int32, qr.shape, 1)
        qn_all = qn_ref[pl.ds(row0, T), :]
        for e in range(MLA_HEADS_PER_STEP):
            mine = jnp.logical_and(lane >= e * MLA_ROPE_DIM, lane < (e + 1) * MLA_ROPE_DIM)
            qn = qn_all[:, e * MLA_NOPE_DIM:(e + 1) * MLA_NOPE_DIM]
            qrm = jnp.where(mine, qr, jnp.zeros_like(qr))
            for c in range(per_map):
                qT_ref[e * per_map + c, 0:MLA_NOPE_DIM, :] = _transpose_bf16(qn[c * CW:(c + 1) * CW, :], eye)
                qT_ref[e * per_map + c, MLA_NOPE_DIM:, :] = _transpose_bf16(qrm[c * CW:(c + 1) * CW, :], eye)

        kv_refs = [(functools.partial(lambda row, e: kcat[e, pl.ds(row, T), :], e=e), vT_ref.at[e])
                   for e in range(MLA_HEADS_PER_STEP)]
        _causal_attn_loop(qi, kv_refs, qT_ref, s_ref, p_ref, a_ref, m_ref, acc_ref, T=T,
                          n_maps=MLA_HEADS_PER_STEP, order=MLA_STAGE_ORDER)

        for e in range(MLA_HEADS_PER_STEP):
            for c in range(per_map):
                cc = e * per_map + c
                oT = (acc_ref[cc, 0:dv, :] / acc_ref[cc, dv:dv + 1, :]).astype(BF16)
                for r in range(CW // LANES):
                    r0 = c * CW + r * LANES
                    o_stage[r0:r0 + LANES, e * dv:(e + 1) * dv] = _transpose_bf16(
                        oT[:, r * LANES:(r + 1) * LANES], eye).astype(o_stage.dtype)
        o_ref[pl.ds(row0, T), :] = o_stage[...]
        return carry

    lax.fori_loop(0, qn_ref.shape[0] // T, q_block, 0)


def _mla_attn(qm, kv, kr_dup, *, T):
    S = qm.shape[0]
    H = MLA_HEADS
    E = MLA_HEADS_PER_STEP
    cw = min(MLA_CHUNK, T)
    nc = E * (T // cw)
    return pl.pallas_call(
        functools.partial(_mla_attn_kernel, T=T),
        grid=(H // E,),
        in_specs=[
            pl.BlockSpec((S, E * MLA_NOPE_DIM), lambda g: (0, g)),
            pl.BlockSpec((S, LANES), lambda g: (0, H + g)),
            pl.BlockSpec((S, E * (MLA_NOPE_DIM + MLA_V_DIM)), lambda g: (0, g)),
            pl.BlockSpec((S, LANES), lambda g: (0, 0)),
        ],
        out_specs=pl.BlockSpec((S, E * MLA_V_DIM), lambda g: (0, g)),
        out_shape=jax.ShapeDtypeStruct((S, H * MLA_V_DIM), BF16),
        scratch_shapes=[pltpu.VMEM((E, S, MLA_NOPE_DIM + LANES), BF16),
                        pltpu.VMEM((E, S // T, MLA_V_DIM + ONES_ROWS, T), BF16),
                        pltpu.VMEM((nc, MLA_NOPE_DIM + LANES, cw), BF16),
                        pltpu.VMEM((2, nc, T, cw), F32), pltpu.VMEM((2, nc, T, cw), BF16),
                        pltpu.VMEM((2, nc, 1, cw), F32), pltpu.VMEM((nc, 1, cw), F32),
                        pltpu.VMEM((nc, MLA_V_DIM + ONES_ROWS, cw), F32),
                        pltpu.VMEM((T, E * MLA_V_DIM), BF16)],
        compiler_params=_cparams(1, V7X_VMEM_LIMIT_MAX_BYTES),
        name="mla_attention",
    )(qm, qm, kv, kr_dup)


def _block_forward(x2d, pos_col, l, norm_mix_w, w_in, lq1, lk1, lq2, lk2, subln_w, q_norm_w, w_uq,
                   kv_norm_w, w_ukv, w_o_diff, w_o_mla, w_out, norm_ffn_w, w_up, conv_w, conv_b, w_down,
                   *, tm=1024, tn=512, t_attn=512, tn_ffn=256, tm_down=512, tm_lat=2048, tn_lat=1024):
    S, D = x2d.shape
    H = DIFF_HEADS
    qkv_w = 3 * H * 2 * DIFF_HEAD_DIM
    q_rank = w_uq.shape[0]
    kv_rank = w_ukv.shape[0]
    lat_w = q_rank + kv_rank
    main_w = qkv_w + lat_w
    gate_start = main_w + MLA_ROPE_DIM
    lam_init = 0.8 - 0.6 * math.exp(-0.3 * l)

    cos, sin = _rope_tables(pos_col)
    h = _rmsnorm(x2d, norm_mix_w, BF16)

    w_in_t = jnp.swapaxes(w_in, 0, 1)
    z = _mm([dict(a=h, a_blk=0, K=D, w=w_in_t, w_row_blk=0, w_col_blk=0)], N=main_w, tm=tm, tn=tn,
            out_dtype=BF16, name="in_proj_main", epilogue="rope_lt",
            epi_arg=((2 * H * 2 * DIFF_HEAD_DIM) // tn, (H * 2 * DIFF_HEAD_DIM) // tn,
                     DIFF_HEAD_DIM ** -0.5 * LOG2_E), rope=(cos, sin), w_transposed=True)
    kr_dup = _mm([dict(a=h, a_blk=0, K=D, w=w_in_t, w_row_blk=0, w_col_blk=main_w // LANES)], N=LANES,
                 tm=tm, tn=LANES, out_dtype=BF16, name="in_proj_krope", epilogue="krope_dup",
                 rope=(cos, sin), w_transposed=True)
    gates = _mm([dict(a=h, a_blk=0, K=D, w=w_in_t, w_row_blk=0, w_elem_off=gate_start)], N=2 * D,
                tm=tm, tn=tn, out_dtype=BF16, name="in_proj_gates", epilogue="sigmoid",
                w_transposed=True)

    o_d = _diff_attn(z, lq1, lk1, lq2, lk2, subln_w, lam_init=lam_init, T=t_attn)

    qk_dim = MLA_NOPE_DIM + MLA_ROPE_DIM
    w_uq3 = w_uq.reshape(q_rank, MLA_HEADS, qk_dim)
    w_uq_perm = jnp.concatenate([w_uq3[:, :, :MLA_NOPE_DIM].reshape(q_rank, -1),
                                 w_uq3[:, :, MLA_NOPE_DIM:].reshape(q_rank, -1)], axis=1)
    qm = _mm([dict(a=z, a_blk=qkv_w // q_rank, K=q_rank, w=w_uq_perm, w_row_blk=0, w_col_blk=0)],
             N=MLA_HEADS * qk_dim, tm=tm_lat, tn=tn_lat, out_dtype=BF16, name="mla_q_up", norm_w=q_norm_w,
             epilogue="scale_rope_ge",
             epi_arg=(qk_dim ** -0.5 * LOG2_E, (MLA_HEADS * MLA_NOPE_DIM) // tn_lat), rope=(cos, sin))
    kv = _mm([dict(a=z, a_blk=(qkv_w + q_rank) // kv_rank, K=kv_rank, w=w_ukv, w_row_blk=0, w_col_blk=0)],
             N=w_ukv.shape[1], tm=tm_lat, tn=tn_lat, out_dtype=BF16, name="mla_kv_up", norm_w=kv_norm_w)
    o_m = _mla_attn(qm, kv, kr_dup, T=t_attn)

    y = _mm([dict(a=o_d, a_blk=0, K=o_d.shape[1], w=w_o_diff, w_row_blk=0, w_col_blk=0, gate=(gates, 0)),
             dict(a=o_m, a_blk=0, K=o_m.shape[1], w=w_o_mla, w_row_blk=0, w_col_blk=0, gate=(gates, D // tn))],
            N=D, tm=tm, tn=tn, out_dtype=BF16, name="branch_merge")
    x1 = _mm([dict(a=y, a_blk=0, K=D, w=w_out, w_row_blk=0, w_col_blk=0)], N=D, tm=tm, tn=tn,
             out_dtype=F32, name="out_proj", addend=x2d)

    h2 = _rmsnorm(x1, norm_ffn_w, BF16)
    act = _ffn_up(h2, w_up, conv_w, conv_b, tm=tm, tn=tn_ffn)
    d_ff = act.shape[1]
    k_half = d_ff // 2
    p0 = _mm([dict(a=act, a_blk=0, K=k_half, w=w_down, w_row_blk=0, w_col_blk=0)], N=D, tm=tm_down, tn=tn,
             out_dtype=F32, name="ffn_down_lo", addend=x1)
    x2 = _mm([dict(a=act, a_blk=1, K=k_half, w=w_down, w_row_blk=1, w_col_blk=0)], N=D, tm=tm_down, tn=tn,
             out_dtype=F32, name="ffn_down_hi", addend=p0)
    return x2


def kernel(x, positions, norm_mix_w, w_in, diff_lambda_q1, diff_lambda_k1, diff_lambda_q2, diff_lambda_k2, diff_subln_w, mla_q_norm_w, mla_w_uq, mla_kv_norm_w, mla_w_ukv, w_o_diff, w_o_mla, w_out, norm_ffn_w, ffn_w_up, ffn_conv_w, ffn_conv_b, ffn_w_down, final_norm_w):
    B, S, D = x.shape
    assert B == 1
    x2d = x.reshape(S, D)
    pos_col = positions.reshape(S, 1)
    for l in range(w_in.shape[0]):
        x2d = _block_forward(
            x2d, pos_col, l, norm_mix_w[l], w_in[l], diff_lambda_q1[l], diff_lambda_k1[l], diff_lambda_q2[l],
            diff_lambda_k2[l], diff_subln_w[l], mla_q_norm_w[l], mla_w_uq[l], mla_kv_norm_w[l], mla_w_ukv[l],
            w_o_diff[l], w_o_mla[l], w_out[l], norm_ffn_w[l], ffn_w_up[l], ffn_conv_w[l], ffn_conv_b[l],
            ffn_w_down[l])
    out = _rmsnorm(x2d, final_norm_w, F32)
    return out.reshape(B, S, D)
```

```python
import functools
import math

import jax
import jax.numpy as jnp
from jax import lax
from jax.experimental import pallas as pl
from jax.experimental.pallas import tpu as pltpu

BF16 = jnp.bfloat16
F32 = jnp.float32

LANES = 128
V7X_VMEM_LIMIT_BYTES = 56 << 20
V7X_VMEM_LIMIT_MAX_BYTES = 62 << 20

DIFF_HEADS = 16
DIFF_HEAD_DIM = 64
MLA_HEADS = 16
MLA_NOPE_DIM = 128
MLA_ROPE_DIM = 64
MLA_V_DIM = 128
ROPE_THETA = 10000.0
NORM_EPS = 1e-6
DIFF_SUBLN_EPS = 1e-5
CONV_WIDTH = 3
MASK_VALUE = -1e30
LOG2_E = math.log2(math.e)


def _cparams(n_axes, vmem_limit_bytes=V7X_VMEM_LIMIT_BYTES):
    return pltpu.CompilerParams(
        dimension_semantics=("arbitrary",) * n_axes,
        vmem_limit_bytes=vmem_limit_bytes,
    )


def _rope_table_kernel(pos_ref, freq_ref, sign_ref, cos_ref, sin_ref):
    ang = pos_ref[...].astype(F32) * freq_ref[...]
    cos_ref[...] = jnp.cos(ang)
    sin_ref[...] = jnp.sin(ang) * sign_ref[...]


def _rope_tables(pos_col, tm=2048):
    S = pos_col.shape[0]
    half = MLA_ROPE_DIM // 2
    inv_freq = ROPE_THETA ** (-jnp.arange(0, MLA_ROPE_DIM, 2, dtype=F32) / MLA_ROPE_DIM)
    freq = jnp.tile(inv_freq, LANES // half).reshape(1, LANES)
    sign = jnp.tile(jnp.concatenate([-jnp.ones((half,), F32), jnp.ones((half,), F32)]),
                    LANES // (2 * half)).reshape(1, LANES)
    return pl.pallas_call(
        _rope_table_kernel,
        grid=(S // tm,),
        in_specs=[pl.BlockSpec((tm, 1), lambda i: (i, 0)),
                  pl.BlockSpec((1, LANES), lambda i: (0, 0)),
                  pl.BlockSpec((1, LANES), lambda i: (0, 0))],
        out_specs=[pl.BlockSpec((tm, LANES), lambda i: (i, 0)),
                   pl.BlockSpec((tm, LANES), lambda i: (i, 0))],
        out_shape=[jax.ShapeDtypeStruct((S, LANES), F32)] * 2,
        compiler_params=_cparams(1),
        name="rope_tables",
    )(pos_col, freq, sign)


def _rope_partner(zc):
    lane = lax.broadcasted_iota(jnp.int32, zc.shape, 1)
    first_half = (lane & 32) == 0
    return jnp.where(first_half, pltpu.roll(zc, 96, 1), pltpu.roll(zc, 32, 1))


def _rope_lanes(z, cos, sin):
    outs = []
    for c in range(z.shape[1] // LANES):
        zc = z[:, c * LANES:(c + 1) * LANES]
        outs.append(zc * cos + _rope_partner(zc) * sin)
    return outs[0] if len(outs) == 1 else jnp.concatenate(outs, axis=1)


def _rmsnorm_kernel(x_ref, w_ref, o_ref, *, eps):
    xf = x_ref[...].astype(F32)
    ms = jnp.mean(xf * xf, axis=1, keepdims=True)
    o_ref[...] = (xf * lax.rsqrt(ms + eps) * w_ref[...]).astype(o_ref.dtype)


def _rmsnorm(x, w, out_dtype, eps=NORM_EPS, tm=512):
    M, D = x.shape
    return pl.pallas_call(
        functools.partial(_rmsnorm_kernel, eps=eps),
        grid=(M // tm,),
        in_specs=[pl.BlockSpec((tm, D), lambda i: (i, 0)),
                  pl.BlockSpec((1, D), lambda i: (0, 0))],
        out_specs=pl.BlockSpec((tm, D), lambda i: (i, 0)),
        out_shape=jax.ShapeDtypeStruct((M, D), out_dtype),
        compiler_params=_cparams(1),
        name="rmsnorm",
    )(x, w.reshape(1, D).astype(F32))


def _tile_walk(ni, n_steps):
    def cur(s):
        c = jnp.minimum(s, n_steps - 1)
        return c % ni, c // ni

    def prev(s):
        p = jnp.maximum(s - 1, 0)
        return p % ni, p // ni

    return cur, prev


MM_SUB_ROWS = 256
MXU_DIM = 256


def _mm_kernel(*refs, n_pairs, has_gate, has_norm, has_addend, has_rope, epilogue, epi_arg, eps, ni, n_steps,
               tm, sub, w_transposed):
    refs = list(refs)
    pair_refs = []
    for _ in range(n_pairs):
        a_ref = refs.pop(0)
        w_ref = refs.pop(0)
        g_ref = refs.pop(0) if has_gate else None
        pair_refs.append((a_ref, w_ref, g_ref))
    nw_ref = refs.pop(0) if has_norm else None
    add_ref = refs.pop(0) if has_addend else None
    cos_ref = refs.pop(0) if has_rope else None
    sin_ref = refs.pop(0) if has_rope else None
    o_ref = refs.pop(0)
    wbf_refs = refs[:n_pairs]
    raw_refs = refs[n_pairs:]

    s = pl.program_id(0)
    cur, prev = _tile_walk(ni, n_steps)
    i_cur, _ = cur(s)
    _, j = prev(s)

    @pl.when(s == 0)
    def _():
        for raw in raw_refs:
            raw[...] = jnp.zeros(raw.shape, F32)

    @pl.when(i_cur == 0)
    def _():
        for (_, w_ref, _), wbf in zip(pair_refs, wbf_refs):
            if w_transposed:
                eye = _eye_bf16(MXU_DIM)
                for kc in range(wbf.shape[0] // MXU_DIM):
                    ks = slice(kc * MXU_DIM, (kc + 1) * MXU_DIM)
                    wbf[ks, :] = _nt_dot(eye, w_ref[:, ks].astype(BF16)).astype(BF16)
            else:
                wbf[...] = w_ref[...].astype(BF16)

    def finish_previous(rows, roped):
        if has_gate:
            acc = None
            for (_, _, g_ref), raw in zip(pair_refs, raw_refs):
                d = raw[rows, :] * g_ref[rows, :].astype(F32)
                acc = d if acc is None else acc + d
        else:
            acc = raw_refs[0][rows, :]
        if has_addend:
            acc = acc + add_ref[rows, :]
        if epilogue == "sigmoid":
            out = 1.0 / (1.0 + jnp.exp(-acc))
        elif epilogue == "rope_lt":
            _, n_scaled, scale = epi_arg
            out = acc
            if roped:
                out = _rope_lanes(acc, cos_ref[rows, :], sin_ref[rows, :]) * jnp.where(j < n_scaled, scale, 1.0)
        elif epilogue == "scale_rope_ge":
            out = acc * epi_arg[0]
            if roped:
                out = _rope_lanes(out, cos_ref[rows, :], sin_ref[rows, :])
        elif epilogue == "krope_dup":
            lane = lax.broadcasted_iota(jnp.int32, acc.shape, 1)
            kr = jnp.where(lane < MLA_ROPE_DIM, acc, 0.0)
            r = kr * cos_ref[rows, :] + _rope_partner(kr) * sin_ref[rows, :]
            out = r + pltpu.roll(r, MLA_ROPE_DIM, 1)
        else:
            out = acc
        o_ref[rows, :] = out.astype(o_ref.dtype)

    def multiply_current(rows):
        acc = None
        for p, ((a_ref, _, _), wbf) in enumerate(zip(pair_refs, wbf_refs)):
            a = a_ref[rows, :]
            if has_norm:
                af = a.astype(F32)
                ms = jnp.mean(af * af, axis=1, keepdims=True)
                a = (af * lax.rsqrt(ms + eps) * nw_ref[...]).astype(BF16)
            d = jnp.dot(a, wbf[...], preferred_element_type=F32)
            if has_gate:
                raw_refs[p][rows, :] = d
            else:
                acc = d if acc is None else acc + d
        if not has_gate:
            raw_refs[0][rows, :] = acc

    def step(roped):
        for r in range(tm // sub):
            rows = slice(r * sub, (r + 1) * sub)
            finish_previous(rows, roped)
            multiply_current(rows)

    if epilogue == "rope_lt":
        pl.when(j < epi_arg[0])(lambda: step(True))
        pl.when(j >= epi_arg[0])(lambda: step(False))
    elif epilogue == "scale_rope_ge":
        pl.when(j >= epi_arg[1])(lambda: step(True))
        pl.when(j < epi_arg[1])(lambda: step(False))
    else:
        step(False)


def _mm(pairs, *, N, tm, tn, out_dtype, name, epilogue="none", epi_arg=None,
        norm_w=None, addend=None, rope=None, eps=NORM_EPS, w_transposed=False):
    M = pairs[0]["a"].shape[0]
    ni, nj = M // tm, N // tn
    n_steps = ni * nj
    cur, prev = _tile_walk(ni, n_steps)
    has_gate = pairs[0].get("gate") is not None
    args, in_specs, wbf_scratch = [], [], []

    def at_cur(fn):
        return lambda s: fn(*cur(s))

    def at_prev(fn):
        return lambda s: fn(*prev(s))

    for p in pairs:
        K = p["K"]
        args.append(p["a"])
        in_specs.append(pl.BlockSpec((tm, K), at_cur(functools.partial(lambda i, j, b: (i, b), b=p["a_blk"]))))
        args.append(p["w"])
        if not w_transposed:
            in_specs.append(pl.BlockSpec((K, tn), at_cur(functools.partial(
                lambda i, j, r, c: (r, c + j), r=p["w_row_blk"], c=p["w_col_blk"]))))
        elif "w_elem_off" in p:
            in_specs.append(pl.BlockSpec((pl.Element(tn), pl.Element(K)), at_cur(functools.partial(
                lambda i, j, r, off, k: (pl.multiple_of(off + tn * j, math.gcd(off, tn)), r * k),
                r=p["w_row_blk"], off=p["w_elem_off"], k=K))))
        else:
            in_specs.append(pl.BlockSpec((tn, K), at_cur(functools.partial(
                lambda i, j, r, c: (c + j, r), r=p["w_row_blk"], c=p["w_col_blk"]))))
        if has_gate:
            g, g_off = p["gate"]
            args.append(g)
            in_specs.append(pl.BlockSpec((tm, tn), at_prev(functools.partial(lambda i, j, c: (i, c + j), c=g_off))))
        wbf_scratch.append(pltpu.VMEM((K, tn), BF16))
    if norm_w is not None:
        args.append(norm_w.reshape(1, -1).astype(F32))
        in_specs.append(pl.BlockSpec((1, norm_w.shape[-1]), lambda s: (0, 0)))
    if addend is not None:
        args.append(addend)
        in_specs.append(pl.BlockSpec((tm, tn), at_prev(lambda i, j: (i, j))))
    if rope is not None:
        for t in rope:
            args.append(t)
            in_specs.append(pl.BlockSpec((tm, LANES), at_prev(lambda i, j: (i, 0))))
    raw_scratch = [pltpu.VMEM((tm, tn), F32)] * (len(pairs) if has_gate else 1)
    kern = functools.partial(
        _mm_kernel, n_pairs=len(pairs), has_gate=has_gate, has_norm=norm_w is not None,
        has_addend=addend is not None, has_rope=rope is not None, epilogue=epilogue, epi_arg=epi_arg, eps=eps,
        ni=ni, n_steps=n_steps, tm=tm, sub=min(MM_SUB_ROWS, tm), w_transposed=w_transposed)
    return pl.pallas_call(
        kern,
        grid=(n_steps + 1,),
        in_specs=in_specs,
        out_specs=pl.BlockSpec((tm, tn), at_prev(lambda i, j: (i, j))),
        out_shape=jax.ShapeDtypeStruct((M, N), out_dtype),
        scratch_shapes=wbf_scratch + raw_scratch,
        compiler_params=_cparams(1),
        name=name,
    )(*args)


CONV_HALO = 8


def _ffn_up_kernel(a_ref, wg_ref, wv_ref, cwg_ref, cwv_ref, cbg_ref, cbv_ref, o_ref,
                   wgbf, wvbf, ug_buf, uv_buf, *, tm, ni, n_steps):
    s = pl.program_id(0)
    cur, _ = _tile_walk(ni, n_steps)
    i_cur, _ = cur(s)

    @pl.when(s == 0)
    def _():
        ug_buf[...] = jnp.zeros(ug_buf.shape, F32)
        uv_buf[...] = jnp.zeros(uv_buf.shape, F32)

    @pl.when(i_cur == 0)
    def _():
        wgbf[...] = wg_ref[...].astype(BF16)
        wvbf[...] = wv_ref[...].astype(BF16)

    sub = min(MM_SUB_ROWS, tm)

    def conv(buf, r0, cw_ref, cb_ref):
        lo = CONV_HALO + r0
        out = cb_ref[...] + cw_ref[0:1, :] * buf[lo - 2:lo - 2 + sub, :]
        out = out + cw_ref[1:2, :] * buf[lo - 1:lo - 1 + sub, :]
        return out + cw_ref[2:3, :] * buf[lo:lo + sub, :]

    keep = jnp.where(i_cur == 0, 0.0, 1.0)
    halo_g = ug_buf[tm:tm + CONV_HALO, :] * keep
    halo_v = uv_buf[tm:tm + CONV_HALO, :] * keep

    for r in reversed(range(tm // sub)):
        r0 = r * sub
        g = conv(ug_buf, r0, cwg_ref, cbg_ref)
        v = conv(uv_buf, r0, cwv_ref, cbv_ref)
        o_ref[r0:r0 + sub, :] = (g / (1.0 + jnp.exp(-g)) * v).astype(o_ref.dtype)
        a = a_ref[r0:r0 + sub, :]
        ug_buf[CONV_HALO + r0:CONV_HALO + r0 + sub, :] = jnp.dot(a, wgbf[...], preferred_element_type=F32)
        uv_buf[CONV_HALO + r0:CONV_HALO + r0 + sub, :] = jnp.dot(a, wvbf[...], preferred_element_type=F32)

    ug_buf[0:CONV_HALO, :] = halo_g
    uv_buf[0:CONV_HALO, :] = halo_v


def _ffn_up(h, w_up, conv_w, conv_b, *, tm, tn):
    M, K = h.shape
    d_ff = w_up.shape[1] // 2
    ni, nj = M // tm, d_ff // tn
    n_steps = ni * nj
    cur, prev = _tile_walk(ni, n_steps)
    cb = conv_b.reshape(1, -1)

    def at_cur(fn):
        return lambda s: fn(*cur(s))

    def at_prev(fn):
        return lambda s: fn(*prev(s))

    return pl.pallas_call(
        functools.partial(_ffn_up_kernel, tm=tm, ni=ni, n_steps=n_steps),
        grid=(n_steps + 1,),
        in_specs=[
            pl.BlockSpec((tm, K), at_cur(lambda i, j: (i, 0))),
            pl.BlockSpec((K, tn), at_cur(lambda i, j: (0, j))),
            pl.BlockSpec((K, tn), at_cur(lambda i, j: (0, nj + j))),
            pl.BlockSpec((CONV_WIDTH, tn), at_prev(lambda i, j: (0, j))),
            pl.BlockSpec((CONV_WIDTH, tn), at_prev(lambda i, j: (0, nj + j))),
            pl.BlockSpec((1, tn), at_prev(lambda i, j: (0, j))),
            pl.BlockSpec((1, tn), at_prev(lambda i, j: (0, nj + j))),
        ],
        out_specs=pl.BlockSpec((tm, tn), at_prev(lambda i, j: (i, j))),
        out_shape=jax.ShapeDtypeStruct((M, d_ff), BF16),
        scratch_shapes=[pltpu.VMEM((K, tn), BF16), pltpu.VMEM((K, tn), BF16),
                        pltpu.VMEM((CONV_HALO + tm, tn), F32), pltpu.VMEM((CONV_HALO + tm, tn), F32)],
        compiler_params=_cparams(1, V7X_VMEM_LIMIT_MAX_BYTES),
        name="ffn_up_conv_gate",
    )(h, w_up, w_up, conv_w, conv_w, cb, cb)


DIFF_CHUNK = 256
MLA_CHUNK = 256
DIFF_STAGE_ORDER = "q|sp"
MLA_STAGE_ORDER = "q|sp"


def _nt_dot(a, b):
    return lax.dot_general(a, b, (((1,), (1,)), ((), ())), preferred_element_type=F32)


def _eye_bf16(n):
    r = lax.broadcasted_iota(jnp.int32, (n, n), 0)
    c = lax.broadcasted_iota(jnp.int32, (n, n), 1)
    return jnp.where(r == c, 1.0, 0.0).astype(BF16)


def _transpose_bf16(x, eye):
    return _nt_dot(eye, x).astype(BF16)


SUBLANES = 8
REDUCE_WAYS = 8


def _reduce_rows(x, op, final):
    n = x.shape[0]
    groups = [x[r * SUBLANES:(r + 1) * SUBLANES] for r in range(n // SUBLANES)]
    ways = min(REDUCE_WAYS, len(groups))
    parts = groups[:ways]
    for g, blk in enumerate(groups[ways:]):
        parts[g % ways] = op(parts[g % ways], blk)
    while len(parts) > 1:
        parts = [op(parts[i], parts[i + 1]) if i + 1 < len(parts) else parts[i] for i in range(0, len(parts), 2)]
    return final(parts[0], axis=0, keepdims=True)


def _softmax_chunk(load_s, m_ref, c, mask_q0):
    def scores():
        sT = load_s()
        if mask_q0 is not None:
            key = lax.broadcasted_iota(jnp.int32, sT.shape, 0)
            qq = lax.broadcasted_iota(jnp.int32, sT.shape, 1) + mask_q0
            sT = jnp.where(qq >= key, sT, MASK_VALUE)
        return sT

    m_prev = m_ref[c]
    m_new = jnp.maximum(m_prev, _reduce_rows(scores(), jnp.maximum, jnp.max))
    p = jnp.exp2(scores() - m_new)
    alpha = jnp.exp2(m_prev - m_new)
    m_ref[c] = m_new
    return p.astype(BF16), alpha


ONES_ROWS = 16


def _causal_attn_loop(qi, kv_refs, qT_ref, s_ref, p_ref, a_ref, m_ref, acc_ref, *, T, n_maps, order):
    CW = qT_ref.shape[2]
    per_map = T // CW
    chunks = [(c, (c % per_map) * CW) for c in range(n_maps * per_map)]
    m_ref[...] = jnp.full(m_ref.shape, MASK_VALUE, F32)
    acc_ref[...] = jnp.zeros(acc_ref.shape, F32)

    maps_per_kv = n_maps // len(kv_refs)

    def kv_of(c):
        return kv_refs[c // (per_map * maps_per_kv)]

    def qk(blk, slot, c):
        k = kv_of(c)[0](pl.multiple_of(blk * T, T))
        s_ref[slot, c] = jnp.dot(k, qT_ref[c], preferred_element_type=F32)

    def softmax(slot, c, q0, diagonal):
        n = q0 + CW if diagonal else T
        p, alpha = _softmax_chunk(lambda: s_ref[slot, c, 0:n, :], m_ref, c, q0 if diagonal else None)
        p_ref[slot, c, 0:n, :] = p
        a_ref[slot, c] = alpha

    def pv(blk, slot, c, q0, diagonal):
        n = q0 + CW if diagonal else T
        acc_ref[c] = acc_ref[c] * a_ref[slot, c] + jnp.dot(
            kv_of(c)[1][blk, :, 0:n], p_ref[slot, c, 0:n, :], preferred_element_type=F32)

    def step(t, slot):
        stage = {"q": lambda c, q0: qk(t, slot, c),
                 "s": lambda c, q0: softmax(1 - slot, c, q0, False),
                 "p": lambda c, q0: pv(t - 2, slot, c, q0, False)}
        for group in order.split("|"):
            for c, q0 in chunks:
                for name in group:
                    stage[name](c, q0)

    def drain(slot):
        for c, q0 in chunks:
            softmax(slot, c, q0, True)
            pv(qi, slot, c, q0, True)

    for c, _ in chunks:
        qk(0, 0, c)

    @pl.when(qi == 0)
    def _():
        drain(0)

    @pl.when(qi >= 1)
    def _():
        for c, q0 in chunks:
            qk(1, 1, c)
            softmax(0, c, q0, False)

    def body(u, carry):
        t = 2 + 2 * u
        step(t, 0)
        step(t + 1, 1)
        return carry

    lax.fori_loop(0, lax.shift_right_arithmetic(qi - 1, 1), body, 0)

    @pl.when(jnp.logical_and(qi >= 2, qi % 2 == 0))
    def _():
        step(qi, 0)
        for c, q0 in chunks:
            pv(qi - 1, 1, c, q0, False)
        drain(0)

    @pl.when(qi % 2 == 1)
    def _():
        for c, q0 in chunks:
            pv(qi - 1, 0, c, q0, False)
        drain(1)


DIFF_HEADS_PER_STEP = 1


def _diff_attn_kernel(q_ref, k_ref, v_ref, lq1_ref, lk1_ref, lq2_ref, lk2_ref, sw_ref, o_ref,
                      vT_ref, qT_ref, s_ref, p_ref, a_ref, m_ref, acc_ref, o_stage, *, T, lam_init):
    eye = _eye_bf16(LANES)
    hd = 2 * DIFF_HEAD_DIM
    dv = hd
    CW = qT_ref.shape[2]
    per_map = T // CW
    E = DIFF_HEADS_PER_STEP
    for e in range(E):
        for jb in range(vT_ref.shape[1]):
            vT_ref[e, jb, 0:dv, :] = _transpose_bf16(v_ref[jb * T:(jb + 1) * T, e * hd:(e + 1) * hd], eye)
            vT_ref[e, jb, dv:dv + ONES_ROWS, :] = jnp.ones((ONES_ROWS, T), BF16)
    lam = (jnp.exp(jnp.sum(lq1_ref[...] * lk1_ref[...], axis=1, keepdims=True))
           - jnp.exp(jnp.sum(lq2_ref[...] * lk2_ref[...], axis=1, keepdims=True)) + lam_init)
    kv_refs = [(functools.partial(lambda row, e: k_ref[pl.ds(row, T), :][:, e * hd:(e + 1) * hd], e=e),
                vT_ref.at[e]) for e in range(E)]

    def q_block(qi, carry):
        row0 = pl.multiple_of(qi * T, T)
        q_all = q_ref[pl.ds(row0, T), :]
        for e in range(E):
            q = q_all[:, e * hd:(e + 1) * hd]
            lane = lax.broadcasted_iota(jnp.int32, q.shape, 1)
            zero = jnp.zeros_like(q)
            for half, keep in enumerate((lane < DIFF_HEAD_DIM, lane >= DIFF_HEAD_DIM)):
                qh = jnp.where(keep, q, zero)
                for c in range(per_map):
                    qT_ref[(2 * e + half) * per_map + c] = _transpose_bf16(qh[c * CW:(c + 1) * CW, :], eye)

        _causal_attn_loop(qi, kv_refs, qT_ref, s_ref, p_ref, a_ref, m_ref, acc_ref, T=T, n_maps=2 * E,
                          order=DIFF_STAGE_ORDER)

        for e in range(E):
            for c in range(per_map):
                c1, c2 = 2 * e * per_map + c, (2 * e + 1) * per_map + c
                o1 = acc_ref[c1, 0:dv, :] / acc_ref[c1, dv:dv + 1, :]
                o2 = acc_ref[c2, 0:dv, :] / acc_ref[c2, dv:dv + 1, :]
                odT = o1 - lam * o2
                ms = jnp.mean(odT * odT, axis=0, keepdims=True)
                outT = (odT * lax.rsqrt(ms + DIFF_SUBLN_EPS) * sw_ref[...] * (1.0 - lam_init)).astype(BF16)
                for r in range(CW // LANES):
                    r0 = c * CW + r * LANES
                    o_stage[r0:r0 + LANES, e * hd:(e + 1) * hd] = _transpose_bf16(
                        outT[:, r * LANES:(r + 1) * LANES], eye).astype(o_stage.dtype)
        o_ref[pl.ds(row0, T), :] = o_stage[...]
        return carry

    lax.fori_loop(0, q_ref.shape[0] // T, q_block, 0)


def _diff_attn(zqkv, lq1, lk1, lq2, lk2, subln_w, *, lam_init, T):
    S = zqkv.shape[0]
    H = DIFF_HEADS
    E = DIFF_HEADS_PER_STEP
    hd = 2 * DIFF_HEAD_DIM
    cw = min(DIFF_CHUNK, T)
    nc = E * 2 * (T // cw)
    vec = lambda a: a.reshape(1, -1).astype(F32)
    small = lambda n: pl.BlockSpec((1, n), lambda g: (0, 0))
    return pl.pallas_call(
        functools.partial(_diff_attn_kernel, T=T, lam_init=lam_init),
        grid=(H // E,),
        in_specs=[
            pl.BlockSpec((S, E * hd), lambda g: (0, g)),
            pl.BlockSpec((S, E * hd), lambda g: (0, H // E + g)),
            pl.BlockSpec((S, E * hd), lambda g: (0, 2 * (H // E) + g)),
            small(DIFF_HEAD_DIM), small(DIFF_HEAD_DIM), small(DIFF_HEAD_DIM), small(DIFF_HEAD_DIM),
            pl.BlockSpec((hd, 1), lambda g: (0, 0)),
        ],
        out_specs=pl.BlockSpec((S, E * hd), lambda g: (0, g)),
        out_shape=jax.ShapeDtypeStruct((S, H * hd), BF16),
        scratch_shapes=[pltpu.VMEM((E, S // T, hd + ONES_ROWS, T), BF16), pltpu.VMEM((nc, hd, cw), BF16),
                        pltpu.VMEM((2, nc, T, cw), F32), pltpu.VMEM((2, nc, T, cw), BF16),
                        pltpu.VMEM((2, nc, 1, cw), F32), pltpu.VMEM((nc, 1, cw), F32),
                        pltpu.VMEM((nc, hd + ONES_ROWS, cw), F32), pltpu.VMEM((T, E * hd), BF16)],
        compiler_params=_cparams(1),
        name="diff_attention",
    )(zqkv, zqkv, zqkv, vec(lq1), vec(lk1), vec(lq2), vec(lk2), subln_w.reshape(-1, 1).astype(F32))


MLA_HEADS_PER_STEP = 2


def _mla_attn_kernel(qn_ref, qr_ref, kv_ref, kr_ref, o_ref, kcat, vT_ref, qT_ref, s_ref, p_ref, a_ref,
                     m_ref, acc_ref, o_stage, *, T):
    eye = _eye_bf16(LANES)
    dv = MLA_V_DIM
    CW = qT_ref.shape[2]
    per_map = T // CW
    pair_w = MLA_NOPE_DIM + MLA_V_DIM

    for e in range(MLA_HEADS_PER_STEP):
        kcat[e, :, 0:MLA_NOPE_DIM] = kv_ref[:, e * pair_w:e * pair_w + MLA_NOPE_DIM]
        kcat[e, :, MLA_NOPE_DIM:] = kr_ref[...]
        for jb in range(vT_ref.shape[1]):
            v_blk = kv_ref[jb * T:(jb + 1) * T, e * pair_w + MLA_NOPE_DIM:(e + 1) * pair_w]
            vT_ref[e, jb, 0:dv, :] = _transpose_bf16(v_blk, eye)
            vT_ref[e, jb, dv:dv + ONES_ROWS, :] = jnp.ones((ONES_ROWS, T), BF16)

    def q_block(qi, carry):
        row0 = pl.multiple_of(qi * T, T)
        qr = qr_ref[pl.ds(row0, T), :]
        lane = lax.broadcasted_iota(jnp.int32, qr.shape, 1)
        qn_all = qn_ref[pl.ds(row0, T), :]
        for e in range(MLA_HEADS_PER_STEP):
            mine = jnp.logical_and(lane >= e * MLA_ROPE_DIM, lane < (e + 1) * MLA_ROPE_DIM)
            qn = qn_all[:, e * MLA_NOPE_DIM:(e + 1) * MLA_NOPE_DIM]
            qrm = jnp.where(mine, qr, jnp.zeros_like(qr))
            for c in range(per_map):
                qT_ref[e * per_map + c, 0:MLA_NOPE_DIM, :] = _transpose_bf16(qn[c * CW:(c + 1) * CW, :], eye)
                qT_ref[e * per_map + c, MLA_NOPE_DIM:, :] = _transpose_bf16(qrm[c * CW:(c + 1) * CW, :], eye)

        kv_refs = [(functools.partial(lambda row, e: kcat[e, pl.ds(row, T), :], e=e), vT_ref.at[e])
                   for e in range(MLA_HEADS_PER_STEP)]
        _causal_attn_loop(qi, kv_refs, qT_ref, s_ref, p_ref, a_ref, m_ref, acc_ref, T=T,
                          n_maps=MLA_HEADS_PER_STEP, order=MLA_STAGE_ORDER)

        for e in range(MLA_HEADS_PER_STEP):
            for c in range(per_map):
                cc = e * per_map + c
                oT = (acc_ref[cc, 0:dv, :] / acc_ref[cc, dv:dv + 1, :]).astype(BF16)
                for r in range(CW // LANES):
                    r0 = c * CW + r * LANES
                    o_stage[r0:r0 + LANES, e * dv:(e + 1) * dv] = _transpose_bf16(
                        oT[:, r * LANES:(r + 1) * LANES], eye).astype(o_stage.dtype)
        o_ref[pl.ds(row0, T), :] = o_stage[...]
        return carry

    lax.fori_loop(0, qn_ref.shape[0] // T, q_block, 0)


def _mla_attn(qm, kv, kr_dup, *, T):
    S = qm.shape[0]
    H = MLA_HEADS
    E = MLA_HEADS_PER_STEP
    cw = min(MLA_CHUNK, T)
    nc = E * (T // cw)
    return pl.pallas_call(
        functools.partial(_mla_attn_kernel, T=T),
        grid=(H // E,),
        in_specs=[
            pl.BlockSpec((S, E * MLA_NOPE_DIM), lambda g: (0, g)),
            pl.BlockSpec((S, LANES), lambda g: (0, H + g)),
            pl.BlockSpec((S, E * (MLA_NOPE_DIM + MLA_V_DIM)), lambda g: (0, g)),
            pl.BlockSpec((S, LANES), lambda g: (0, 0)),
        ],
        out_specs=pl.BlockSpec((S, E * MLA_V_DIM), lambda g: (0, g)),
        out_shape=jax.ShapeDtypeStruct((S, H * MLA_V_DIM), BF16),
        scratch_shapes=[pltpu.VMEM((E, S, MLA_NOPE_DIM + LANES), BF16),
                        pltpu.VMEM((E, S // T, MLA_V_DIM + ONES_ROWS, T), BF16),
                        pltpu.VMEM((nc, MLA_NOPE_DIM + LANES, cw), BF16),
                        pltpu.VMEM((2, nc, T, cw), F32), pltpu.VMEM((2, nc, T, cw), BF16),
                        pltpu.VMEM((2, nc, 1, cw), F32), pltpu.VMEM((nc, 1, cw), F32),
                        pltpu.VMEM((nc, MLA_V_DIM + ONES_ROWS, cw), F32),
                        pltpu.VMEM((T, E * MLA_V_DIM), BF16)],
        compiler_params=_cparams(1, V7X_VMEM_LIMIT_MAX_BYTES),
        name="mla_attention",
    )(qm, qm, kv, kr_dup)


def _block_forward(x2d, pos_col, l, norm_mix_w, w_in, lq1, lk1, lq2, lk2, subln_w, q_norm_w, w_uq,
                   kv_norm_w, w_ukv, w_o_diff, w_o_mla, w_out, norm_ffn_w, w_up, conv_w, conv_b, w_down,
                   *, tm=1024, tn=512, t_attn=512, tn_ffn=256, tm_down=512, tm_lat=2048, tn_lat=1024,
                   tm_ffn=2048):
    S, D = x2d.shape
    H = DIFF_HEADS
    qkv_w = 3 * H * 2 * DIFF_HEAD_DIM
    q_rank = w_uq.shape[0]
    kv_rank = w_ukv.shape[0]
    lat_w = q_rank + kv_rank
    main_w = qkv_w + lat_w
    gate_start = main_w + MLA_ROPE_DIM
    lam_init = 0.8 - 0.6 * math.exp(-0.3 * l)

    cos, sin = _rope_tables(pos_col)
    h = _rmsnorm(x2d, norm_mix_w, BF16)

    w_in_t = jnp.swapaxes(w_in, 0, 1)
    z = _mm([dict(a=h, a_blk=0, K=D, w=w_in_t, w_row_blk=0, w_col_blk=0)], N=main_w, tm=tm, tn=tn,
            out_dtype=BF16, name="in_proj_main", epilogue="rope_lt",
            epi_arg=((2 * H * 2 * DIFF_HEAD_DIM) // tn, (H * 2 * DIFF_HEAD_DIM) // tn,
                     DIFF_HEAD_DIM ** -0.5 * LOG2_E), rope=(cos, sin), w_transposed=True)
    kr_dup = _mm([dict(a=h, a_blk=0, K=D, w=w_in_t, w_row_blk=0, w_col_blk=main_w // LANES)], N=LANES,
                 tm=tm, tn=LANES, out_dtype=BF16, name="in_proj_krope", epilogue="krope_dup",
                 rope=(cos, sin), w_transposed=True)
    gates = _mm([dict(a=h, a_blk=0, K=D, w=w_in_t, w_row_blk=0, w_elem_off=gate_start)], N=2 * D,
                tm=tm, tn=tn, out_dtype=BF16, name="in_proj_gates", epilogue="sigmoid",
                w_transposed=True)

    o_d = _diff_attn(z, lq1, lk1, lq2, lk2, subln_w, lam_init=lam_init, T=t_attn)

    qk_dim = MLA_NOPE_DIM + MLA_ROPE_DIM
    w_uq3 = w_uq.reshape(q_rank, MLA_HEADS, qk_dim)
    w_uq_perm = jnp.concatenate([w_uq3[:, :, :MLA_NOPE_DIM].reshape(q_rank, -1),
                                 w_uq3[:, :, MLA_NOPE_DIM:].reshape(q_rank, -1)], axis=1)
    qm = _mm([dict(a=z, a_blk=qkv_w // q_rank, K=q_rank, w=w_uq_perm, w_row_blk=0, w_col_blk=0)],
             N=MLA_HEADS * qk_dim, tm=tm_lat, tn=tn_lat, out_dtype=BF16, name="mla_q_up", norm_w=q_norm_w,
             epilogue="scale_rope_ge",
             epi_arg=(qk_dim ** -0.5 * LOG2_E, (MLA_HEADS * MLA_NOPE_DIM) // tn_lat), rope=(cos, sin))
    kv = _mm([dict(a=z, a_blk=(qkv_w + q_rank) // kv_rank, K=kv_rank, w=w_ukv, w_row_blk=0, w_col_blk=0)],
             N=w_ukv.shape[1], tm=tm_lat, tn=tn_lat, out_dtype=BF16, name="mla_kv_up", norm_w=kv_norm_w)
    o_m = _mla_attn(qm, kv, kr_dup, T=t_attn)

    y = _mm([dict(a=o_d, a_blk=0, K=o_d.shape[1], w=w_o_diff, w_row_blk=0, w_col_blk=0, gate=(gates, 0)),
             dict(a=o_m, a_blk=0, K=o_m.shape[1], w=w_o_mla, w_row_blk=0, w_col_blk=0, gate=(gates, D // tn))],
            N=D, tm=tm, tn=tn, out_dtype=BF16, name="branch_merge")
    x1 = _mm([dict(a=y, a_blk=0, K=D, w=w_out, w_row_blk=0, w_col_blk=0)], N=D, tm=tm, tn=tn,
             out_dtype=F32, name="out_proj", addend=x2d)

    h2 = _rmsnorm(x1, norm_ffn_w, BF16)
    act = _ffn_up(h2, w_up, conv_w, conv_b, tm=tm_ffn, tn=tn_ffn)
    d_ff = act.shape[1]
    k_half = d_ff // 2
    p0 = _mm([dict(a=act, a_blk=0, K=k_half, w=w_down, w_row_blk=0, w_col_blk=0)], N=D, tm=tm_down, tn=tn,
             out_dtype=F32, name="ffn_down_lo", addend=x1)
    x2 = _mm([dict(a=act, a_blk=1, K=k_half, w=w_down, w_row_blk=1, w_col_blk=0)], N=D, tm=tm_down, tn=tn,
             out_dtype=F32, name="ffn_down_hi", addend=p0)
    return x2


def kernel(x, positions, norm_mix_w, w_in, diff_lambda_q1, diff_lambda_k1, diff_lambda_q2, diff_lambda_k2, diff_subln_w, mla_q_norm_w, mla_w_uq, mla_kv_norm_w, mla_w_ukv, w_o_diff, w_o_mla, w_out, norm_ffn_w, ffn_w_up, ffn_conv_w, ffn_conv_b, ffn_w_down, final_norm_w):
    B, S, D = x.shape
    assert B == 1
    x2d = x.reshape(S, D)
    pos_col = positions.reshape(S, 1)
    for l in range(w_in.shape[0]):
        x2d = _block_forward(
            x2d, pos_col, l, norm_mix_w[l], w_in[l], diff_lambda_q1[l], diff_lambda_k1[l], diff_lambda_q2[l],
            diff_lambda_k2[l], diff_subln_w[l], mla_q_norm_w[l], mla_w_uq[l], mla_kv_norm_w[l], mla_w_ukv[l],
            w_o_diff[l], w_o_mla[l], w_out[l], norm_ffn_w[l], ffn_w_up[l], ffn_conv_w[l], ffn_conv_b[l],
            ffn_w_down[l])
    out = _rmsnorm(x2d, final_norm_w, F32)
    return out.reshape(B, S, D)
```

```python
import functools
import math

import jax
import jax.numpy as jnp
from jax import lax
from jax.experimental import pallas as pl
from jax.experimental.pallas import tpu as pltpu

BF16 = jnp.bfloat16
F32 = jnp.float32

LANES = 128
V7X_VMEM_LIMIT_BYTES = 56 << 20
V7X_VMEM_LIMIT_MAX_BYTES = 62 << 20

DIFF_HEADS = 16
DIFF_HEAD_DIM = 64
MLA_HEADS = 16
MLA_NOPE_DIM = 128
MLA_ROPE_DIM = 64
MLA_V_DIM = 128
ROPE_THETA = 10000.0
NORM_EPS = 1e-6
DIFF_SUBLN_EPS = 1e-5
CONV_WIDTH = 3
MASK_VALUE = -1e30
LOG2_E = math.log2(math.e)


def _cparams(n_axes, vmem_limit_bytes=V7X_VMEM_LIMIT_BYTES):
    return pltpu.CompilerParams(
        dimension_semantics=("arbitrary",) * n_axes,
        vmem_limit_bytes=vmem_limit_bytes,
    )


def _rope_table_kernel(pos_ref, freq_ref, sign_ref, cos_ref, sin_ref):
    ang = pos_ref[...].astype(F32) * freq_ref[...]
    cos_ref[...] = jnp.cos(ang)
    sin_ref[...] = jnp.sin(ang) * sign_ref[...]


def _rope_tables(pos_col, tm=2048):
    S = pos_col.shape[0]
    half = MLA_ROPE_DIM // 2
    inv_freq = ROPE_THETA ** (-jnp.arange(0, MLA_ROPE_DIM, 2, dtype=F32) / MLA_ROPE_DIM)
    freq = jnp.tile(inv_freq, LANES // half).reshape(1, LANES)
    sign = jnp.tile(jnp.concatenate([-jnp.ones((half,), F32), jnp.ones((half,), F32)]),
                    LANES // (2 * half)).reshape(1, LANES)
    return pl.pallas_call(
        _rope_table_kernel,
        grid=(S // tm,),
        in_specs=[pl.BlockSpec((tm, 1), lambda i: (i, 0)),
                  pl.BlockSpec((1, LANES), lambda i: (0, 0)),
                  pl.BlockSpec((1, LANES), lambda i: (0, 0))],
        out_specs=[pl.BlockSpec((tm, LANES), lambda i: (i, 0)),
                   pl.BlockSpec((tm, LANES), lambda i: (i, 0))],
        out_shape=[jax.ShapeDtypeStruct((S, LANES), F32)] * 2,
        compiler_params=_cparams(1),
        name="rope_tables",
    )(pos_col, freq, sign)


def _rope_partner(zc):
    lane = lax.broadcasted_iota(jnp.int32, zc.shape, 1)
    first_half = (lane & 32) == 0
    return jnp.where(first_half, pltpu.roll(zc, 96, 1), pltpu.roll(zc, 32, 1))


def _rope_lanes(z, cos, sin):
    outs = []
    for c in range(z.shape[1] // LANES):
        zc = z[:, c * LANES:(c + 1) * LANES]
        outs.append(zc * cos + _rope_partner(zc) * sin)
    return outs[0] if len(outs) == 1 else jnp.concatenate(outs, axis=1)


def _rmsnorm_kernel(x_ref, w_ref, o_ref, *, eps):
    xf = x_ref[...].astype(F32)
    ms = jnp.mean(xf * xf, axis=1, keepdims=True)
    o_ref[...] = (xf * lax.rsqrt(ms + eps) * w_ref[...]).astype(o_ref.dtype)


def _rmsnorm(x, w, out_dtype, eps=NORM_EPS, tm=512):
    M, D = x.shape
    return pl.pallas_call(
        functools.partial(_rmsnorm_kernel, eps=eps),
        grid=(M // tm,),
        in_specs=[pl.BlockSpec((tm, D), lambda i: (i, 0)),
                  pl.BlockSpec((1, D), lambda i: (0, 0))],
        out_specs=pl.BlockSpec((tm, D), lambda i: (i, 0)),
        out_shape=jax.ShapeDtypeStruct((M, D), out_dtype),
        compiler_params=_cparams(1),
        name="rmsnorm",
    )(x, w.reshape(1, D).astype(F32))


def _tile_walk(ni, n_steps):
    def cur(s):
        c = jnp.minimum(s, n_steps - 1)
        return c % ni, c // ni

    def prev(s):
        p = jnp.maximum(s - 1, 0)
        return p % ni, p // ni

    return cur, prev


MM_SUB_ROWS = 256
MXU_DIM = 256


def _mm_kernel(*refs, n_pairs, has_gate, has_norm, has_addend, has_rope, epilogue, epi_arg, eps, ni, n_steps,
               tm, sub, w_transposed):
    refs = list(refs)
    pair_refs = []
    for _ in range(n_pairs):
        a_ref = refs.pop(0)
        w_ref = refs.pop(0)
        g_ref = refs.pop(0) if has_gate else None
        pair_refs.append((a_ref, w_ref, g_ref))
    nw_ref = refs.pop(0) if has_norm else None
    add_ref = refs.pop(0) if has_addend else None
    cos_ref = refs.pop(0) if has_rope else None
    sin_ref = refs.pop(0) if has_rope else None
    o_ref = refs.pop(0)
    wbf_refs = refs[:n_pairs]
    raw_refs = refs[n_pairs:]

    s = pl.program_id(0)
    cur, prev = _tile_walk(ni, n_steps)
    i_cur, _ = cur(s)
    _, j = prev(s)

    @pl.when(s == 0)
    def _():
        for raw in raw_refs:
            raw[...] = jnp.zeros(raw.shape, F32)

    @pl.when(i_cur == 0)
    def _():
        for (_, w_ref, _), wbf in zip(pair_refs, wbf_refs):
            if w_transposed:
                eye = _eye_bf16(MXU_DIM)
                for kc in range(wbf.shape[0] // MXU_DIM):
                    ks = slice(kc * MXU_DIM, (kc + 1) * MXU_DIM)
                    wbf[ks, :] = _nt_dot(eye, w_ref[:, ks].astype(BF16)).astype(BF16)
            else:
                wbf[...] = w_ref[...].astype(BF16)

    def finish_previous(rows, roped):
        if has_gate:
            acc = None
            for (_, _, g_ref), raw in zip(pair_refs, raw_refs):
                d = raw[rows, :] * g_ref[rows, :].astype(F32)
                acc = d if acc is None else acc + d
        else:
            acc = raw_refs[0][rows, :]
        if has_addend:
            acc = acc + add_ref[rows, :]
        if epilogue == "sigmoid":
            out = 1.0 / (1.0 + jnp.exp(-acc))
        elif epilogue == "rope_lt":
            _, n_scaled, scale = epi_arg
            out = acc
            if roped:
                out = _rope_lanes(acc, cos_ref[rows, :], sin_ref[rows, :]) * jnp.where(j < n_scaled, scale, 1.0)
        elif epilogue == "scale_rope_ge":
            out = acc * epi_arg[0]
            if roped:
                out = _rope_lanes(out, cos_ref[rows, :], sin_ref[rows, :])
        elif epilogue == "krope_dup":
            lane = lax.broadcasted_iota(jnp.int32, acc.shape, 1)
            kr = jnp.where(lane < MLA_ROPE_DIM, acc, 0.0)
            r = kr * cos_ref[rows, :] + _rope_partner(kr) * sin_ref[rows, :]
            out = r + pltpu.roll(r, MLA_ROPE_DIM, 1)
        else:
            out = acc
        o_ref[rows, :] = out.astype(o_ref.dtype)

    def multiply_current(rows):
        acc = None
        for p, ((a_ref, _, _), wbf) in enumerate(zip(pair_refs, wbf_refs)):
            a = a_ref[rows, :]
            if has_norm:
                af = a.astype(F32)
                ms = jnp.mean(af * af, axis=1, keepdims=True)
                a = (af * lax.rsqrt(ms + eps) * nw_ref[...]).astype(BF16)
            d = jnp.dot(a, wbf[...], preferred_element_type=F32)
            if has_gate:
                raw_refs[p][rows, :] = d
            else:
                acc = d if acc is None else acc + d
        if not has_gate:
            raw_refs[0][rows, :] = acc

    def step(roped):
        for r in range(tm // sub):
            rows = slice(r * sub, (r + 1) * sub)
            finish_previous(rows, roped)
            multiply_current(rows)

    if epilogue == "rope_lt":
        pl.when(j < epi_arg[0])(lambda: step(True))
        pl.when(j >= epi_arg[0])(lambda: step(False))
    elif epilogue == "scale_rope_ge":
        pl.when(j >= epi_arg[1])(lambda: step(True))
        pl.when(j < epi_arg[1])(lambda: step(False))
    else:
        step(False)


def _mm(pairs, *, N, tm, tn, out_dtype, name, epilogue="none", epi_arg=None,
        norm_w=None, addend=None, rope=None, eps=NORM_EPS, w_transposed=False,
        vmem_limit_bytes=V7X_VMEM_LIMIT_BYTES):
    M = pairs[0]["a"].shape[0]
    ni, nj = M // tm, N // tn
    n_steps = ni * nj
    cur, prev = _tile_walk(ni, n_steps)
    has_gate = pairs[0].get("gate") is not None
    args, in_specs, wbf_scratch = [], [], []

    def at_cur(fn):
        return lambda s: fn(*cur(s))

    def at_prev(fn):
        return lambda s: fn(*prev(s))

    for p in pairs:
        K = p["K"]
        args.append(p["a"])
        in_specs.append(pl.BlockSpec((tm, K), at_cur(functools.partial(lambda i, j, b: (i, b), b=p["a_blk"]))))
        args.append(p["w"])
        if not w_transposed:
            in_specs.append(pl.BlockSpec((K, tn), at_cur(functools.partial(
                lambda i, j, r, c: (r, c + j), r=p["w_row_blk"], c=p["w_col_blk"]))))
        elif "w_elem_off" in p:
            in_specs.append(pl.BlockSpec((pl.Element(tn), pl.Element(K)), at_cur(functools.partial(
                lambda i, j, r, off, k: (pl.multiple_of(off + tn * j, math.gcd(off, tn)), r * k),
                r=p["w_row_blk"], off=p["w_elem_off"], k=K))))
        else:
            in_specs.append(pl.BlockSpec((tn, K), at_cur(functools.partial(
                lambda i, j, r, c: (c + j, r), r=p["w_row_blk"], c=p["w_col_blk"]))))
        if has_gate:
            g, g_off = p["gate"]
            args.append(g)
            in_specs.append(pl.BlockSpec((tm, tn), at_prev(functools.partial(lambda i, j, c: (i, c + j), c=g_off))))
        wbf_scratch.append(pltpu.VMEM((K, tn), BF16))
    if norm_w is not None:
        args.append(norm_w.reshape(1, -1).astype(F32))
        in_specs.append(pl.BlockSpec((1, norm_w.shape[-1]), lambda s: (0, 0)))
    if addend is not None:
        args.append(addend)
        in_specs.append(pl.BlockSpec((tm, tn), at_prev(lambda i, j: (i, j))))
    if rope is not None:
        for t in rope:
            args.append(t)
            in_specs.append(pl.BlockSpec((tm, LANES), at_prev(lambda i, j: (i, 0))))
    raw_scratch = [pltpu.VMEM((tm, tn), F32)] * (len(pairs) if has_gate else 1)
    kern = functools.partial(
        _mm_kernel, n_pairs=len(pairs), has_gate=has_gate, has_norm=norm_w is not None,
        has_addend=addend is not None, has_rope=rope is not None, epilogue=epilogue, epi_arg=epi_arg, eps=eps,
        ni=ni, n_steps=n_steps, tm=tm, sub=min(MM_SUB_ROWS, tm), w_transposed=w_transposed)
    return pl.pallas_call(
        kern,
        grid=(n_steps + 1,),
        in_specs=in_specs,
        out_specs=pl.BlockSpec((tm, tn), at_prev(lambda i, j: (i, j))),
        out_shape=jax.ShapeDtypeStruct((M, N), out_dtype),
        scratch_shapes=wbf_scratch + raw_scratch,
        compiler_params=_cparams(1, vmem_limit_bytes),
        name=name,
    )(*args)


CONV_HALO = 8


def _ffn_up_kernel(a_ref, wg_ref, wv_ref, cwg_ref, cwv_ref, cbg_ref, cbv_ref, o_ref,
                   wgbf, wvbf, ug_buf, uv_buf, *, tm, ni, n_steps):
    s = pl.program_id(0)
    cur, _ = _tile_walk(ni, n_steps)
    i_cur, _ = cur(s)

    @pl.when(s == 0)
    def _():
        ug_buf[...] = jnp.zeros(ug_buf.shape, F32)
        uv_buf[...] = jnp.zeros(uv_buf.shape, F32)

    @pl.when(i_cur == 0)
    def _():
        wgbf[...] = wg_ref[...].astype(BF16)
        wvbf[...] = wv_ref[...].astype(BF16)

    sub = min(MM_SUB_ROWS, tm)

    def conv(buf, r0, cw_ref, cb_ref):
        lo = CONV_HALO + r0
        out = cb_ref[...] + cw_ref[0:1, :] * buf[lo - 2:lo - 2 + sub, :]
        out = out + cw_ref[1:2, :] * buf[lo - 1:lo - 1 + sub, :]
        return out + cw_ref[2:3, :] * buf[lo:lo + sub, :]

    keep = jnp.where(i_cur == 0, 0.0, 1.0)
    halo_g = ug_buf[tm:tm + CONV_HALO, :] * keep
    halo_v = uv_buf[tm:tm + CONV_HALO, :] * keep

    for r in reversed(range(tm // sub)):
        r0 = r * sub
        g = conv(ug_buf, r0, cwg_ref, cbg_ref)
        v = conv(uv_buf, r0, cwv_ref, cbv_ref)
        o_ref[r0:r0 + sub, :] = (g / (1.0 + jnp.exp(-g)) * v).astype(o_ref.dtype)
        a = a_ref[r0:r0 + sub, :]
        ug_buf[CONV_HALO + r0:CONV_HALO + r0 + sub, :] = jnp.dot(a, wgbf[...], preferred_element_type=F32)
        uv_buf[CONV_HALO + r0:CONV_HALO + r0 + sub, :] = jnp.dot(a, wvbf[...], preferred_element_type=F32)

    ug_buf[0:CONV_HALO, :] = halo_g
    uv_buf[0:CONV_HALO, :] = halo_v


def _ffn_up(h, w_up, conv_w, conv_b, *, tm, tn):
    M, K = h.shape
    d_ff = w_up.shape[1] // 2
    ni, nj = M // tm, d_ff // tn
    n_steps = ni * nj
    cur, prev = _tile_walk(ni, n_steps)
    cb = conv_b.reshape(1, -1)

    def at_cur(fn):
        return lambda s: fn(*cur(s))

    def at_prev(fn):
        return lambda s: fn(*prev(s))

    return pl.pallas_call(
        functools.partial(_ffn_up_kernel, tm=tm, ni=ni, n_steps=n_steps),
        grid=(n_steps + 1,),
        in_specs=[
            pl.BlockSpec((tm, K), at_cur(lambda i, j: (i, 0))),
            pl.BlockSpec((K, tn), at_cur(lambda i, j: (0, j))),
            pl.BlockSpec((K, tn), at_cur(lambda i, j: (0, nj + j))),
            pl.BlockSpec((CONV_WIDTH, tn), at_prev(lambda i, j: (0, j))),
            pl.BlockSpec((CONV_WIDTH, tn), at_prev(lambda i, j: (0, nj + j))),
            pl.BlockSpec((1, tn), at_prev(lambda i, j: (0, j))),
            pl.BlockSpec((1, tn), at_prev(lambda i, j: (0, nj + j))),
        ],
        out_specs=pl.BlockSpec((tm, tn), at_prev(lambda i, j: (i, j))),
        out_shape=jax.ShapeDtypeStruct((M, d_ff), BF16),
        scratch_shapes=[pltpu.VMEM((K, tn), BF16), pltpu.VMEM((K, tn), BF16),
                        pltpu.VMEM((CONV_HALO + tm, tn), F32), pltpu.VMEM((CONV_HALO + tm, tn), F32)],
        compiler_params=_cparams(1, V7X_VMEM_LIMIT_MAX_BYTES),
        name="ffn_up_conv_gate",
    )(h, w_up, w_up, conv_w, conv_w, cb, cb)


DIFF_CHUNK = 256
MLA_CHUNK = 256
DIFF_STAGE_ORDER = "q|sp"
MLA_STAGE_ORDER = "q|sp"


def _nt_dot(a, b):
    return lax.dot_general(a, b, (((1,), (1,)), ((), ())), preferred_element_type=F32)


def _eye_bf16(n):
    r = lax.broadcasted_iota(jnp.int32, (n, n), 0)
    c = lax.broadcasted_iota(jnp.int32, (n, n), 1)
    return jnp.where(r == c, 1.0, 0.0).astype(BF16)


def _transpose_bf16(x, eye):
    return _nt_dot(eye, x).astype(BF16)


SUBLANES = 8
REDUCE_WAYS = 8


def _reduce_rows(x, op, final):
    n = x.shape[0]
    groups = [x[r * SUBLANES:(r + 1) * SUBLANES] for r in range(n // SUBLANES)]
    ways = min(REDUCE_WAYS, len(groups))
    parts = groups[:ways]
    for g, blk in enumerate(groups[ways:]):
        parts[g % ways] = op(parts[g % ways], blk)
    while len(parts) > 1:
        parts = [op(parts[i], parts[i + 1]) if i + 1 < len(parts) else parts[i] for i in range(0, len(parts), 2)]
    return final(parts[0], axis=0, keepdims=True)


def _softmax_chunk(load_s, m_ref, c, mask_q0):
    def scores():
        sT = load_s()
        if mask_q0 is not None:
            key = lax.broadcasted_iota(jnp.int32, sT.shape, 0)
            qq = lax.broadcasted_iota(jnp.int32, sT.shape, 1) + mask_q0
            sT = jnp.where(qq >= key, sT, MASK_VALUE)
        return sT

    m_prev = m_ref[c]
    m_new = jnp.maximum(m_prev, _reduce_rows(scores(), jnp.maximum, jnp.max))
    p = jnp.exp2(scores() - m_new)
    alpha = jnp.exp2(m_prev - m_new)
    m_ref[c] = m_new
    return p.astype(BF16), alpha


ONES_ROWS = 16


def _causal_attn_loop(qi, kv_refs, qT_ref, s_ref, p_ref, a_ref, m_ref, acc_ref, *, T, n_maps, order):
    CW = qT_ref.shape[2]
    per_map = T // CW
    chunks = [(c, (c % per_map) * CW) for c in range(n_maps * per_map)]
    m_ref[...] = jnp.full(m_ref.shape, MASK_VALUE, F32)
    acc_ref[...] = jnp.zeros(acc_ref.shape, F32)

    maps_per_kv = n_maps // len(kv_refs)

    def kv_of(c):
        return kv_refs[c // (per_map * maps_per_kv)]

    def qk(blk, slot, c):
        k = kv_of(c)[0](pl.multiple_of(blk * T, T))
        s_ref[slot, c] = jnp.dot(k, qT_ref[c], preferred_element_type=F32)

    def softmax(slot, c, q0, diagonal):
        n = q0 + CW if diagonal else T
        p, alpha = _softmax_chunk(lambda: s_ref[slot, c, 0:n, :], m_ref, c, q0 if diagonal else None)
        p_ref[slot, c, 0:n, :] = p
        a_ref[slot, c] = alpha

    def pv(blk, slot, c, q0, diagonal):
        n = q0 + CW if diagonal else T
        acc_ref[c] = acc_ref[c] * a_ref[slot, c] + jnp.dot(
            kv_of(c)[1][blk, :, 0:n], p_ref[slot, c, 0:n, :], preferred_element_type=F32)

    def step(t, slot):
        stage = {"q": lambda c, q0: qk(t, slot, c),
                 "s": lambda c, q0: softmax(1 - slot, c, q0, False),
                 "p": lambda c, q0: pv(t - 2, slot, c, q0, False)}
        for group in order.split("|"):
            for c, q0 in chunks:
                for name in group:
                    stage[name](c, q0)

    def drain(slot):
        for c, q0 in chunks:
            softmax(slot, c, q0, True)
            pv(qi, slot, c, q0, True)

    for c, _ in chunks:
        qk(0, 0, c)

    @pl.when(qi == 0)
    def _():
        drain(0)

    @pl.when(qi >= 1)
    def _():
        for c, q0 in chunks:
            qk(1, 1, c)
            softmax(0, c, q0, False)

    def body(u, carry):
        t = 2 + 2 * u
        step(t, 0)
        step(t + 1, 1)
        return carry

    lax.fori_loop(0, lax.shift_right_arithmetic(qi - 1, 1), body, 0)

    @pl.when(jnp.logical_and(qi >= 2, qi % 2 == 0))
    def _():
        step(qi, 0)
        for c, q0 in chunks:
            pv(qi - 1, 1, c, q0, False)
        drain(0)

    @pl.when(qi % 2 == 1)
    def _():
        for c, q0 in chunks:
            pv(qi - 1, 0, c, q0, False)
        drain(1)


DIFF_HEADS_PER_STEP = 1


def _diff_attn_kernel(q_ref, k_ref, v_ref, lq1_ref, lk1_ref, lq2_ref, lk2_ref, sw_ref, o_ref,
                      vT_ref, qT_ref, s_ref, p_ref, a_ref, m_ref, acc_ref, o_stage, *, T, lam_init):
    eye = _eye_bf16(LANES)
    hd = 2 * DIFF_HEAD_DIM
    dv = hd
    CW = qT_ref.shape[2]
    per_map = T // CW
    E = DIFF_HEADS_PER_STEP
    for e in range(E):
        for jb in range(vT_ref.shape[1]):
            vT_ref[e, jb, 0:dv, :] = _transpose_bf16(v_ref[jb * T:(jb + 1) * T, e * hd:(e + 1) * hd], eye)
            vT_ref[e, jb, dv:dv + ONES_ROWS, :] = jnp.ones((ONES_ROWS, T), BF16)
    lam = (jnp.exp(jnp.sum(lq1_ref[...] * lk1_ref[...], axis=1, keepdims=True))
           - jnp.exp(jnp.sum(lq2_ref[...] * lk2_ref[...], axis=1, keepdims=True)) + lam_init)
    kv_refs = [(functools.partial(lambda row, e: k_ref[pl.ds(row, T), :][:, e * hd:(e + 1) * hd], e=e),
                vT_ref.at[e]) for e in range(E)]

    def q_block(qi, carry):
        row0 = pl.multiple_of(qi * T, T)
        q_all = q_ref[pl.ds(row0, T), :]
        for e in range(E):
            q = q_all[:, e * hd:(e + 1) * hd]
            lane = lax.broadcasted_iota(jnp.int32, q.shape, 1)
            zero = jnp.zeros_like(q)
            for half, keep in enumerate((lane < DIFF_HEAD_DIM, lane >= DIFF_HEAD_DIM)):
                qh = jnp.where(keep, q, zero)
                for c in range(per_map):
                    qT_ref[(2 * e + half) * per_map + c] = _transpose_bf16(qh[c * CW:(c + 1) * CW, :], eye)

        _causal_attn_loop(qi, kv_refs, qT_ref, s_ref, p_ref, a_ref, m_ref, acc_ref, T=T, n_maps=2 * E,
                          order=DIFF_STAGE_ORDER)

        for e in range(E):
            for c in range(per_map):
                c1, c2 = 2 * e * per_map + c, (2 * e + 1) * per_map + c
                o1 = acc_ref[c1, 0:dv, :] / acc_ref[c1, dv:dv + 1, :]
                o2 = acc_ref[c2, 0:dv, :] / acc_ref[c2, dv:dv + 1, :]
                odT = o1 - lam * o2
                ms = jnp.mean(odT * odT, axis=0, keepdims=True)
                outT = (odT * lax.rsqrt(ms + DIFF_SUBLN_EPS) * sw_ref[...] * (1.0 - lam_init)).astype(BF16)
                for r in range(CW // LANES):
                    r0 = c * CW + r * LANES
                    o_stage[r0:r0 + LANES, e * hd:(e + 1) * hd] = _transpose_bf16(
                        outT[:, r * LANES:(r + 1) * LANES], eye).astype(o_stage.dtype)
        o_ref[pl.ds(row0, T), :] = o_stage[...]
        return carry

    lax.fori_loop(0, q_ref.shape[0] // T, q_block, 0)


def _diff_attn(zqkv, lq1, lk1, lq2, lk2, subln_w, *, lam_init, T):
    S = zqkv.shape[0]
    H = DIFF_HEADS
    E = DIFF_HEADS_PER_STEP
    hd = 2 * DIFF_HEAD_DIM
    cw = min(DIFF_CHUNK, T)
    nc = E * 2 * (T // cw)
    vec = lambda a: a.reshape(1, -1).astype(F32)
    small = lambda n: pl.BlockSpec((1, n), lambda g: (0, 0))
    return pl.pallas_call(
        functools.partial(_diff_attn_kernel, T=T, lam_init=lam_init),
        grid=(H // E,),
        in_specs=[
            pl.BlockSpec((S, E * hd), lambda g: (0, g)),
            pl.BlockSpec((S, E * hd), lambda g: (0, H // E + g)),
            pl.BlockSpec((S, E * hd), lambda g: (0, 2 * (H // E) + g)),
            small(DIFF_HEAD_DIM), small(DIFF_HEAD_DIM), small(DIFF_HEAD_DIM), small(DIFF_HEAD_DIM),
            pl.BlockSpec((hd, 1), lambda g: (0, 0)),
        ],
        out_specs=pl.BlockSpec((S, E * hd), lambda g: (0, g)),
        out_shape=jax.ShapeDtypeStruct((S, H * hd), BF16),
        scratch_shapes=[pltpu.VMEM((E, S // T, hd + ONES_ROWS, T), BF16), pltpu.VMEM((nc, hd, cw), BF16),
                        pltpu.VMEM((2, nc, T, cw), F32), pltpu.VMEM((2, nc, T, cw), BF16),
                        pltpu.VMEM((2, nc, 1, cw), F32), pltpu.VMEM((nc, 1, cw), F32),
                        pltpu.VMEM((nc, hd + ONES_ROWS, cw), F32), pltpu.VMEM((T, E * hd), BF16)],
        compiler_params=_cparams(1),
        name="diff_attention",
    )(zqkv, zqkv, zqkv, vec(lq1), vec(lk1), vec(lq2), vec(lk2), subln_w.reshape(-1, 1).astype(F32))


MLA_HEADS_PER_STEP = 2


def _mla_attn_kernel(qn_ref, qr_ref, kv_ref, kr_ref, o_ref, kcat, vT_ref, qT_ref, s_ref, p_ref, a_ref,
                     m_ref, acc_ref, o_stage, *, T):
    eye = _eye_bf16(LANES)
    dv = MLA_V_DIM
    CW = qT_ref.shape[2]
    per_map = T // CW
    pair_w = MLA_NOPE_DIM + MLA_V_DIM

    for e in range(MLA_HEADS_PER_STEP):
        kcat[e, :, 0:MLA_NOPE_DIM] = kv_ref[:, e * pair_w:e * pair_w + MLA_NOPE_DIM]
        kcat[e, :, MLA_NOPE_DIM:] = kr_ref[...]
        for jb in range(vT_ref.shape[1]):
            v_blk = kv_ref[jb * T:(jb + 1) * T, e * pair_w + MLA_NOPE_DIM:(e + 1) * pair_w]
            vT_ref[e, jb, 0:dv, :] = _transpose_bf16(v_blk, eye)
            vT_ref[e, jb, dv:dv + ONES_ROWS, :] = jnp.ones((ONES_ROWS, T), BF16)

    def q_block(qi, carry):
        row0 = pl.multiple_of(qi * T, T)
        qr = qr_ref[pl.ds(row0, T), :]
        lane = lax.broadcasted_iota(jnp.int32, qr.shape, 1)
        qn_all = qn_ref[pl.ds(row0, T), :]
        for e in range(MLA_HEADS_PER_STEP):
            mine = jnp.logical_and(lane >= e * MLA_ROPE_DIM, lane < (e + 1) * MLA_ROPE_DIM)
            qn = qn_all[:, e * MLA_NOPE_DIM:(e + 1) * MLA_NOPE_DIM]
            qrm = jnp.where(mine, qr, jnp.zeros_like(qr))
            for c in range(per_map):
                qT_ref[e * per_map + c, 0:MLA_NOPE_DIM, :] = _transpose_bf16(qn[c * CW:(c + 1) * CW, :], eye)
                qT_ref[e * per_map + c, MLA_NOPE_DIM:, :] = _transpose_bf16(qrm[c * CW:(c + 1) * CW, :], eye)

        kv_refs = [(functools.partial(lambda row, e: kcat[e, pl.ds(row, T), :], e=e), vT_ref.at[e])
                   for e in range(MLA_HEADS_PER_STEP)]
        _causal_attn_loop(qi, kv_refs, qT_ref, s_ref, p_ref, a_ref, m_ref, acc_ref, T=T,
                          n_maps=MLA_HEADS_PER_STEP, order=MLA_STAGE_ORDER)

        for e in range(MLA_HEADS_PER_STEP):
            for c in range(per_map):
                cc = e * per_map + c
                oT = (acc_ref[cc, 0:dv, :] / acc_ref[cc, dv:dv + 1, :]).astype(BF16)
                for r in range(CW // LANES):
                    r0 = c * CW + r * LANES
                    o_stage[r0:r0 + LANES, e * dv:(e + 1) * dv] = _transpose_bf16(
                        oT[:, r * LANES:(r + 1) * LANES], eye).astype(o_stage.dtype)
        o_ref[pl.ds(row0, T), :] = o_stage[...]
        return carry

    lax.fori_loop(0, qn_ref.shape[0] // T, q_block, 0)


def _mla_attn(qm, kv, kr_dup, *, T):
    S = qm.shape[0]
    H = MLA_HEADS
    E = MLA_HEADS_PER_STEP
    cw = min(MLA_CHUNK, T)
    nc = E * (T // cw)
    return pl.pallas_call(
        functools.partial(_mla_attn_kernel, T=T),
        grid=(H // E,),
        in_specs=[
            pl.BlockSpec((S, E * MLA_NOPE_DIM), lambda g: (0, g)),
            pl.BlockSpec((S, LANES), lambda g: (0, H + g)),
            pl.BlockSpec((S, E * (MLA_NOPE_DIM + MLA_V_DIM)), lambda g: (0, g)),
            pl.BlockSpec((S, LANES), lambda g: (0, 0)),
        ],
        out_specs=pl.BlockSpec((S, E * MLA_V_DIM), lambda g: (0, g)),
        out_shape=jax.ShapeDtypeStruct((S, H * MLA_V_DIM), BF16),
        scratch_shapes=[pltpu.VMEM((E, S, MLA_NOPE_DIM + LANES), BF16),
                        pltpu.VMEM((E, S // T, MLA_V_DIM + ONES_ROWS, T), BF16),
                        pltpu.VMEM((nc, MLA_NOPE_DIM + LANES, cw), BF16),
                        pltpu.VMEM((2, nc, T, cw), F32), pltpu.VMEM((2, nc, T, cw), BF16),
                        pltpu.VMEM((2, nc, 1, cw), F32), pltpu.VMEM((nc, 1, cw), F32),
                        pltpu.VMEM((nc, MLA_V_DIM + ONES_ROWS, cw), F32),
                        pltpu.VMEM((T, E * MLA_V_DIM), BF16)],
        compiler_params=_cparams(1, V7X_VMEM_LIMIT_MAX_BYTES),
        name="mla_attention",
    )(qm, qm, kv, kr_dup)


def _block_forward(x2d, pos_col, l, norm_mix_w, w_in, lq1, lk1, lq2, lk2, subln_w, q_norm_w, w_uq,
                   kv_norm_w, w_ukv, w_o_diff, w_o_mla, w_out, norm_ffn_w, w_up, conv_w, conv_b, w_down,
                   *, tm=1024, tn=512, t_attn=512, tn_ffn=256, tm_down=1024, tm_lat=2048, tn_lat=1024,
                   tm_ffn=2048):
    S, D = x2d.shape
    H = DIFF_HEADS
    qkv_w = 3 * H * 2 * DIFF_HEAD_DIM
    q_rank = w_uq.shape[0]
    kv_rank = w_ukv.shape[0]
    lat_w = q_rank + kv_rank
    main_w = qkv_w + lat_w
    gate_start = main_w + MLA_ROPE_DIM
    lam_init = 0.8 - 0.6 * math.exp(-0.3 * l)

    cos, sin = _rope_tables(pos_col)
    h = _rmsnorm(x2d, norm_mix_w, BF16)

    w_in_t = jnp.swapaxes(w_in, 0, 1)
    z = _mm([dict(a=h, a_blk=0, K=D, w=w_in_t, w_row_blk=0, w_col_blk=0)], N=main_w, tm=tm, tn=tn,
            out_dtype=BF16, name="in_proj_main", epilogue="rope_lt",
            epi_arg=((2 * H * 2 * DIFF_HEAD_DIM) // tn, (H * 2 * DIFF_HEAD_DIM) // tn,
                     DIFF_HEAD_DIM ** -0.5 * LOG2_E), rope=(cos, sin), w_transposed=True)
    kr_dup = _mm([dict(a=h, a_blk=0, K=D, w=w_in_t, w_row_blk=0, w_col_blk=main_w // LANES)], N=LANES,
                 tm=tm, tn=LANES, out_dtype=BF16, name="in_proj_krope", epilogue="krope_dup",
                 rope=(cos, sin), w_transposed=True)
    gates = _mm([dict(a=h, a_blk=0, K=D, w=w_in_t, w_row_blk=0, w_elem_off=gate_start)], N=2 * D,
                tm=tm, tn=tn, out_dtype=BF16, name="in_proj_gates", epilogue="sigmoid",
                w_transposed=True)

    o_d = _diff_attn(z, lq1, lk1, lq2, lk2, subln_w, lam_init=lam_init, T=t_attn)

    qk_dim = MLA_NOPE_DIM + MLA_ROPE_DIM
    w_uq3 = w_uq.reshape(q_rank, MLA_HEADS, qk_dim)
    w_uq_perm = jnp.concatenate([w_uq3[:, :, :MLA_NOPE_DIM].reshape(q_rank, -1),
                                 w_uq3[:, :, MLA_NOPE_DIM:].reshape(q_rank, -1)], axis=1)
    qm = _mm([dict(a=z, a_blk=qkv_w // q_rank, K=q_rank, w=w_uq_perm, w_row_blk=0, w_col_blk=0)],
             N=MLA_HEADS * qk_dim, tm=tm_lat, tn=tn_lat, out_dtype=BF16, name="mla_q_up", norm_w=q_norm_w,
             epilogue="scale_rope_ge",
             epi_arg=(qk_dim ** -0.5 * LOG2_E, (MLA_HEADS * MLA_NOPE_DIM) // tn_lat), rope=(cos, sin))
    kv = _mm([dict(a=z, a_blk=(qkv_w + q_rank) // kv_rank, K=kv_rank, w=w_ukv, w_row_blk=0, w_col_blk=0)],
             N=w_ukv.shape[1], tm=tm_lat, tn=tn_lat, out_dtype=BF16, name="mla_kv_up", norm_w=kv_norm_w)
    o_m = _mla_attn(qm, kv, kr_dup, T=t_attn)

    y = _mm([dict(a=o_d, a_blk=0, K=o_d.shape[1], w=w_o_diff, w_row_blk=0, w_col_blk=0, gate=(gates, 0)),
             dict(a=o_m, a_blk=0, K=o_m.shape[1], w=w_o_mla, w_row_blk=0, w_col_blk=0, gate=(gates, D // tn))],
            N=D, tm=tm, tn=tn, out_dtype=BF16, name="branch_merge")
    x1 = _mm([dict(a=y, a_blk=0, K=D, w=w_out, w_row_blk=0, w_col_blk=0)], N=D, tm=tm, tn=tn,
             out_dtype=F32, name="out_proj", addend=x2d)

    h2 = _rmsnorm(x1, norm_ffn_w, BF16)
    act = _ffn_up(h2, w_up, conv_w, conv_b, tm=tm_ffn, tn=tn_ffn)
    d_ff = act.shape[1]
    k_half = d_ff // 2
    p0 = _mm([dict(a=act, a_blk=0, K=k_half, w=w_down, w_row_blk=0, w_col_blk=0)], N=D, tm=tm_down, tn=tn,
             out_dtype=F32, name="ffn_down_lo", addend=x1, vmem_limit_bytes=V7X_VMEM_LIMIT_MAX_BYTES)
    x2 = _mm([dict(a=act, a_blk=1, K=k_half, w=w_down, w_row_blk=1, w_col_blk=0)], N=D, tm=tm_down, tn=tn,
             out_dtype=F32, name="ffn_down_hi", addend=p0, vmem_limit_bytes=V7X_VMEM_LIMIT_MAX_BYTES)
    return x2


def kernel(x, positions, norm_mix_w, w_in, diff_lambda_q1, diff_lambda_k1, diff_lambda_q2, diff_lambda_k2, diff_subln_w, mla_q_norm_w, mla_w_uq, mla_kv_norm_w, mla_w_ukv, w_o_diff, w_o_mla, w_out, norm_ffn_w, ffn_w_up, ffn_conv_w, ffn_conv_b, ffn_w_down, final_norm_w):
    B, S, D = x.shape
    assert B == 1
    x2d = x.reshape(S, D)
    pos_col = positions.reshape(S, 1)
    for l in range(w_in.shape[0]):
        x2d = _block_forward(
            x2d, pos_col, l, norm_mix_w[l], w_in[l], diff_lambda_q1[l], diff_lambda_k1[l], diff_lambda_q2[l],
            diff_lambda_k2[l], diff_subln_w[l], mla_q_norm_w[l], mla_w_uq[l], mla_kv_norm_w[l], mla_w_ukv[l],
            w_o_diff[l], w_o_mla[l], w_out[l], norm_ffn_w[l], ffn_w_up[l], ffn_conv_w[l], ffn_conv_b[l],
            ffn_w_down[l])
    out = _rmsnorm(x2d, final_norm_w, F32)
    return out.reshape(B, S, D)
```
